```python
import jax, jax.numpy as jnp
from jax import lax
import numpy as np

D_MODEL = 1024
BATCH = 2
SEQ = 8192
DEPTH = 1

MEM_LEN = 256
EPS = 1e-6
POOL_WINDOWS = (2, 4, 8, 16)
POOL_GROUP = 128
POOL_WIDTH = POOL_GROUP * len(POOL_WINDOWS)
NSA_HEADS = 16
NSA_KV_GROUPS = 2
HEAD_DIM = 64
NSA_WIDTH = NSA_HEADS * HEAD_DIM
NSA_KV_WIDTH = NSA_KV_GROUPS * HEAD_DIM
CMP_BLOCK = 32
CMP_STRIDE = 16
CMP_HIDDEN = 256
SEL_BLOCK = 64
SEL_TOPK = 16
WINDOW = 512
Q_BLOCK = 128
ROPE_THETA = 500000.0
ROT_DIM = HEAD_DIM // 4
XA_HEADS = 4
XA_HEAD_DIM = 128
XA_WIDTH = XA_HEADS * XA_HEAD_DIM
N_BRANCHES = 3
D_FF = 2816
CONV_WIDTH = 3
IN_SIZES = (POOL_WIDTH, NSA_WIDTH, 6 * NSA_KV_WIDTH, 3 * NSA_HEADS, XA_WIDTH, N_BRANCHES * D_MODEL)
D_IN = POOL_WIDTH + NSA_WIDTH + 6 * NSA_KV_WIDTH + 3 * NSA_HEADS + XA_WIDTH + N_BRANCHES * D_MODEL

kernel_name = 'hybrid_pool_nsa_memory_block'


def rmsnorm(x, g):
    xf = x.astype(jnp.float32)
    y = xf * lax.rsqrt(jnp.mean(xf * xf, axis=-1, keepdims=True) + EPS)
    return (y * g.astype(jnp.float32)).astype(x.dtype)


def rotary(x, pos):
    half = ROT_DIM // 2
    inv_freq = ROPE_THETA ** (-jnp.arange(half, dtype=jnp.float32) * (2.0 / ROT_DIM))
    ang = pos.astype(jnp.float32)[..., None] * inv_freq
    cos = jnp.cos(ang)[:, :, None, :]
    sin = jnp.sin(ang)[:, :, None, :]
    xr = x[..., :ROT_DIM].astype(jnp.float32)
    x1, x2 = xr[..., :half], xr[..., half:]
    rot = jnp.concatenate([x1 * cos - x2 * sin, x2 * cos + x1 * sin], axis=-1).astype(x.dtype)
    return jnp.concatenate([rot, x[..., ROT_DIM:]], axis=-1)


def masked_softmax(s, mask):
    s = jnp.where(mask, s, -jnp.inf)
    m = jnp.max(s, axis=-1, keepdims=True)
    m = jnp.where(jnp.isfinite(m), m, 0.0)
    p = jnp.where(mask, jnp.exp(s - m), 0.0)
    return p / jnp.maximum(jnp.sum(p, axis=-1, keepdims=True), jnp.finfo(jnp.float32).tiny)


def multiscale_pool(u, pool_w, pool_scale):
    B, S, _ = u.shape
    uf = u.astype(jnp.float32)
    cs = jnp.concatenate([jnp.zeros((B, 1, POOL_WIDTH), jnp.float32), jnp.cumsum(uf, axis=1)], axis=1)
    t = jnp.arange(S)
    outs = []
    for gi, w in enumerate(POOL_WINDOWS):
        c = cs[:, :, gi * POOL_GROUP:(gi + 1) * POOL_GROUP]
        start = jnp.maximum(t + 1 - w, 0)
        cnt = (t + 1 - start).astype(jnp.float32)
        outs.append((c[:, 1:] - c[:, start]) / cnt[None, :, None] - uf[:, :, gi * POOL_GROUP:(gi + 1) * POOL_GROUP])
    p = jnp.stack(outs, axis=2).astype(u.dtype)
    y = jnp.einsum('bsgc,gcd->bsgd', p, pool_w).reshape(B, S, POOL_WIDTH)
    return y * pool_scale


def compress(kv, pe, w1, w2):
    B, S = kv.shape[:2]
    n_cmp = (S - CMP_BLOCK) // CMP_STRIDE + 1
    idx = jnp.arange(n_cmp)[:, None] * CMP_STRIDE + jnp.arange(CMP_BLOCK)[None, :]
    blk = kv[:, idx] + pe[None, None, :, None, :]
    blk = blk.transpose(0, 1, 3, 2, 4).reshape(B, n_cmp, NSA_KV_GROUPS, CMP_BLOCK * HEAD_DIM)
    return jax.nn.gelu(blk @ w1, approximate=True) @ w2


def nsa_attention(q, k_cmp, v_cmp, k_sel, v_sel, k_win, v_win, gates):
    B, S = q.shape[:2]
    G, HG = NSA_KV_GROUPS, NSA_HEADS // NSA_KV_GROUPS
    n_cmp = k_cmp.shape[1]
    n_sel = S // SEL_BLOCK
    top = min(SEL_TOPK, n_sel)
    nb = S // Q_BLOCK
    cmp_start = jnp.arange(n_cmp) * CMP_STRIDE
    cmp_end = cmp_start + CMP_BLOCK - 1
    sel_start = jnp.arange(n_sel) * SEL_BLOCK
    overlap = ((cmp_start[:, None] < sel_start[None, :] + SEL_BLOCK)
               & (cmp_start[:, None] + CMP_BLOCK > sel_start[None, :])).astype(jnp.float32)
    ks_blocks = k_sel.reshape(B, n_sel, SEL_BLOCK, G, HEAD_DIM).transpose(0, 3, 1, 2, 4)
    vs_blocks = v_sel.reshape(B, n_sel, SEL_BLOCK, G, HEAD_DIM).transpose(0, 3, 1, 2, 4)
    pad = ((0, 0), (WINDOW, 0), (0, 0), (0, 0))
    kw_pad = jnp.pad(k_win, pad)
    vw_pad = jnp.pad(v_win, pad)
    gather = jax.vmap(jax.vmap(lambda blocks, ix: blocks[ix]))
    blk_ids = jnp.arange(n_sel)[None, :]

    def block_fn(args):
        qb, gb, jb = args
        qb = qb.reshape(B, Q_BLOCK, G, HG, HEAD_DIM)
        tq = jb * Q_BLOCK + jnp.arange(Q_BLOCK)
        s_c = jnp.einsum('bqghd,bcgd->bghqc', qb, k_cmp).astype(jnp.float32)
        p_c = masked_softmax(s_c, cmp_end[None, :] <= tq[:, None])
        o_c = jnp.einsum('bghqc,bcgd->bqghd', p_c.astype(v_cmp.dtype), v_cmp)
        imp = jnp.einsum('bghqc,cj->bgqj', p_c, overlap)
        cur = (tq // SEL_BLOCK)[:, None]
        forced = (blk_ids == 0) | (blk_ids == cur) | (blk_ids == cur - 1)
        imp = jnp.where(blk_ids > cur, -jnp.inf, jnp.where(forced, jnp.inf, imp))
        _, idx = lax.top_k(imp, top)
        k_g = gather(ks_blocks, idx).reshape(B, G, Q_BLOCK, top * SEL_BLOCK, HEAD_DIM)
        v_g = gather(vs_blocks, idx).reshape(B, G, Q_BLOCK, top * SEL_BLOCK, HEAD_DIM)
        kpos = (idx[..., None] * SEL_BLOCK + jnp.arange(SEL_BLOCK)).reshape(B, G, Q_BLOCK, top * SEL_BLOCK)
        s_s = jnp.einsum('bqghd,bgqkd->bghqk', qb, k_g).astype(jnp.float32)
        p_s = masked_softmax(s_s, (kpos <= tq[None, None, :, None])[:, :, None])
        o_s = jnp.einsum('bghqk,bgqkd->bqghd', p_s.astype(v_g.dtype), v_g)
        k_w = lax.dynamic_slice_in_dim(kw_pad, jb * Q_BLOCK, WINDOW + Q_BLOCK, axis=1)
        v_w = lax.dynamic_slice_in_dim(vw_pad, jb * Q_BLOCK, WINDOW + Q_BLOCK, axis=1)
        kidx = jb * Q_BLOCK - WINDOW + jnp.arange(WINDOW + Q_BLOCK)
        diff = tq[:, None] - kidx[None, :]
        s_w = jnp.einsum('bqghd,bkgd->bghqk', qb, k_w).astype(jnp.float32)
        p_w = masked_softmax(s_w, (kidx[None, :] >= 0) & (diff >= 0) & (diff < WINDOW))
        o_w = jnp.einsum('bghqk,bkgd->bqghd', p_w.astype(v_w.dtype), v_w)
        gb = gb.reshape(B, Q_BLOCK, G, HG, 3)
        o = gb[..., 0:1] * o_c + gb[..., 1:2] * o_s + gb[..., 2:3] * o_w
        return o.reshape(B, Q_BLOCK, NSA_WIDTH)

    qs = q.reshape(B, nb, Q_BLOCK, NSA_HEADS, HEAD_DIM).swapaxes(0, 1)
    gs = gates.reshape(B, nb, Q_BLOCK, NSA_HEADS, 3).swapaxes(0, 1)
    out = lax.map(block_fn, (qs, gs, jnp.arange(nb)))
    return out.swapaxes(0, 1).reshape(B, S, NSA_WIDTH)


def memory_attention(q, mem_n, w_mem_kv):
    B, M, _ = mem_n.shape
    kv = (mem_n @ w_mem_kv).reshape(B, M, 2, XA_HEADS, XA_HEAD_DIM)
    k, v = kv[:, :, 0], kv[:, :, 1]
    s = jnp.einsum('bshd,bmhd->bhsm', q, k).astype(jnp.float32) * (XA_HEAD_DIM ** -0.5)
    p = jax.nn.softmax(s, axis=-1).astype(v.dtype)
    return jnp.einsum('bhsm,bmhd->bshd', p, v).reshape(B, q.shape[1], XA_WIDTH)


def hybrid_mixer(h, mem, positions, w_in, pool_w, pool_scale, cmp_pe, cmp_w1, cmp_w2, mem_norm_g,
                 w_mem_kv, w_br_pool, w_br_nsa, w_br_xa, w_out):
    B, S, _ = h.shape
    z = h @ w_in
    u_pool, q, kv, g_nsa, q_x, g_br = jnp.split(z, np.cumsum(IN_SIZES)[:-1].tolist(), axis=-1)
    y_pool = multiscale_pool(u_pool, pool_w, pool_scale)
    q = rotary(q.reshape(B, S, NSA_HEADS, HEAD_DIM), positions) * (HEAD_DIM ** -0.5)
    kv = kv.reshape(B, S, 6, NSA_KV_GROUPS, HEAD_DIM)
    k_c = rotary(kv[:, :, 0], positions)
    k_s = rotary(kv[:, :, 2], positions)
    k_w = rotary(kv[:, :, 4], positions)
    kc = compress(k_c, cmp_pe[0], cmp_w1[0], cmp_w2[0])
    vc = compress(kv[:, :, 1], cmp_pe[1], cmp_w1[1], cmp_w2[1])
    nsa_gates = jax.nn.sigmoid(g_nsa.reshape(B, S, NSA_HEADS, 3))
    y_nsa = nsa_attention(q, kc, vc, k_s, kv[:, :, 3], k_w, kv[:, :, 5], nsa_gates)
    y_mem = memory_attention(q_x.reshape(B, S, XA_HEADS, XA_HEAD_DIM), rmsnorm(mem, mem_norm_g), w_mem_kv)
    g = jax.nn.sigmoid(g_br.reshape(B, S, N_BRANCHES, D_MODEL))
    y = g[:, :, 0] * (y_pool @ w_br_pool) + g[:, :, 1] * (y_nsa @ w_br_nsa) + g[:, :, 2] * (y_mem @ w_br_xa)
    return y @ w_out


def conv_ffn(h, w_up, conv_w, conv_b, w_down):
    S = h.shape[1]
    u = h @ w_up
    up = jnp.pad(u, ((0, 0), (CONV_WIDTH - 1, 0), (0, 0)))
    c = conv_b
    for k in range(CONV_WIDTH):
        c = c + conv_w[k] * up[:, k:k + S]
    gate, val = jnp.split(c, 2, axis=-1)
    return (jax.nn.gelu(gate, approximate=True) * val) @ w_down


def setup_inputs(seed: int = 0) -> dict:
    key = jax.random.key(seed)
    ks = jax.random.split(key, 24)
    f32 = jnp.float32

    def nrm(k, shape, scale):
        return jax.random.normal(k, shape, f32) * scale

    def gain(k, shape):
        return 1.0 + 0.05 * jax.random.normal(k, shape, f32)

    L = DEPTH
    x = nrm(ks[0], (BATCH, SEQ, D_MODEL), 1.0)
    mem = nrm(ks[1], (BATCH, MEM_LEN, D_MODEL), 1.0)
    offset = jax.random.randint(ks[2], (BATCH, 1), 0, 4096, dtype=jnp.int32)
    positions = (offset + jnp.arange(SEQ, dtype=jnp.int32)[None, :]).astype(jnp.int32)
    return {
        'x': x,
        'mem': mem,
        'positions': positions,
        'pre_mix_g': gain(ks[3], (L, D_MODEL)),
        'w_in': nrm(ks[4], (L, D_MODEL, D_IN), D_MODEL ** -0.5),
        'pool_w': nrm(ks[5], (L, len(POOL_WINDOWS), POOL_GROUP, POOL_GROUP), POOL_GROUP ** -0.5),
        'pool_scale': 1.0 + 0.1 * jax.random.normal(ks[6], (L, POOL_WIDTH), f32),
        'cmp_pe': nrm(ks[7], (L, 2, CMP_BLOCK, HEAD_DIM), 0.1),
        'cmp_w1': nrm(ks[8], (L, 2, CMP_BLOCK * HEAD_DIM, CMP_HIDDEN), (CMP_BLOCK * HEAD_DIM) ** -0.5),
        'cmp_w2': nrm(ks[9], (L, 2, CMP_HIDDEN, HEAD_DIM), CMP_HIDDEN ** -0.5),
        'mem_norm_g': gain(ks[10], (L, D_MODEL)),
        'w_mem_kv': nrm(ks[11], (L, D_MODEL, 2 * XA_WIDTH), D_MODEL ** -0.5),
        'w_br_pool': nrm(ks[12], (L, POOL_WIDTH, D_MODEL), POOL_WIDTH ** -0.5),
        'w_br_nsa': nrm(ks[13], (L, NSA_WIDTH, D_MODEL), NSA_WIDTH ** -0.5),
        'w_br_xa': nrm(ks[14], (L, XA_WIDTH, D_MODEL), XA_WIDTH ** -0.5),
        'w_out': nrm(ks[15], (L, D_MODEL, D_MODEL), D_MODEL ** -0.5),
        'post_mix_g': gain(ks[16], (L, D_MODEL)),
        'pre_ffn_g': gain(ks[17], (L, D_MODEL)),
        'w_up': nrm(ks[18], (L, D_MODEL, 2 * D_FF), D_MODEL ** -0.5),
        'conv_w': nrm(ks[19], (L, CONV_WIDTH, 2 * D_FF), CONV_WIDTH ** -0.5),
        'conv_b': nrm(ks[20], (L, 2 * D_FF), 0.01),
        'w_down': nrm(ks[21], (L, D_FF, D_MODEL), D_FF ** -0.5),
        'post_ffn_g': gain(ks[22], (L, D_MODEL)),
    }


def reference(x, mem, positions, pre_mix_g, w_in, pool_w, pool_scale, cmp_pe, cmp_w1, cmp_w2,
              mem_norm_g, w_mem_kv, w_br_pool, w_br_nsa, w_br_xa, w_out, post_mix_g, pre_ffn_g,
              w_up, conv_w, conv_b, w_down, post_ffn_g):
    for l in range(DEPTH):
        y = hybrid_mixer(rmsnorm(x, pre_mix_g[l]), mem, positions, w_in[l], pool_w[l], pool_scale[l],
                         cmp_pe[l], cmp_w1[l], cmp_w2[l], mem_norm_g[l], w_mem_kv[l],
                         w_br_pool[l], w_br_nsa[l], w_br_xa[l], w_out[l])
        x = x + rmsnorm(y, post_mix_g[l])
        f = conv_ffn(rmsnorm(x, pre_ffn_g[l]), w_up[l], conv_w[l], conv_b[l], w_down[l])
        x = x + rmsnorm(f, post_ffn_g[l])
    return x
```

```python
import functools

import numpy as np
import jax
import jax.numpy as jnp
from jax import lax
from jax.experimental import pallas as pl
from jax.experimental.pallas import tpu as pltpu

F32 = jnp.float32
BF16 = jnp.bfloat16
HIGHEST = lax.Precision.HIGHEST

D_MODEL = 1024
EPS = 1e-6
POOL_WINDOWS = (2, 4, 8, 16)
POOL_GROUP = 128
POOL_WIDTH = POOL_GROUP * len(POOL_WINDOWS)
NSA_HEADS = 16
NSA_KV_GROUPS = 2
HEADS_PER_GROUP = NSA_HEADS // NSA_KV_GROUPS
HEAD_DIM = 64
NSA_WIDTH = NSA_HEADS * HEAD_DIM
NSA_KV_WIDTH = NSA_KV_GROUPS * HEAD_DIM
CMP_BLOCK = 32
CMP_STRIDE = 16
CMP_HIDDEN = 256
SEL_BLOCK = 64
SEL_TOPK = 16
WINDOW = 512
ROPE_THETA = 500000.0
ROT_DIM = HEAD_DIM // 4
XA_HEADS = 4
XA_HEAD_DIM = 128
XA_WIDTH = XA_HEADS * XA_HEAD_DIM
N_BRANCHES = 3
D_FF = 2816
CONV_WIDTH = 3
IN_SIZES = (POOL_WIDTH, NSA_WIDTH, 6 * NSA_KV_WIDTH, 3 * NSA_HEADS, XA_WIDTH, N_BRANCHES * D_MODEL)

LANES = 128
Q_TILE = 128
SEL_KEY_TILE = 512
ROW_CHUNK = 512
MASK_NEG = -1e9
SCORE_FLOOR = -1e30
VMEM_LIMIT = 56 * 1024 * 1024


def _rms(x, g):
    return x * lax.rsqrt(jnp.mean(x * x, axis=-1, keepdims=True) + EPS) * g


def _gelu_tanh(x):
    return 0.5 * x * (1.0 + jnp.tanh(0.7978845608028654 * (x + 0.044715 * (x * x * x))))


def _dot_nt(a, b, precision=None):
    return lax.dot_general(a, b, (((1,), (1,)), ((), ())), precision=precision,
                           preferred_element_type=F32)


def _resident(shape):
    return pl.BlockSpec(shape, lambda *_: (0,) * len(shape), pipeline_mode=pl.Buffered(1))


def _inproj_kernel(x_ref, pos_ref, g_ref, wq_ref, wkc_ref, wkv_ref, wgn_ref, frq_ref, sg1_ref, sg2_ref,
                   q_ref, kc_ref, vc_ref, ks_ref, vs_ref, kw_ref, vw_ref, gn_ref):
    h = _rms(x_ref[...], g_ref[...])
    hb = h.astype(BF16)
    ang = pos_ref[...].astype(F32) * frq_ref[...]
    c = jnp.cos(ang)
    s = jnp.sin(ang)
    s1 = s * sg1_ref[...]
    s2 = s * sg2_ref[...]

    def rope(v):
        return v * c + pltpu.roll(v, LANES - ROT_DIM // 2, 1) * s1 + pltpu.roll(v, ROT_DIM // 2, 1) * s2

    q = jnp.dot(h, wq_ref[...], precision=HIGHEST, preferred_element_type=F32)
    for j in range(NSA_WIDTH // LANES):
        q_ref[:, j * LANES:(j + 1) * LANES] = rope(q[:, j * LANES:(j + 1) * LANES]) * (HEAD_DIM ** -0.5)
    kc_ref[...] = rope(jnp.dot(h, wkc_ref[...], precision=HIGHEST, preferred_element_type=F32))
    kv = jnp.dot(hb, wkv_ref[...], preferred_element_type=F32)
    vc_ref[...] = kv[:, 0:128]
    ks_ref[...] = rope(kv[:, 128:256]).astype(BF16)
    vs_ref[...] = kv[:, 256:384].astype(BF16)
    kw_ref[...] = rope(kv[:, 384:512]).astype(BF16)
    vw_ref[...] = kv[:, 512:640].astype(BF16)
    gn_ref[...] = jax.nn.sigmoid(jnp.dot(hb, wgn_ref[...], preferred_element_type=F32))


def _in_proj(x2, pos2, g, wq, wkc, wkv, wgn, tm=512):
    T = x2.shape[0]
    half = ROT_DIM // 2
    d = np.arange(LANES) % HEAD_DIM
    inv_freq = ROPE_THETA ** (-(np.arange(half, dtype=np.float64)) * (2.0 / ROT_DIM))
    frq = np.where(d < ROT_DIM, inv_freq[d % half], 0.0).astype(np.float32)[None, :]
    sg1 = np.where(d < half, -1.0, 0.0).astype(np.float32)[None, :]
    sg2 = np.where((d >= half) & (d < ROT_DIM), 1.0, 0.0).astype(np.float32)[None, :]
    row = lambda w: pl.BlockSpec((tm, w), lambda i: (i, 0))
    out_shape = [jax.ShapeDtypeStruct((T, NSA_WIDTH), F32),
                 jax.ShapeDtypeStruct((T, 128), F32), jax.ShapeDtypeStruct((T, 128), F32),
                 jax.ShapeDtypeStruct((T, 128), BF16), jax.ShapeDtypeStruct((T, 128), BF16),
                 jax.ShapeDtypeStruct((T, 128), BF16), jax.ShapeDtypeStruct((T, 128), BF16),
                 jax.ShapeDtypeStruct((T, 128), F32)]
    return pl.pallas_call(
        _inproj_kernel,
        grid=(T // tm,),
        in_specs=[row(D_MODEL), row(1), _resident((1, D_MODEL)), _resident(wq.shape), _resident(wkc.shape),
                  _resident(wkv.shape), _resident(wgn.shape), _resident((1, LANES)), _resident((1, LANES)),
                  _resident((1, LANES))],
        out_specs=[row(NSA_WIDTH)] + [row(128)] * 7,
        out_shape=out_shape,
        compiler_params=pltpu.CompilerParams(dimension_semantics=("parallel",), vmem_limit_bytes=VMEM_LIMIT),
        name="in_proj",
    )(x2, pos2, g, wq, wkc, wkv, wgn, jnp.asarray(frq), jnp.asarray(sg1), jnp.asarray(sg2))


def _compress_kernel(c_ref, pet_ref, peb_ref, wt_ref, wb_ref, w2_ref, o_ref):
    c = c_ref[0]
    a = jnp.dot(c + pet_ref[...], wt_ref[...], precision=HIGHEST, preferred_element_type=F32)
    b = jnp.dot(c + peb_ref[...], wb_ref[...], precision=HIGHEST, preferred_element_type=F32)
    n = c.shape[0]
    hid = a + pltpu.roll(b, n - 1, 0)
    o_ref[0] = jnp.dot(_gelu_tanh(hid), w2_ref[...], precision=HIGHEST, preferred_element_type=F32)


def _compress(kv, pe, w1, w2):
    B, S, _ = kv.shape
    n = S // CMP_STRIDE
    G = NSA_KV_GROUPS
    half = CMP_BLOCK // 2
    eye = jnp.eye(G, dtype=F32)

    def blockdiag_w1(w):
        w = w.reshape(half, 1, HEAD_DIM, 1, CMP_HIDDEN) * eye.reshape(1, G, 1, G, 1)
        return w.reshape(half * G * HEAD_DIM, G * CMP_HIDDEN)

    wt = blockdiag_w1(w1[:half * HEAD_DIM])
    wb = blockdiag_w1(w1[half * HEAD_DIM:])
    w2b = (w2.reshape(1, CMP_HIDDEN, 1, HEAD_DIM) * eye.reshape(G, 1, G, 1)).reshape(G * CMP_HIDDEN, G * HEAD_DIM)
    pet = jnp.broadcast_to(pe[:half, None, :], (half, G, HEAD_DIM)).reshape(1, half * G * HEAD_DIM)
    peb = jnp.broadcast_to(pe[half:, None, :], (half, G, HEAD_DIM)).reshape(1, half * G * HEAD_DIM)
    c = kv.reshape(B, n, CMP_STRIDE * 128)
    return pl.pallas_call(
        _compress_kernel,
        grid=(B,),
        in_specs=[pl.BlockSpec((1, n, CMP_STRIDE * 128), lambda b: (b, 0, 0)),
                  _resident(pet.shape), _resident(peb.shape), _resident(wt.shape), _resident(wb.shape),
                  _resident(w2b.shape)],
        out_specs=pl.BlockSpec((1, n, 128), lambda b: (b, 0, 0)),
        out_shape=jax.ShapeDtypeStruct((B, n, 128), F32),
        compiler_params=pltpu.CompilerParams(dimension_semantics=("parallel",), vmem_limit_bytes=VMEM_LIMIT),
        name="compress",
    )(c, pet, peb, wt, wb, w2b)


def _memkv_kernel(m_ref, g_ref, w_ref, o_ref):
    mn = _rms(m_ref[0], g_ref[...])
    o_ref[0] = jnp.dot(mn.astype(BF16), w_ref[...], preferred_element_type=F32).astype(BF16)


def _mem_kv(mem, g, w):
    B, M, _ = mem.shape
    return pl.pallas_call(
        _memkv_kernel,
        grid=(B,),
        in_specs=[pl.BlockSpec((1, M, D_MODEL), lambda b: (b, 0, 0)), _resident((1, D_MODEL)),
                  _resident(w.shape)],
        out_specs=pl.BlockSpec((1, M, 2 * XA_WIDTH), lambda b: (b, 0, 0)),
        out_shape=jax.ShapeDtypeStruct((B, M, 2 * XA_WIDTH), BF16),
        compiler_params=pltpu.CompilerParams(dimension_semantics=("parallel",), vmem_limit_bytes=VMEM_LIMIT),
        name="mem_kv",
    )(mem, g, w)


def _softmax_rows(s, mask):
    s = jnp.where(mask[None], s, SCORE_FLOOR)
    m = jnp.max(s, axis=-1, keepdims=True)
    p = jnp.where(mask[None], jnp.exp(s - m), 0.0)
    l = jnp.sum(p, axis=-1, keepdims=True)
    return p / jnp.maximum(l, jnp.finfo(F32).tiny)


def _nsa_kernel(q_ref, kcc_ref, vcc_ref, ks_ref, vs_ref, kw_ref, vw_ref, gn_ref, ov_ref, o_ref,
                qs_ref, qa_ref, oc_ref, os_ref, ow_ref, m_ref, l_ref):
    qt = pl.program_id(1)
    q0 = qt * Q_TILE
    n_cmp = kcc_ref.shape[1]
    heads_per_chunk = ROW_CHUNK // Q_TILE
    n_chunks = NSA_HEADS * Q_TILE // ROW_CHUNK
    lane = lax.broadcasted_iota(jnp.int32, (Q_TILE, LANES), 1)
    qrow = lax.broadcasted_iota(jnp.int32, (Q_TILE, LANES), 0) + q0
    left = lane < HEAD_DIM

    for j in range(NSA_HEADS // 2):
        blk = q_ref[0, :, j * LANES:(j + 1) * LANES]
        rolled = pltpu.roll(blk, HEAD_DIM, 1)
        if (2 * j) // HEADS_PER_GROUP == 0:
            even, odd = jnp.where(left, blk, 0.0), jnp.where(left, rolled, 0.0)
        else:
            even, odd = jnp.where(left, 0.0, rolled), jnp.where(left, 0.0, blk)
        qs_ref[(2 * j) * Q_TILE:(2 * j + 1) * Q_TILE, :] = even
        qs_ref[(2 * j + 1) * Q_TILE:(2 * j + 2) * Q_TILE, :] = odd
    qa_ref[:, 0:LANES] = qs_ref[...].astype(BF16)

    ci = lax.broadcasted_iota(jnp.int32, (Q_TILE, n_cmp), 1)
    rq = lax.broadcasted_iota(jnp.int32, (Q_TILE, n_cmp), 0) + q0
    mask_c = ci * CMP_STRIDE + (CMP_BLOCK - 1) <= rq
    kcc = kcc_ref[0]
    vcc = vcc_ref[0].astype(BF16)
    cur = qrow // SEL_BLOCK
    forced = (lane == 0) | (lane == cur) | (lane == cur - 1)
    lane_f = lane.astype(F32)
    for g in range(NSA_KV_GROUPS):
        pc_sum = jnp.zeros((Q_TILE, n_cmp), F32)
        for c in range(n_chunks // NSA_KV_GROUPS):
            r0 = (g * (n_chunks // NSA_KV_GROUPS) + c) * ROW_CHUNK
            s = _dot_nt(qs_ref[r0:r0 + ROW_CHUNK, :], kcc, precision=HIGHEST)
            p = _softmax_rows(s.reshape(heads_per_chunk, Q_TILE, n_cmp), mask_c)
            pc_sum = pc_sum + jnp.sum(p, axis=0)
            oc_ref[r0:r0 + ROW_CHUNK, :] = jnp.dot(p.reshape(ROW_CHUNK, n_cmp).astype(BF16), vcc,
                                                   preferred_element_type=F32)
        imp = jnp.dot(pc_sum, ov_ref[...], precision=HIGHEST, preferred_element_type=F32)
        v = jnp.where(lane > cur, -1.0, jnp.where(forced, jnp.inf, imp))
        sel = jnp.zeros((Q_TILE, LANES), F32)
        for _ in range(SEL_TOPK):
            mx = jnp.max(v, axis=-1, keepdims=True)
            first = jnp.min(jnp.where(v == mx, lane_f, float(LANES)), axis=-1, keepdims=True)
            pick = lane_f == first
            sel = jnp.where(pick, 1.0, sel)
            v = jnp.where(pick, -2.0, v)
        neg = jnp.where((sel > 0.0) & (lane < 2 * qt), 0.0, MASK_NEG).astype(BF16)
        for h in range(HEADS_PER_GROUP):
            r0 = (g * HEADS_PER_GROUP + h) * Q_TILE
            qa_ref[r0:r0 + Q_TILE, LANES:2 * LANES] = neg

    m_ref[...] = jnp.full(m_ref.shape, SCORE_FLOOR, F32)
    l_ref[...] = jnp.zeros(l_ref.shape, F32)
    os_ref[...] = jnp.zeros(os_ref.shape, F32)
    krow = lax.broadcasted_iota(jnp.int32, (SEL_KEY_TILE, LANES), 0)
    klane = lax.broadcasted_iota(jnp.int32, (SEL_KEY_TILE, LANES), 1)

    def online_update(r0, rows, s, vt):
        m_old = m_ref[pl.ds(r0, rows), :]
        m_new = jnp.maximum(m_old, jnp.max(s, axis=-1, keepdims=True))
        alpha = jnp.exp(m_old - m_new)
        p = jnp.exp(s - m_new)
        l_ref[pl.ds(r0, rows), :] = alpha * l_ref[pl.ds(r0, rows), :] + jnp.sum(p, axis=-1, keepdims=True)
        os_ref[pl.ds(r0, rows), :] = alpha * os_ref[pl.ds(r0, rows), :] + jnp.dot(
            p.astype(BF16), vt, preferred_element_type=F32)
        m_ref[pl.ds(r0, rows), :] = m_new

    def key_tile(kt, carry):
        k0 = pl.multiple_of(kt * SEL_KEY_TILE, SEL_KEY_TILE)
        onehot = jnp.where((k0 + krow) // SEL_BLOCK == klane, 1.0, 0.0).astype(BF16)
        ka = jnp.concatenate([ks_ref[0, pl.ds(k0, SEL_KEY_TILE), :], onehot], axis=1)
        vt = vs_ref[0, pl.ds(k0, SEL_KEY_TILE), :]

        def chunk(rc, carry2):
            r0 = pl.multiple_of(rc * ROW_CHUNK, ROW_CHUNK)
            online_update(r0, ROW_CHUNK, _dot_nt(qa_ref[pl.ds(r0, ROW_CHUNK), :], ka), vt)
            return carry2

        return lax.fori_loop(0, n_chunks, chunk, carry)

    lax.fori_loop(0, (q0 + SEL_KEY_TILE - 1) // SEL_KEY_TILE, key_tile, 0)

    kd = ks_ref[0, pl.ds(pl.multiple_of(q0, Q_TILE), Q_TILE), :]
    vd = vs_ref[0, pl.ds(pl.multiple_of(q0, Q_TILE), Q_TILE), :]
    causal = lane <= lax.broadcasted_iota(jnp.int32, (Q_TILE, LANES), 0)

    def diag_chunk(rc, carry):
        r0 = pl.multiple_of(rc * ROW_CHUNK, ROW_CHUNK)
        s = _dot_nt(qa_ref[pl.ds(r0, ROW_CHUNK), 0:LANES], kd).reshape(heads_per_chunk, Q_TILE, Q_TILE)
        s = jnp.where(causal[None], s, SCORE_FLOOR).reshape(ROW_CHUNK, Q_TILE)
        online_update(r0, ROW_CHUNK, s, vd)
        return carry

    lax.fori_loop(0, n_chunks, diag_chunk, 0)

    wlen = WINDOW + Q_TILE
    w0 = pl.multiple_of(jnp.maximum(q0 - WINDOW, 0), Q_TILE)
    kwn = kw_ref[0, pl.ds(w0, wlen), :]
    vwn = vw_ref[0, pl.ds(w0, wlen), :]
    kidx = w0 + lax.broadcasted_iota(jnp.int32, (Q_TILE, wlen), 1)
    tqw = q0 + lax.broadcasted_iota(jnp.int32, (Q_TILE, wlen), 0)
    mask_w = (kidx <= tqw) & (tqw - kidx < WINDOW)

    def win_chunk(rc, carry):
        r0 = pl.multiple_of(rc * ROW_CHUNK, ROW_CHUNK)
        s = _dot_nt(qa_ref[pl.ds(r0, ROW_CHUNK), 0:LANES], kwn)
        p = _softmax_rows(s.reshape(heads_per_chunk, Q_TILE, wlen), mask_w)
        ow_ref[pl.ds(r0, ROW_CHUNK), :] = jnp.dot(p.reshape(ROW_CHUNK, wlen).astype(BF16), vwn,
                                                  preferred_element_type=F32)
        return carry

    lax.fori_loop(0, n_chunks, win_chunk, 0)

    gates = gn_ref[0]
    for j in range(NSA_HEADS // 2):
        comb = []
        for h in (2 * j, 2 * j + 1):
            r0 = h * Q_TILE
            o_sel = os_ref[r0:r0 + Q_TILE, :] / l_ref[r0:r0 + Q_TILE, :]
            comb.append(gates[:, 3 * h:3 * h + 1] * oc_ref[r0:r0 + Q_TILE, :]
                        + gates[:, 3 * h + 1:3 * h + 2] * o_sel
                        + gates[:, 3 * h + 2:3 * h + 3] * ow_ref[r0:r0 + Q_TILE, :])
        if (2 * j) // HEADS_PER_GROUP == 0:
            blk = jnp.where(left, comb[0], pltpu.roll(comb[1], HEAD_DIM, 1))
        else:
            blk = jnp.where(left, pltpu.roll(comb[0], HEAD_DIM, 1), comb[1])
        o_ref[0, :, j * LANES:(j + 1) * LANES] = blk.astype(o_ref.dtype)


def _nsa(q, kcc, vcc, ks, vs, kw, vw, gn):
    B, S, _ = q.shape
    n_cmp = kcc.shape[1]
    n_sel = S // SEL_BLOCK
    assert n_sel <= LANES and S % SEL_KEY_TILE == 0 and S >= WINDOW + Q_TILE
    ci = np.arange(n_cmp)[:, None] * CMP_STRIDE
    sj = np.arange(LANES)[None, :] * SEL_BLOCK
    overlap = ((ci < sj + SEL_BLOCK) & (ci + CMP_BLOCK > sj)).astype(np.float32)
    rows = NSA_HEADS * Q_TILE
    full = lambda n, w: pl.BlockSpec((1, n, w), lambda b, i: (b, 0, 0))
    tile = lambda w: pl.BlockSpec((1, Q_TILE, w), lambda b, i: (b, i, 0))
    return pl.pallas_call(
        _nsa_kernel,
        grid=(B, S // Q_TILE),
        in_specs=[tile(NSA_WIDTH), full(n_cmp, 128), full(n_cmp, 128), full(S, 128), full(S, 128),
                  full(S, 128), full(S, 128), tile(128), _resident(overlap.shape)],
        out_specs=tile(NSA_WIDTH),
        out_shape=jax.ShapeDtypeStruct((B, S, NSA_WIDTH), BF16),
        scratch_shapes=[pltpu.VMEM((rows, LANES), F32), pltpu.VMEM((rows, 2 * LANES), BF16),
                        pltpu.VMEM((rows, LANES), F32), pltpu.VMEM((rows, LANES), F32),
                        pltpu.VMEM((rows, LANES), F32), pltpu.VMEM((rows, 1), F32),
                        pltpu.VMEM((rows, 1), F32)],
        compiler_params=pltpu.CompilerParams(dimension_semantics=("parallel", "parallel"),
                                             vmem_limit_bytes=VMEM_LIMIT),
        name="nsa",
    )(q, kcc, vcc, ks, vs, kw, vw, gn, jnp.asarray(overlap))


POOL_HALO = 16


def _merge_kernel(x_ref, xh_ref, yn_ref, kvm_ref, g_ref, wpool_ref, wqx_ref, wgbr_ref, poolw_ref, pscale_ref,
                  wbp_ref, wbn_ref, wbx_ref, wout_ref, gpost_ref, o_ref, *, S):
    i = pl.program_id(0)
    tm = x_ref.shape[0]
    t0 = (i * tm) % S
    x = x_ref[...]
    g = g_ref[...]
    h = _rms(x, g)
    hb = h.astype(BF16)

    hh = _rms(xh_ref[...], g).astype(BF16)
    u_halo = jnp.dot(hh, wpool_ref[...], preferred_element_type=F32)
    u_halo = jnp.where(t0 > 0, u_halo, 0.0)
    u = jnp.dot(hb, wpool_ref[...], preferred_element_type=F32)
    ue = jnp.concatenate([u_halo, u], axis=0)
    trow = t0 + lax.broadcasted_iota(jnp.int32, (tm, 1), 0) + 1
    ypool = []
    for gi, w in enumerate(POOL_WINDOWS):
        acc = ue[:, gi * POOL_GROUP:(gi + 1) * POOL_GROUP]
        step = 1
        while step < w:
            acc = acc + pltpu.roll(acc, step, 0)
            step *= 2
        cnt = jnp.minimum(trow, w).astype(F32)
        p = acc[POOL_HALO:, :] / cnt - u[:, gi * POOL_GROUP:(gi + 1) * POOL_GROUP]
        ypool.append(jnp.dot(p.astype(BF16), poolw_ref[gi], preferred_element_type=F32))
    ypool = jnp.concatenate(ypool, axis=1) * pscale_ref[...]

    qx = jnp.dot(hb, wqx_ref[...], preferred_element_type=F32)
    ymem = []
    for hd in range(XA_HEADS):
        km = kvm_ref[0, :, hd * XA_HEAD_DIM:(hd + 1) * XA_HEAD_DIM]
        vm = kvm_ref[0, :, XA_WIDTH + hd * XA_HEAD_DIM:XA_WIDTH + (hd + 1) * XA_HEAD_DIM]
        s = _dot_nt(qx[:, hd * XA_HEAD_DIM:(hd + 1) * XA_HEAD_DIM].astype(BF16), km) * (XA_HEAD_DIM ** -0.5)
        e = jnp.exp(s - jnp.max(s, axis=-1, keepdims=True))
        p = e / jnp.sum(e, axis=-1, keepdims=True)
        ymem.append(jnp.dot(p.astype(BF16), vm, preferred_element_type=F32))
    ymem = jnp.concatenate(ymem, axis=1)

    gbr = jax.nn.sigmoid(jnp.dot(hb, wgbr_ref[...], preferred_element_type=F32))
    y = (gbr[:, 0:D_MODEL] * jnp.dot(ypool.astype(BF16), wbp_ref[...], preferred_element_type=F32)
         + gbr[:, D_MODEL:2 * D_MODEL] * jnp.dot(yn_ref[...], wbn_ref[...], preferred_element_type=F32)
         + gbr[:, 2 * D_MODEL:3 * D_MODEL] * jnp.dot(ymem.astype(BF16), wbx_ref[...], preferred_element_type=F32))
    o = jnp.dot(y.astype(BF16), wout_ref[...], preferred_element_type=F32)
    o_ref[...] = x + _rms(o, gpost_ref[...])


def _merge(x2, ynsa, kvm, g, wpool, wqx, wgbr, poolw, pscale, wbp, wbn, wbx, wout, gpost, S, tm=256):
    T = x2.shape[0]
    M = kvm.shape[1]
    row = lambda w: pl.BlockSpec((tm, w), lambda i: (i, 0))
    halo = pl.BlockSpec((POOL_HALO, D_MODEL), lambda i: (jnp.maximum(i * (tm // POOL_HALO) - 1, 0), 0))
    return pl.pallas_call(
        functools.partial(_merge_kernel, S=S),
        grid=(T // tm,),
        in_specs=[row(D_MODEL), halo, row(NSA_WIDTH),
                  pl.BlockSpec((1, M, 2 * XA_WIDTH), lambda i: ((i * tm) // S, 0, 0)),
                  _resident((1, D_MODEL)), _resident(wpool.shape), _resident(wqx.shape), _resident(wgbr.shape),
                  _resident(poolw.shape), _resident(pscale.shape), _resident(wbp.shape), _resident(wbn.shape),
                  _resident(wbx.shape), _resident(wout.shape), _resident((1, D_MODEL))],
        out_specs=row(D_MODEL),
        out_shape=jax.ShapeDtypeStruct((T, D_MODEL), F32),
        compiler_params=pltpu.CompilerParams(dimension_semantics=("parallel",), vmem_limit_bytes=VMEM_LIMIT),
        name="merge",
    )(x2, x2, ynsa, kvm, g, wpool, wqx, wgbr, poolw, pscale, wbp, wbn, wbx, wout, gpost)


FFN_HALO = 8
FFN_CHUNK = 1408


def _ffn_kernel(x_ref, xh_ref, g_ref, wup_ref, cw_ref, cb_ref, wdn_ref, gpost_ref, o_ref, *, S):
    i = pl.program_id(0)
    tm = x_ref.shape[0]
    t0 = (i * tm) % S
    x = x_ref[...]
    g = g_ref[...]
    hh = jnp.where(t0 > 0, _rms(xh_ref[...], g), 0.0)
    he = jnp.concatenate([hh, _rms(x, g)], axis=0).astype(BF16)

    def conv(col0):
        u = jnp.dot(he, wup_ref[:, col0:col0 + FFN_CHUNK], preferred_element_type=F32)
        c = cb_ref[:, col0:col0 + FFN_CHUNK] + cw_ref[CONV_WIDTH - 1:CONV_WIDTH, col0:col0 + FFN_CHUNK] * u
        for k in range(1, CONV_WIDTH):
            tap = cw_ref[CONV_WIDTH - 1 - k:CONV_WIDTH - k, col0:col0 + FFN_CHUNK]
            c = c + tap * pltpu.roll(u, k, 0)
        return c[FFN_HALO:, :]

    f = jnp.zeros((tm, D_MODEL), F32)
    for j in range(D_FF // FFN_CHUNK):
        act = _gelu_tanh(conv(j * FFN_CHUNK)) * conv(D_FF + j * FFN_CHUNK)
        f = f + jnp.dot(act.astype(BF16), wdn_ref[j * FFN_CHUNK:(j + 1) * FFN_CHUNK, :],
                        preferred_element_type=F32)
    o_ref[...] = x + _rms(f, gpost_ref[...])


def _ffn(x2, g, wup, cw, cb, wdn, gpost, S, tm=512):
    T = x2.shape[0]
    row = pl.BlockSpec((tm, D_MODEL), lambda i: (i, 0))
    halo = pl.BlockSpec((FFN_HALO, D_MODEL), lambda i: (jnp.maximum(i * (tm // FFN_HALO) - 1, 0), 0))
    return pl.pallas_call(
        functools.partial(_ffn_kernel, S=S),
        grid=(T // tm,),
        in_specs=[row, halo, _resident((1, D_MODEL)), _resident(wup.shape), _resident(cw.shape),
                  _resident(cb.shape), _resident(wdn.shape), _resident((1, D_MODEL))],
        out_specs=row,
        out_shape=jax.ShapeDtypeStruct((T, D_MODEL), F32),
        compiler_params=pltpu.CompilerParams(dimension_semantics=("parallel",), vmem_limit_bytes=VMEM_LIMIT),
        name="ffn",
    )(x2, x2, g, wup, cw, cb, wdn, gpost)


def _layer(x2, mem, pos2, B, S, pre_mix_g, w_in, pool_w, pool_scale, cmp_pe, cmp_w1, cmp_w2, mem_norm_g,
           w_mem_kv, w_br_pool, w_br_nsa, w_br_xa, w_out, post_mix_g, pre_ffn_g, w_up, conv_w, conv_b,
           w_down, post_ffn_g):
    o = np.cumsum((0,) + IN_SIZES)
    w_pool, w_q, w_kv, w_gn, w_qx, w_gbr = (w_in[:, o[k]:o[k + 1]] for k in range(6))
    w_gn = jnp.pad(w_gn, ((0, 0), (0, LANES - w_gn.shape[1])))
    vec = lambda a: a.reshape(1, -1)
    q, kc, vc, ks, vs, kw, vw, gn = _in_proj(x2, pos2, vec(pre_mix_g), w_q, w_kv[:, :128],
                                             w_kv[:, 128:].astype(BF16), w_gn.astype(BF16))
    b3 = lambda a: a.reshape(B, S, a.shape[-1])
    kcc = _compress(b3(kc), cmp_pe[0], cmp_w1[0], cmp_w2[0])
    vcc = _compress(b3(vc), cmp_pe[1], cmp_w1[1], cmp_w2[1])
    ynsa = _nsa(b3(q), kcc, vcc, b3(ks), b3(vs), b3(kw), b3(vw), b3(gn))
    kvm = _mem_kv(mem, vec(mem_norm_g), w_mem_kv.astype(BF16))
    x2 = _merge(x2, ynsa.reshape(B * S, NSA_WIDTH), kvm, vec(pre_mix_g), w_pool.astype(BF16), w_qx.astype(BF16),
                w_gbr.astype(BF16), pool_w.astype(BF16), vec(pool_scale), w_br_pool.astype(BF16),
                w_br_nsa.astype(BF16), w_br_xa.astype(BF16), w_out.astype(BF16), vec(post_mix_g), S)
    return _ffn(x2, vec(pre_ffn_g), w_up.astype(BF16), conv_w, vec(conv_b), w_down.astype(BF16),
                vec(post_ffn_g), S)


def kernel(x, mem, positions, pre_mix_g, w_in, pool_w, pool_scale, cmp_pe, cmp_w1, cmp_w2, mem_norm_g, w_mem_kv,
           w_br_pool, w_br_nsa, w_br_xa, w_out, post_mix_g, pre_ffn_g, w_up, conv_w, conv_b, w_down, post_ffn_g):
    B, S, D = x.shape
    x2 = x.reshape(B * S, D)
    pos2 = positions.reshape(B * S, 1)
    for l in range(pre_mix_g.shape[0]):
        x2 = _layer(x2, mem, pos2, B, S, pre_mix_g[l], w_in[l], pool_w[l], pool_scale[l], cmp_pe[l], cmp_w1[l],
                    cmp_w2[l], mem_norm_g[l], w_mem_kv[l], w_br_pool[l], w_br_nsa[l], w_br_xa[l], w_out[l],
                    post_mix_g[l], pre_ffn_g[l], w_up[l], conv_w[l], conv_b[l], w_down[l], post_ffn_g[l])
    return x2.reshape(B, S, D)
```

```python
import functools

import numpy as np
import jax
import jax.numpy as jnp
from jax import lax
from jax.experimental import pallas as pl
from jax.experimental.pallas import tpu as pltpu

F32 = jnp.float32
BF16 = jnp.bfloat16
HIGHEST = lax.Precision.HIGHEST

D_MODEL = 1024
EPS = 1e-6
POOL_WINDOWS = (2, 4, 8, 16)
POOL_GROUP = 128
POOL_WIDTH = POOL_GROUP * len(POOL_WINDOWS)
NSA_HEADS = 16
NSA_KV_GROUPS = 2
HEADS_PER_GROUP = NSA_HEADS // NSA_KV_GROUPS
HEAD_DIM = 64
NSA_WIDTH = NSA_HEADS * HEAD_DIM
NSA_KV_WIDTH = NSA_KV_GROUPS * HEAD_DIM
CMP_BLOCK = 32
CMP_STRIDE = 16
CMP_HIDDEN = 256
SEL_BLOCK = 64
SEL_TOPK = 16
WINDOW = 512
ROPE_THETA = 500000.0
ROT_DIM = HEAD_DIM // 4
XA_HEADS = 4
XA_HEAD_DIM = 128
XA_WIDTH = XA_HEADS * XA_HEAD_DIM
N_BRANCHES = 3
D_FF = 2816
CONV_WIDTH = 3
IN_SIZES = (POOL_WIDTH, NSA_WIDTH, 6 * NSA_KV_WIDTH, 3 * NSA_HEADS, XA_WIDTH, N_BRANCHES * D_MODEL)

LANES = 128
Q_TILE = 128
SEL_KEY_TILE = 512
COL_CHUNK = 512
MASK_NEG = -1e9
SCORE_FLOOR = -1e30
VMEM_LIMIT = 56 * 1024 * 1024


def _rms(x, g):
    return x * lax.rsqrt(jnp.mean(x * x, axis=-1, keepdims=True) + EPS) * g


def _gelu_tanh(x):
    return 0.5 * x * (1.0 + jnp.tanh(0.7978845608028654 * (x + 0.044715 * (x * x * x))))


def _dot_nt(a, b, precision=None):
    return lax.dot_general(a, b, (((1,), (1,)), ((), ())), precision=precision,
                           preferred_element_type=F32)


def _resident(shape):
    return pl.BlockSpec(shape, lambda *_: (0,) * len(shape), pipeline_mode=pl.Buffered(1))


def _inproj_kernel(x_ref, pos_ref, g_ref, wq_ref, wkc_ref, wkv_ref, wgn_ref, frq_ref, sg1_ref, sg2_ref,
                   q_ref, kc_ref, vc_ref, ks_ref, vs_ref, kw_ref, vw_ref, gn_ref):
    h = _rms(x_ref[...], g_ref[...])
    hb = h.astype(BF16)
    ang = pos_ref[...].astype(F32) * frq_ref[...]
    c = jnp.cos(ang)
    s = jnp.sin(ang)
    s1 = s * sg1_ref[...]
    s2 = s * sg2_ref[...]

    def rope(v):
        return v * c + pltpu.roll(v, LANES - ROT_DIM // 2, 1) * s1 + pltpu.roll(v, ROT_DIM // 2, 1) * s2

    q = jnp.dot(h, wq_ref[...], precision=HIGHEST, preferred_element_type=F32)
    for j in range(NSA_WIDTH // LANES):
        q_ref[:, j * LANES:(j + 1) * LANES] = rope(q[:, j * LANES:(j + 1) * LANES]) * (HEAD_DIM ** -0.5)
    kc_ref[...] = rope(jnp.dot(h, wkc_ref[...], precision=HIGHEST, preferred_element_type=F32))
    kv = jnp.dot(hb, wkv_ref[...], preferred_element_type=F32)
    vc_ref[...] = kv[:, 0:128]
    ks_ref[...] = rope(kv[:, 128:256]).astype(BF16)
    vs_ref[...] = kv[:, 256:384].T.astype(BF16)
    kw_ref[...] = rope(kv[:, 384:512]).astype(BF16)
    vw_ref[...] = kv[:, 512:640].T.astype(BF16)
    gn_ref[...] = jax.nn.sigmoid(jnp.dot(hb, wgn_ref[...], preferred_element_type=F32))


def _in_proj(x2, pos2, g, wq, wkc, wkv, wgn, tm=512):
    T = x2.shape[0]
    half = ROT_DIM // 2
    d = np.arange(LANES) % HEAD_DIM
    inv_freq = ROPE_THETA ** (-jnp.arange(half, dtype=F32) * (2.0 / ROT_DIM))
    frq = jnp.where(d < ROT_DIM, inv_freq[d % half], 0.0)[None, :]
    sg1 = np.where(d < half, -1.0, 0.0).astype(np.float32)[None, :]
    sg2 = np.where((d >= half) & (d < ROT_DIM), 1.0, 0.0).astype(np.float32)[None, :]
    row = lambda w: pl.BlockSpec((tm, w), lambda i: (i, 0))
    col = pl.BlockSpec((128, tm), lambda i: (0, i))
    out_shape = [jax.ShapeDtypeStruct((T, NSA_WIDTH), F32),
                 jax.ShapeDtypeStruct((T, 128), F32), jax.ShapeDtypeStruct((T, 128), F32),
                 jax.ShapeDtypeStruct((T, 128), BF16), jax.ShapeDtypeStruct((128, T), BF16),
                 jax.ShapeDtypeStruct((T, 128), BF16), jax.ShapeDtypeStruct((128, T), BF16),
                 jax.ShapeDtypeStruct((T, 128), F32)]
    return pl.pallas_call(
        _inproj_kernel,
        grid=(T // tm,),
        in_specs=[row(D_MODEL), row(1), _resident((1, D_MODEL)), _resident(wq.shape), _resident(wkc.shape),
                  _resident(wkv.shape), _resident(wgn.shape), _resident((1, LANES)), _resident((1, LANES)),
                  _resident((1, LANES))],
        out_specs=[row(NSA_WIDTH), row(128), row(128), row(128), col, row(128), col, row(128)],
        out_shape=out_shape,
        compiler_params=pltpu.CompilerParams(dimension_semantics=("parallel",), vmem_limit_bytes=VMEM_LIMIT),
        name="in_proj",
    )(x2, pos2, g, wq, wkc, wkv, wgn, frq, jnp.asarray(sg1), jnp.asarray(sg2))


def _compress_kernel(c_ref, pet_ref, peb_ref, wt_ref, wb_ref, w2_ref, o_ref):
    c = c_ref[0]
    a = jnp.dot(c + pet_ref[...], wt_ref[...], precision=HIGHEST, preferred_element_type=F32)
    b = jnp.dot(c + peb_ref[...], wb_ref[...], precision=HIGHEST, preferred_element_type=F32)
    n = c.shape[0]
    hid = a + pltpu.roll(b, n - 1, 0)
    o_ref[0] = jnp.dot(_gelu_tanh(hid), w2_ref[...], precision=HIGHEST, preferred_element_type=F32)


def _compress(kv, pe, w1, w2):
    B, S, _ = kv.shape
    n = S // CMP_STRIDE
    G = NSA_KV_GROUPS
    half = CMP_BLOCK // 2
    eye = jnp.eye(G, dtype=F32)

    def blockdiag_w1(w):
        w = w.reshape(half, 1, HEAD_DIM, 1, CMP_HIDDEN) * eye.reshape(1, G, 1, G, 1)
        return w.reshape(half * G * HEAD_DIM, G * CMP_HIDDEN)

    wt = blockdiag_w1(w1[:half * HEAD_DIM])
    wb = blockdiag_w1(w1[half * HEAD_DIM:])
    w2b = (w2.reshape(1, CMP_HIDDEN, 1, HEAD_DIM) * eye.reshape(G, 1, G, 1)).reshape(G * CMP_HIDDEN, G * HEAD_DIM)
    pet = jnp.broadcast_to(pe[:half, None, :], (half, G, HEAD_DIM)).reshape(1, half * G * HEAD_DIM)
    peb = jnp.broadcast_to(pe[half:, None, :], (half, G, HEAD_DIM)).reshape(1, half * G * HEAD_DIM)
    c = kv.reshape(B, n, CMP_STRIDE * 128)
    return pl.pallas_call(
        _compress_kernel,
        grid=(B,),
        in_specs=[pl.BlockSpec((1, n, CMP_STRIDE * 128), lambda b: (b, 0, 0)),
                  _resident(pet.shape), _resident(peb.shape), _resident(wt.shape), _resident(wb.shape),
                  _resident(w2b.shape)],
        out_specs=pl.BlockSpec((1, n, 128), lambda b: (b, 0, 0)),
        out_shape=jax.ShapeDtypeStruct((B, n, 128), F32),
        compiler_params=pltpu.CompilerParams(dimension_semantics=("parallel",), vmem_limit_bytes=VMEM_LIMIT),
        name="compress",
    )(c, pet, peb, wt, wb, w2b)


def _memkv_kernel(m_ref, g_ref, w_ref, o_ref):
    mn = _rms(m_ref[0], g_ref[...])
    o_ref[0] = jnp.dot(mn.astype(BF16), w_ref[...], preferred_element_type=F32).astype(BF16)


def _mem_kv(mem, g, w):
    B, M, _ = mem.shape
    return pl.pallas_call(
        _memkv_kernel,
        grid=(B,),
        in_specs=[pl.BlockSpec((1, M, D_MODEL), lambda b: (b, 0, 0)), _resident((1, D_MODEL)),
                  _resident(w.shape)],
        out_specs=pl.BlockSpec((1, M, 2 * XA_WIDTH), lambda b: (b, 0, 0)),
        out_shape=jax.ShapeDtypeStruct((B, M, 2 * XA_WIDTH), BF16),
        compiler_params=pltpu.CompilerParams(dimension_semantics=("parallel",), vmem_limit_bytes=VMEM_LIMIT),
        name="mem_kv",
    )(mem, g, w)


def _nsa_kernel(q_ref, kcc_ref, vcc_ref, ks_ref, vst_ref, kw_ref, vwt_ref, gn_ref, ovt_ref, o_ref,
                qst_ref, qat_ref, oct_ref, ost_ref, owt_ref, m_ref, l_ref):
    qt = pl.program_id(1)
    q0 = qt * Q_TILE
    n_cmp = kcc_ref.shape[1]
    heads_per_chunk = COL_CHUNK // Q_TILE
    n_chunks = NSA_HEADS * Q_TILE // COL_CHUNK
    chunk_cols = [slice(c * COL_CHUNK, (c + 1) * COL_CHUNK) for c in range(n_chunks)]
    head_cols = [slice(h * Q_TILE, (h + 1) * Q_TILE) for h in range(NSA_HEADS)]
    per_head = lambda a: jnp.concatenate([a] * heads_per_chunk, axis=1)

    zeros = jnp.zeros((HEAD_DIM, Q_TILE), F32)
    for j in range(NSA_HEADS // 2):
        t = q_ref[0, :, j * LANES:(j + 1) * LANES].T
        for k in range(2):
            h = 2 * j + k
            part = t[k * HEAD_DIM:(k + 1) * HEAD_DIM, :]
            pair = [part, zeros] if h // HEADS_PER_GROUP == 0 else [zeros, part]
            qst_ref[:, head_cols[h]] = jnp.concatenate(pair, axis=0)
    qat_ref[0:LANES, :] = qst_ref[...].astype(BF16)

    crow = lax.broadcasted_iota(jnp.int32, (n_cmp, Q_TILE), 0)
    cq = lax.broadcasted_iota(jnp.int32, (n_cmp, Q_TILE), 1) + q0
    valid_c = crow * CMP_STRIDE + (CMP_BLOCK - 1) <= cq
    bias_c = per_head(jnp.where(valid_c, 0.0, SCORE_FLOOR))
    keep_c = per_head(jnp.where(valid_c, 1.0, 0.0))
    kcc = kcc_ref[0]
    vcct = vcc_ref[0].T.astype(BF16)
    brow = lax.broadcasted_iota(jnp.int32, (LANES, Q_TILE), 0)
    cur = (lax.broadcasted_iota(jnp.int32, (LANES, Q_TILE), 1) + q0) // SEL_BLOCK
    forced = (brow == 0) | (brow == cur) | (brow == cur - 1)
    brow_f = brow.astype(F32)
    for g in range(NSA_KV_GROUPS):
        pc_sum = jnp.zeros((n_cmp, Q_TILE), F32)
        for c in range(n_chunks // NSA_KV_GROUPS):
            cols = chunk_cols[g * (n_chunks // NSA_KV_GROUPS) + c]
            s = jnp.dot(kcc, qst_ref[:, cols], precision=HIGHEST, preferred_element_type=F32) + bias_c
            p = jnp.exp(s - jnp.max(s, axis=0, keepdims=True)) * keep_c
            p = p / jnp.maximum(jnp.sum(p, axis=0, keepdims=True), jnp.finfo(F32).tiny)
            for k in range(heads_per_chunk):
                pc_sum = pc_sum + p[:, k * Q_TILE:(k + 1) * Q_TILE]
            oct_ref[:, cols] = jnp.dot(vcct, p.astype(BF16), preferred_element_type=F32)
        imp = jnp.dot(ovt_ref[...], pc_sum, precision=HIGHEST, preferred_element_type=F32)
        v = jnp.where(brow > cur, -1.0, jnp.where(forced, jnp.inf, imp))
        sel = jnp.zeros((LANES, Q_TILE), F32)
        for _ in range(SEL_TOPK):
            mx = jnp.max(v, axis=0, keepdims=True)
            first = jnp.min(jnp.where(v == mx, brow_f, float(LANES)), axis=0, keepdims=True)
            pick = brow_f == first
            sel = jnp.where(pick, 1.0, sel)
            v = jnp.where(pick, -2.0, v)
        neg = jnp.where((sel > 0.0) & (brow < 2 * qt), 0.0, MASK_NEG).astype(BF16)
        for h in range(HEADS_PER_GROUP):
            qat_ref[LANES:2 * LANES, head_cols[g * HEADS_PER_GROUP + h]] = neg

    m_ref[...] = jnp.full(m_ref.shape, SCORE_FLOOR, F32)
    l_ref[...] = jnp.zeros(l_ref.shape, F32)
    ost_ref[...] = jnp.zeros(ost_ref.shape, F32)
    krow = lax.broadcasted_iota(jnp.int32, (SEL_KEY_TILE, LANES), 0)
    klane = lax.broadcasted_iota(jnp.int32, (SEL_KEY_TILE, LANES), 1)

    def online_update(cols, s, vt):
        m_old = m_ref[:, cols]
        m_new = jnp.maximum(m_old, jnp.max(s, axis=0, keepdims=True))
        alpha = jnp.exp(m_old - m_new)
        p = jnp.exp(s - m_new)
        l_ref[:, cols] = alpha * l_ref[:, cols] + jnp.sum(p, axis=0, keepdims=True)
        ost_ref[:, cols] = alpha * ost_ref[:, cols] + jnp.dot(vt, p.astype(BF16), preferred_element_type=F32)
        m_ref[:, cols] = m_new

    def key_tile(kt, carry):
        k0 = pl.multiple_of(kt * SEL_KEY_TILE, SEL_KEY_TILE)
        onehot = jnp.where((k0 + krow) // SEL_BLOCK == klane, 1.0, 0.0).astype(BF16)
        ka = jnp.concatenate([ks_ref[0, pl.ds(k0, SEL_KEY_TILE), :], onehot], axis=1)
        vt = vst_ref[:, pl.ds(k0, SEL_KEY_TILE)]
        for cols in chunk_cols:
            online_update(cols, jnp.dot(ka, qat_ref[:, cols], preferred_element_type=F32), vt)
        return carry

    lax.fori_loop(0, (q0 + SEL_KEY_TILE - 1) // SEL_KEY_TILE, key_tile, 0)

    qd = pl.multiple_of(q0, Q_TILE)
    kd = ks_ref[0, pl.ds(qd, Q_TILE), :]
    vdt = vst_ref[:, pl.ds(qd, Q_TILE)]
    drow = lax.broadcasted_iota(jnp.int32, (Q_TILE, Q_TILE), 0)
    dcol = lax.broadcasted_iota(jnp.int32, (Q_TILE, Q_TILE), 1)
    bias_d = per_head(jnp.where(drow <= dcol, 0.0, SCORE_FLOOR))
    for cols in chunk_cols:
        online_update(cols, jnp.dot(kd, qat_ref[0:LANES, cols], preferred_element_type=F32) + bias_d, vdt)

    wlen = WINDOW + Q_TILE
    w0 = pl.multiple_of(jnp.maximum(q0 - WINDOW, 0), Q_TILE)
    kwn = kw_ref[0, pl.ds(w0, wlen), :]
    vwnt = vwt_ref[:, pl.ds(w0, wlen)]
    kidx = w0 + lax.broadcasted_iota(jnp.int32, (wlen, Q_TILE), 0)
    tqw = q0 + lax.broadcasted_iota(jnp.int32, (wlen, Q_TILE), 1)
    bias_w = per_head(jnp.where((kidx <= tqw) & (tqw - kidx < WINDOW), 0.0, SCORE_FLOOR))
    for cols in chunk_cols:
        s = jnp.dot(kwn, qat_ref[0:LANES, cols], preferred_element_type=F32) + bias_w
        p = jnp.exp(s - jnp.max(s, axis=0, keepdims=True))
        o = jnp.dot(vwnt, p.astype(BF16), preferred_element_type=F32)
        owt_ref[:, cols] = o / jnp.sum(p, axis=0, keepdims=True)

    gt = gn_ref[0].T
    for j in range(NSA_HEADS // 2):
        parts = []
        for h in (2 * j, 2 * j + 1):
            g = h // HEADS_PER_GROUP
            rows = slice(g * HEAD_DIM, (g + 1) * HEAD_DIM)
            cols = head_cols[h]
            o_sel = ost_ref[rows, cols] / l_ref[:, cols]
            parts.append(gt[3 * h:3 * h + 1, :] * oct_ref[rows, cols] + gt[3 * h + 1:3 * h + 2, :] * o_sel
                         + gt[3 * h + 2:3 * h + 3, :] * owt_ref[rows, cols])
        o_ref[0, :, j * LANES:(j + 1) * LANES] = jnp.concatenate(parts, axis=0).T.astype(o_ref.dtype)


def _nsa(q, kcc, vcc, ks, vst, kw, vwt, gn):
    B, S, _ = q.shape
    n_cmp = kcc.shape[1]
    n_sel = S // SEL_BLOCK
    assert n_sel <= LANES and S % SEL_KEY_TILE == 0 and S >= WINDOW + Q_TILE
    ci = np.arange(n_cmp)[None, :] * CMP_STRIDE
    sj = np.arange(LANES)[:, None] * SEL_BLOCK
    overlap_t = ((ci < sj + SEL_BLOCK) & (ci + CMP_BLOCK > sj)).astype(np.float32)
    cols = NSA_HEADS * Q_TILE
    full = lambda n, w: pl.BlockSpec((1, n, w), lambda b, i: (b, 0, 0))
    full_t = pl.BlockSpec((128, S), lambda b, i: (0, b))
    tile = lambda w: pl.BlockSpec((1, Q_TILE, w), lambda b, i: (b, i, 0))
    return pl.pallas_call(
        _nsa_kernel,
        grid=(B, S // Q_TILE),
        in_specs=[tile(NSA_WIDTH), full(n_cmp, 128), full(n_cmp, 128), full(S, 128), full_t,
                  full(S, 128), full_t, tile(128), _resident(overlap_t.shape)],
        out_specs=tile(NSA_WIDTH),
        out_shape=jax.ShapeDtypeStruct((B, S, NSA_WIDTH), BF16),
        scratch_shapes=[pltpu.VMEM((LANES, cols), F32), pltpu.VMEM((2 * LANES, cols), BF16),
                        pltpu.VMEM((LANES, cols), F32), pltpu.VMEM((LANES, cols), F32),
                        pltpu.VMEM((LANES, cols), F32), pltpu.VMEM((1, cols), F32),
                        pltpu.VMEM((1, cols), F32)],
        compiler_params=pltpu.CompilerParams(dimension_semantics=("parallel", "parallel"),
                                             vmem_limit_bytes=VMEM_LIMIT),
        name="nsa",
    )(q, kcc, vcc, ks, vst, kw, vwt, gn, jnp.asarray(overlap_t))


POOL_HALO = 16


def _merge_kernel(x_ref, xh_ref, yn_ref, kvm_ref, g_ref, wpool_ref, wqx_ref, wgbr_ref, poolw_ref, pscale_ref,
                  wbp_ref, wbn_ref, wbx_ref, wout_ref, gpost_ref, o_ref, *, S):
    i = pl.program_id(0)
    tm = x_ref.shape[0]
    t0 = (i * tm) % S
    x = x_ref[...]
    g = g_ref[...]
    h = _rms(x, g)
    hb = h.astype(BF16)

    hh = _rms(xh_ref[...], g).astype(BF16)
    u_halo = jnp.dot(hh, wpool_ref[...], preferred_element_type=F32)
    u_halo = jnp.where(t0 > 0, u_halo, 0.0)
    u = jnp.dot(hb, wpool_ref[...], preferred_element_type=F32)
    ue = jnp.concatenate([u_halo, u], axis=0)
    trow = t0 + lax.broadcasted_iota(jnp.int32, (tm, 1), 0) + 1
    ypool = []
    for gi, w in enumerate(POOL_WINDOWS):
        acc = ue[:, gi * POOL_GROUP:(gi + 1) * POOL_GROUP]
        step = 1
        while step < w:
            acc = acc + pltpu.roll(acc, step, 0)
            step *= 2
        cnt = jnp.minimum(trow, w).astype(F32)
        p = acc[POOL_HALO:, :] / cnt - u[:, gi * POOL_GROUP:(gi + 1) * POOL_GROUP]
        ypool.append(jnp.dot(p.astype(BF16), poolw_ref[gi], preferred_element_type=F32))
    ypool = jnp.concatenate(ypool, axis=1) * pscale_ref[...]

    qx = jnp.dot(hb, wqx_ref[...], preferred_element_type=F32)
    ymem = []
    for hd in range(XA_HEADS):
        km = kvm_ref[0, :, hd * XA_HEAD_DIM:(hd + 1) * XA_HEAD_DIM]
        vm = kvm_ref[0, :, XA_WIDTH + hd * XA_HEAD_DIM:XA_WIDTH + (hd + 1) * XA_HEAD_DIM]
        s = _dot_nt(qx[:, hd * XA_HEAD_DIM:(hd + 1) * XA_HEAD_DIM].astype(BF16), km) * (XA_HEAD_DIM ** -0.5)
        e = jnp.exp(s - jnp.max(s, axis=-1, keepdims=True))
        p = e / jnp.sum(e, axis=-1, keepdims=True)
        ymem.append(jnp.dot(p.astype(BF16), vm, preferred_element_type=F32))
    ymem = jnp.concatenate(ymem, axis=1)

    gbr = jax.nn.sigmoid(jnp.dot(hb, wgbr_ref[...], preferred_element_type=F32))
    y = (gbr[:, 0:D_MODEL] * jnp.dot(ypool.astype(BF16), wbp_ref[...], preferred_element_type=F32)
         + gbr[:, D_MODEL:2 * D_MODEL] * jnp.dot(yn_ref[...], wbn_ref[...], preferred_element_type=F32)
         + gbr[:, 2 * D_MODEL:3 * D_MODEL] * jnp.dot(ymem.astype(BF16), wbx_ref[...], preferred_element_type=F32))
    o = jnp.dot(y.astype(BF16), wout_ref[...], preferred_element_type=F32)
    o_ref[...] = x + _rms(o, gpost_ref[...])


def _merge(x2, ynsa, kvm, g, wpool, wqx, wgbr, poolw, pscale, wbp, wbn, wbx, wout, gpost, S, tm=256):
    T = x2.shape[0]
    M = kvm.shape[1]
    row = lambda w: pl.BlockSpec((tm, w), lambda i: (i, 0))
    halo = pl.BlockSpec((POOL_HALO, D_MODEL), lambda i: (jnp.maximum(i * (tm // POOL_HALO) - 1, 0), 0))
    return pl.pallas_call(
        functools.partial(_merge_kernel, S=S),
        grid=(T // tm,),
        in_specs=[row(D_MODEL), halo, row(NSA_WIDTH),
                  pl.BlockSpec((1, M, 2 * XA_WIDTH), lambda i: ((i * tm) // S, 0, 0)),
                  _resident((1, D_MODEL)), _resident(wpool.shape), _resident(wqx.shape), _resident(wgbr.shape),
                  _resident(poolw.shape), _resident(pscale.shape), _resident(wbp.shape), _resident(wbn.shape),
                  _resident(wbx.shape), _resident(wout.shape), _resident((1, D_MODEL))],
        out_specs=row(D_MODEL),
        out_shape=jax.ShapeDtypeStruct((T, D_MODEL), F32),
        compiler_params=pltpu.CompilerParams(dimension_semantics=("parallel",), vmem_limit_bytes=VMEM_LIMIT),
        name="merge",
    )(x2, x2, ynsa, kvm, g, wpool, wqx, wgbr, poolw, pscale, wbp, wbn, wbx, wout, gpost)


FFN_HALO = 8
FFN_CHUNK = 1408


def _ffn_kernel(x_ref, xh_ref, g_ref, wup_ref, cw_ref, cb_ref, wdn_ref, gpost_ref, o_ref, *, S):
    i = pl.program_id(0)
    tm = x_ref.shape[0]
    t0 = (i * tm) % S
    x = x_ref[...]
    g = g_ref[...]
    hh = jnp.where(t0 > 0, _rms(xh_ref[...], g), 0.0)
    he = jnp.concatenate([hh, _rms(x, g)], axis=0).astype(BF16)

    def conv(col0):
        u = jnp.dot(he, wup_ref[:, col0:col0 + FFN_CHUNK], preferred_element_type=F32)
        c = cb_ref[:, col0:col0 + FFN_CHUNK] + cw_ref[CONV_WIDTH - 1:CONV_WIDTH, col0:col0 + FFN_CHUNK] * u
        for k in range(1, CONV_WIDTH):
            tap = cw_ref[CONV_WIDTH - 1 - k:CONV_WIDTH - k, col0:col0 + FFN_CHUNK]
            c = c + tap * pltpu.roll(u, k, 0)
        return c[FFN_HALO:, :]

    f = jnp.zeros((tm, D_MODEL), F32)
    for j in range(D_FF // FFN_CHUNK):
        act = _gelu_tanh(conv(j * FFN_CHUNK)) * conv(D_FF + j * FFN_CHUNK)
        f = f + jnp.dot(act.astype(BF16), wdn_ref[j * FFN_CHUNK:(j + 1) * FFN_CHUNK, :],
                        preferred_element_type=F32)
    o_ref[...] = x + _rms(f, gpost_ref[...])


def _ffn(x2, g, wup, cw, cb, wdn, gpost, S, tm=512):
    T = x2.shape[0]
    row = pl.BlockSpec((tm, D_MODEL), lambda i: (i, 0))
    halo = pl.BlockSpec((FFN_HALO, D_MODEL), lambda i: (jnp.maximum(i * (tm // FFN_HALO) - 1, 0), 0))
    return pl.pallas_call(
        functools.partial(_ffn_kernel, S=S),
        grid=(T // tm,),
        in_specs=[row, halo, _resident((1, D_MODEL)), _resident(wup.shape), _resident(cw.shape),
                  _resident(cb.shape), _resident(wdn.shape), _resident((1, D_MODEL))],
        out_specs=row,
        out_shape=jax.ShapeDtypeStruct((T, D_MODEL), F32),
        compiler_params=pltpu.CompilerParams(dimension_semantics=("parallel",), vmem_limit_bytes=VMEM_LIMIT),
        name="ffn",
    )(x2, x2, g, wup, cw, cb, wdn, gpost)


def _layer(x2, mem, pos2, B, S, pre_mix_g, w_in, pool_w, pool_scale, cmp_pe, cmp_w1, cmp_w2, mem_norm_g,
           w_mem_kv, w_br_pool, w_br_nsa, w_br_xa, w_out, post_mix_g, pre_ffn_g, w_up, conv_w, conv_b,
           w_down, post_ffn_g):
    o = np.cumsum((0,) + IN_SIZES)
    w_pool, w_q, w_kv, w_gn, w_qx, w_gbr = (w_in[:, o[k]:o[k + 1]] for k in range(6))
    w_gn = jnp.pad(w_gn, ((0, 0), (0, LANES - w_gn.shape[1])))
    vec = lambda a: a.reshape(1, -1)
    q, kc, vc, ks, vst, kw, vwt, gn = _in_proj(x2, pos2, vec(pre_mix_g), w_q, w_kv[:, :128],
                                               w_kv[:, 128:].astype(BF16), w_gn.astype(BF16))
    b3 = lambda a: a.reshape(B, S, a.shape[-1])
    kcc = _compress(b3(kc), cmp_pe[0], cmp_w1[0], cmp_w2[0])
    vcc = _compress(b3(vc), cmp_pe[1], cmp_w1[1], cmp_w2[1])
    ynsa = _nsa(b3(q), kcc, vcc, b3(ks), vst, b3(kw), vwt, b3(gn))
    kvm = _mem_kv(mem, vec(mem_norm_g), w_mem_kv.astype(BF16))
    x2 = _merge(x2, ynsa.reshape(B * S, NSA_WIDTH), kvm, vec(pre_mix_g), w_pool.astype(BF16), w_qx.astype(BF16),
                w_gbr.astype(BF16), pool_w.astype(BF16), vec(pool_scale), w_br_pool.astype(BF16),
                w_br_nsa.astype(BF16), w_br_xa.astype(BF16), w_out.astype(BF16), vec(post_mix_g), S)
    return _ffn(x2, vec(pre_ffn_g), w_up.astype(BF16), conv_w, vec(conv_b), w_down.astype(BF16),
                vec(post_ffn_g), S)


def kernel(x, mem, positions, pre_mix_g, w_in, pool_w, pool_scale, cmp_pe, cmp_w1, cmp_w2, mem_norm_g, w_mem_kv,
           w_br_pool, w_br_nsa, w_br_xa, w_out, post_mix_g, pre_ffn_g, w_up, conv_w, conv_b, w_down, post_ffn_g):
    B, S, D = x.shape
    x2 = x.reshape(B * S, D)
    pos2 = positions.reshape(B * S, 1)
    for l in range(pre_mix_g.shape[0]):
        x2 = _layer(x2, mem, pos2, B, S, pre_mix_g[l], w_in[l], pool_w[l], pool_scale[l], cmp_pe[l], cmp_w1[l],
                    cmp_w2[l], mem_norm_g[l], w_mem_kv[l], w_br_pool[l], w_br_nsa[l], w_br_xa[l], w_out[l],
                    post_mix_g[l], pre_ffn_g[l], w_up[l], conv_w[l], conv_b[l], w_down[l], post_ffn_g[l])
    return x2.reshape(B, S, D)
```

```python
import functools

import numpy as np
import jax
import jax.numpy as jnp
from jax import lax
from jax.experimental import pallas as pl
from jax.experimental.pallas import tpu as pltpu

F32 = jnp.float32
BF16 = jnp.bfloat16
HIGHEST = lax.Precision.HIGHEST

D_MODEL = 1024
EPS = 1e-6
POOL_WINDOWS = (2, 4, 8, 16)
POOL_GROUP = 128
POOL_WIDTH = POOL_GROUP * len(POOL_WINDOWS)
NSA_HEADS = 16
NSA_KV_GROUPS = 2
HEADS_PER_GROUP = NSA_HEADS // NSA_KV_GROUPS
HEAD_DIM = 64
NSA_WIDTH = NSA_HEADS * HEAD_DIM
NSA_KV_WIDTH = NSA_KV_GROUPS * HEAD_DIM
CMP_BLOCK = 32
CMP_STRIDE = 16
CMP_HIDDEN = 256
SEL_BLOCK = 64
SEL_TOPK = 16
WINDOW = 512
ROPE_THETA = 500000.0
ROT_DIM = HEAD_DIM // 4
XA_HEADS = 4
XA_HEAD_DIM = 128
XA_WIDTH = XA_HEADS * XA_HEAD_DIM
N_BRANCHES = 3
D_FF = 2816
CONV_WIDTH = 3
IN_SIZES = (POOL_WIDTH, NSA_WIDTH, 6 * NSA_KV_WIDTH, 3 * NSA_HEADS, XA_WIDTH, N_BRANCHES * D_MODEL)

LANES = 128
Q_TILE = 128
SEL_KEY_TILE = 512
COL_CHUNK = 2048
MASK_NEG = -1e9
SCORE_FLOOR = -1e30
VMEM_LIMIT = 56 * 1024 * 1024
Q_SCALE = HEAD_DIM ** -0.5 * 1.4426950408889634


def _rms(x, g):
    return x * lax.rsqrt(jnp.mean(x * x, axis=-1, keepdims=True) + EPS) * g


def _gelu_tanh(x):
    return 0.5 * x * (1.0 + jnp.tanh(0.7978845608028654 * (x + 0.044715 * (x * x * x))))


def _split_bf16(x):
    hi = x.astype(BF16)
    return hi, (x - hi.astype(F32)).astype(BF16)


def _dot_nt(a, b, precision=None):
    return lax.dot_general(a, b, (((1,), (1,)), ((), ())), precision=precision,
                           preferred_element_type=F32)


def _resident(shape):
    return pl.BlockSpec(shape, lambda *_: (0,) * len(shape), pipeline_mode=pl.Buffered(1))


def _inproj_kernel(x_ref, pos_ref, g_ref, wq_ref, wkc_ref, wkv_ref, wgn_ref, frq_ref, sg1_ref, sg2_ref,
                   q_ref, kc_ref, vc_ref, ks_ref, vs_ref, kw_ref, vw_ref, gn_ref):
    h = _rms(x_ref[...], g_ref[...])
    hb = h.astype(BF16)
    ang = pos_ref[...].astype(F32) * frq_ref[...]
    c = jnp.cos(ang)
    s = jnp.sin(ang)
    s1 = s * sg1_ref[...]
    s2 = s * sg2_ref[...]

    def rope(v):
        return v * c + pltpu.roll(v, LANES - ROT_DIM // 2, 1) * s1 + pltpu.roll(v, ROT_DIM // 2, 1) * s2

    q = jnp.dot(h, wq_ref[...], precision=HIGHEST, preferred_element_type=F32)
    for j in range(NSA_WIDTH // LANES):
        q_ref[:, j * LANES:(j + 1) * LANES] = rope(q[:, j * LANES:(j + 1) * LANES]) * Q_SCALE
    kc_ref[...] = rope(jnp.dot(h, wkc_ref[...], precision=HIGHEST, preferred_element_type=F32))
    kv = jnp.dot(hb, wkv_ref[...], preferred_element_type=F32)
    vc_ref[...] = kv[:, 0:128]
    ks_ref[...] = rope(kv[:, 128:256]).astype(BF16)
    vs_ref[...] = kv[:, 256:384].T.astype(BF16)
    kw_ref[...] = rope(kv[:, 384:512]).astype(BF16)
    vw_ref[...] = kv[:, 512:640].T.astype(BF16)
    gn_ref[...] = jax.nn.sigmoid(jnp.dot(hb, wgn_ref[...], preferred_element_type=F32))


def _in_proj(x2, pos2, g, wq, wkc, wkv, wgn, tm=512):
    T = x2.shape[0]
    half = ROT_DIM // 2
    d = np.arange(LANES) % HEAD_DIM
    inv_freq = ROPE_THETA ** (-jnp.arange(half, dtype=F32) * (2.0 / ROT_DIM))
    frq = jnp.where(d < ROT_DIM, inv_freq[d % half], 0.0)[None, :]
    sg1 = np.where(d < half, -1.0, 0.0).astype(np.float32)[None, :]
    sg2 = np.where((d >= half) & (d < ROT_DIM), 1.0, 0.0).astype(np.float32)[None, :]
    row = lambda w: pl.BlockSpec((tm, w), lambda i: (i, 0))
    col = pl.BlockSpec((128, tm), lambda i: (0, i))
    out_shape = [jax.ShapeDtypeStruct((T, NSA_WIDTH), F32),
                 jax.ShapeDtypeStruct((T, 128), F32), jax.ShapeDtypeStruct((T, 128), F32),
                 jax.ShapeDtypeStruct((T, 128), BF16), jax.ShapeDtypeStruct((128, T), BF16),
                 jax.ShapeDtypeStruct((T, 128), BF16), jax.ShapeDtypeStruct((128, T), BF16),
                 jax.ShapeDtypeStruct((T, 128), F32)]
    return pl.pallas_call(
        _inproj_kernel,
        grid=(T // tm,),
        in_specs=[row(D_MODEL), row(1), _resident((1, D_MODEL)), _resident(wq.shape), _resident(wkc.shape),
                  _resident(wkv.shape), _resident(wgn.shape), _resident((1, LANES)), _resident((1, LANES)),
                  _resident((1, LANES))],
        out_specs=[row(NSA_WIDTH), row(128), row(128), row(128), col, row(128), col, row(128)],
        out_shape=out_shape,
        compiler_params=pltpu.CompilerParams(dimension_semantics=("parallel",), vmem_limit_bytes=VMEM_LIMIT),
        name="in_proj",
    )(x2, pos2, g, wq, wkc, wkv, wgn, frq, jnp.asarray(sg1), jnp.asarray(sg2))


def _compress_kernel(c_ref, pet_ref, peb_ref, wt_ref, wb_ref, w2_ref, o_ref):
    c = c_ref[0]
    a = jnp.dot(c + pet_ref[...], wt_ref[...], precision=HIGHEST, preferred_element_type=F32)
    b = jnp.dot(c + peb_ref[...], wb_ref[...], precision=HIGHEST, preferred_element_type=F32)
    n = c.shape[0]
    hid = a + pltpu.roll(b, n - 1, 0)
    o_ref[0] = jnp.dot(_gelu_tanh(hid), w2_ref[...], precision=HIGHEST, preferred_element_type=F32)


def _compress(kv, pe, w1, w2):
    B, S, _ = kv.shape
    n = S // CMP_STRIDE
    G = NSA_KV_GROUPS
    half = CMP_BLOCK // 2
    eye = jnp.eye(G, dtype=F32)

    def blockdiag_w1(w):
        w = w.reshape(half, 1, HEAD_DIM, 1, CMP_HIDDEN) * eye.reshape(1, G, 1, G, 1)
        return w.reshape(half * G * HEAD_DIM, G * CMP_HIDDEN)

    wt = blockdiag_w1(w1[:half * HEAD_DIM])
    wb = blockdiag_w1(w1[half * HEAD_DIM:])
    w2b = (w2.reshape(1, CMP_HIDDEN, 1, HEAD_DIM) * eye.reshape(G, 1, G, 1)).reshape(G * CMP_HIDDEN, G * HEAD_DIM)
    pet = jnp.broadcast_to(pe[:half, None, :], (half, G, HEAD_DIM)).reshape(1, half * G * HEAD_DIM)
    peb = jnp.broadcast_to(pe[half:, None, :], (half, G, HEAD_DIM)).reshape(1, half * G * HEAD_DIM)
    c = kv.reshape(B, n, CMP_STRIDE * 128)
    return pl.pallas_call(
        _compress_kernel,
        grid=(B,),
        in_specs=[pl.BlockSpec((1, n, CMP_STRIDE * 128), lambda b: (b, 0, 0)),
                  _resident(pet.shape), _resident(peb.shape), _resident(wt.shape), _resident(wb.shape),
                  _resident(w2b.shape)],
        out_specs=pl.BlockSpec((1, n, 128), lambda b: (b, 0, 0)),
        out_shape=jax.ShapeDtypeStruct((B, n, 128), F32),
        compiler_params=pltpu.CompilerParams(dimension_semantics=("parallel",), vmem_limit_bytes=VMEM_LIMIT),
        name="compress",
    )(c, pet, peb, wt, wb, w2b)


def _memkv_kernel(m_ref, g_ref, w_ref, o_ref):
    mn = _rms(m_ref[0], g_ref[...])
    o_ref[0] = jnp.dot(mn.astype(BF16), w_ref[...], preferred_element_type=F32).astype(BF16)


def _mem_kv(mem, g, w):
    B, M, _ = mem.shape
    return pl.pallas_call(
        _memkv_kernel,
        grid=(B,),
        in_specs=[pl.BlockSpec((1, M, D_MODEL), lambda b: (b, 0, 0)), _resident((1, D_MODEL)),
                  _resident(w.shape)],
        out_specs=pl.BlockSpec((1, M, 2 * XA_WIDTH), lambda b: (b, 0, 0)),
        out_shape=jax.ShapeDtypeStruct((B, M, 2 * XA_WIDTH), BF16),
        compiler_params=pltpu.CompilerParams(dimension_semantics=("parallel",), vmem_limit_bytes=VMEM_LIMIT),
        name="mem_kv",
    )(mem, g, w)


def _nsa_kernel(q_ref, kcc_ref, vcc_ref, ks_ref, vst_ref, kw_ref, vwt_ref, gn_ref, ovt_ref, o_ref,
                qst_ref, qat_ref, oct_ref, ost_ref, owt_ref, m_ref, l_ref):
    qt = pl.program_id(1)
    q0 = qt * Q_TILE
    n_cmp = kcc_ref.shape[1]
    heads_per_chunk = COL_CHUNK // Q_TILE
    n_chunks = NSA_HEADS * Q_TILE // COL_CHUNK
    chunk_cols = [slice(c * COL_CHUNK, (c + 1) * COL_CHUNK) for c in range(n_chunks)]
    head_cols = [slice(h * Q_TILE, (h + 1) * Q_TILE) for h in range(NSA_HEADS)]
    group_cols = [slice(g * HEADS_PER_GROUP * Q_TILE, (g + 1) * HEADS_PER_GROUP * Q_TILE)
                  for g in range(NSA_KV_GROUPS)]
    per_head = lambda a, n=heads_per_chunk: jnp.concatenate([a] * n, axis=1)

    zeros = jnp.zeros((HEAD_DIM, Q_TILE), F32)
    for j in range(NSA_HEADS // 2):
        t = q_ref[0, :, j * LANES:(j + 1) * LANES].T
        for k in range(2):
            h = 2 * j + k
            part = t[k * HEAD_DIM:(k + 1) * HEAD_DIM, :]
            pair = [part, zeros] if h // HEADS_PER_GROUP == 0 else [zeros, part]
            qst_ref[:, head_cols[h]] = jnp.concatenate(pair, axis=0)
    qat_ref[0:LANES, :] = qst_ref[...].astype(BF16)

    crow = lax.broadcasted_iota(jnp.int32, (n_cmp, Q_TILE), 0)
    cq = lax.broadcasted_iota(jnp.int32, (n_cmp, Q_TILE), 1) + q0
    valid_c = crow * CMP_STRIDE + (CMP_BLOCK - 1) <= cq
    bias_c = per_head(jnp.where(valid_c, 0.0, SCORE_FLOOR), HEADS_PER_GROUP)
    keep_c = per_head(jnp.where(valid_c, 1.0, 0.0), HEADS_PER_GROUP)
    k_hi, k_lo = _split_bf16(kcc_ref[0])
    kcat = jnp.concatenate([k_hi, k_hi, k_lo], axis=1)
    vcct = vcc_ref[0].T.astype(BF16)
    brow = lax.broadcasted_iota(jnp.int32, (LANES, Q_TILE), 0)
    cur = (lax.broadcasted_iota(jnp.int32, (LANES, Q_TILE), 1) + q0) // SEL_BLOCK
    forced = (brow == 0) | (brow == cur) | (brow == cur - 1)
    brow_f = brow.astype(F32)
    for g in range(NSA_KV_GROUPS):
        cols = group_cols[g]
        q_hi, q_lo = _split_bf16(qst_ref[:, cols])
        s = jnp.dot(kcat, jnp.concatenate([q_hi, q_lo, q_hi], axis=0), preferred_element_type=F32) + bias_c
        p = jnp.exp2(s - jnp.max(s, axis=0, keepdims=True)) * keep_c
        p = p / jnp.maximum(jnp.sum(p, axis=0, keepdims=True), jnp.finfo(F32).tiny)
        pc_sum = p[:, 0:Q_TILE]
        for k in range(1, HEADS_PER_GROUP):
            pc_sum = pc_sum + p[:, k * Q_TILE:(k + 1) * Q_TILE]
        oct_ref[:, cols] = jnp.dot(vcct, p.astype(BF16), preferred_element_type=F32)
        imp = jnp.dot(ovt_ref[...], pc_sum, precision=HIGHEST, preferred_element_type=F32)
        v = jnp.where(brow > cur, -1.0, jnp.where(forced, jnp.inf, imp))
        sel = jnp.zeros((LANES, Q_TILE), F32)
        for _ in range(SEL_TOPK):
            mx = jnp.max(v, axis=0, keepdims=True)
            first = jnp.min(jnp.where(v == mx, brow_f, float(LANES)), axis=0, keepdims=True)
            pick = brow_f == first
            sel = jnp.where(pick, 1.0, sel)
            v = jnp.where(pick, -2.0, v)
        neg = jnp.where((sel > 0.0) & (brow < 2 * qt), 0.0, MASK_NEG).astype(BF16)
        for h in range(HEADS_PER_GROUP):
            qat_ref[LANES:2 * LANES, head_cols[g * HEADS_PER_GROUP + h]] = neg

    m_ref[...] = jnp.full(m_ref.shape, SCORE_FLOOR, F32)
    l_ref[...] = jnp.zeros(l_ref.shape, F32)
    ost_ref[...] = jnp.zeros(ost_ref.shape, F32)
    krow = lax.broadcasted_iota(jnp.int32, (SEL_KEY_TILE, LANES), 0)
    klane = lax.broadcasted_iota(jnp.int32, (SEL_KEY_TILE, LANES), 1)

    def online_update(cols, s, vt):
        m_old = m_ref[:, cols]
        m_new = jnp.maximum(m_old, jnp.max(s, axis=0, keepdims=True))
        alpha = jnp.exp2(m_old - m_new)
        p = jnp.exp2(s - m_new)
        l_ref[:, cols] = alpha * l_ref[:, cols] + jnp.sum(p, axis=0, keepdims=True)
        ost_ref[:, cols] = alpha * ost_ref[:, cols] + jnp.dot(vt, p.astype(BF16), preferred_element_type=F32)
        m_ref[:, cols] = m_new

    def key_tile(kt, carry):
        k0 = pl.multiple_of(kt * SEL_KEY_TILE, SEL_KEY_TILE)
        onehot = jnp.where((k0 + krow) // SEL_BLOCK == klane, 1.0, 0.0).astype(BF16)
        ka = jnp.concatenate([ks_ref[0, pl.ds(k0, SEL_KEY_TILE), :], onehot], axis=1)
        vt = vst_ref[:, pl.ds(k0, SEL_KEY_TILE)]
        for cols in chunk_cols:
            online_update(cols, jnp.dot(ka, qat_ref[:, cols], preferred_element_type=F32), vt)
        return carry

    lax.fori_loop(0, (q0 + SEL_KEY_TILE - 1) // SEL_KEY_TILE, key_tile, 0)

    qd = pl.multiple_of(q0, Q_TILE)
    kd = ks_ref[0, pl.ds(qd, Q_TILE), :]
    vdt = vst_ref[:, pl.ds(qd, Q_TILE)]
    drow = lax.broadcasted_iota(jnp.int32, (Q_TILE, Q_TILE), 0)
    dcol = lax.broadcasted_iota(jnp.int32, (Q_TILE, Q_TILE), 1)
    bias_d = per_head(jnp.where(drow <= dcol, 0.0, SCORE_FLOOR))
    for cols in chunk_cols:
        online_update(cols, jnp.dot(kd, qat_ref[0:LANES, cols], preferred_element_type=F32) + bias_d, vdt)

    wlen = WINDOW + Q_TILE
    w0 = pl.multiple_of(jnp.maximum(q0 - WINDOW, 0), Q_TILE)
    kwn = kw_ref[0, pl.ds(w0, wlen), :]
    vwnt = vwt_ref[:, pl.ds(w0, wlen)]
    kidx = w0 + lax.broadcasted_iota(jnp.int32, (wlen, Q_TILE), 0)
    tqw = q0 + lax.broadcasted_iota(jnp.int32, (wlen, Q_TILE), 1)
    bias_w = per_head(jnp.where((kidx <= tqw) & (tqw - kidx < WINDOW), 0.0, SCORE_FLOOR))
    for cols in chunk_cols:
        s = jnp.dot(kwn, qat_ref[0:LANES, cols], preferred_element_type=F32) + bias_w
        p = jnp.exp2(s - jnp.max(s, axis=0, keepdims=True))
        o = jnp.dot(vwnt, p.astype(BF16), preferred_element_type=F32)
        owt_ref[:, cols] = o / jnp.sum(p, axis=0, keepdims=True)

    gt = gn_ref[0].T
    for j in range(NSA_HEADS // 2):
        parts = []
        for h in (2 * j, 2 * j + 1):
            g = h // HEADS_PER_GROUP
            rows = slice(g * HEAD_DIM, (g + 1) * HEAD_DIM)
            cols = head_cols[h]
            o_sel = ost_ref[rows, cols] / l_ref[:, cols]
            parts.append(gt[3 * h:3 * h + 1, :] * oct_ref[rows, cols] + gt[3 * h + 1:3 * h + 2, :] * o_sel
                         + gt[3 * h + 2:3 * h + 3, :] * owt_ref[rows, cols])
        o_ref[0, :, j * LANES:(j + 1) * LANES] = jnp.concatenate(parts, axis=0).T.astype(o_ref.dtype)


def _nsa(q, kcc, vcc, ks, vst, kw, vwt, gn):
    B, S, _ = q.shape
    n_cmp = kcc.shape[1]
    n_sel = S // SEL_BLOCK
    assert n_sel <= LANES and S % SEL_KEY_TILE == 0 and S >= WINDOW + Q_TILE
    ci = np.arange(n_cmp)[None, :] * CMP_STRIDE
    sj = np.arange(LANES)[:, None] * SEL_BLOCK
    overlap_t = ((ci < sj + SEL_BLOCK) & (ci + CMP_BLOCK > sj)).astype(np.float32)
    cols = NSA_HEADS * Q_TILE
    full = lambda n, w: pl.BlockSpec((1, n, w), lambda b, i: (b, 0, 0))
    full_t = pl.BlockSpec((128, S), lambda b, i: (0, b))
    tile = lambda w: pl.BlockSpec((1, Q_TILE, w), lambda b, i: (b, i, 0))
    return pl.pallas_call(
        _nsa_kernel,
        grid=(B, S // Q_TILE),
        in_specs=[tile(NSA_WIDTH), full(n_cmp, 128), full(n_cmp, 128), full(S, 128), full_t,
                  full(S, 128), full_t, tile(128), _resident(overlap_t.shape)],
        out_specs=tile(NSA_WIDTH),
        out_shape=jax.ShapeDtypeStruct((B, S, NSA_WIDTH), BF16),
        scratch_shapes=[pltpu.VMEM((LANES, cols), F32), pltpu.VMEM((2 * LANES, cols), BF16),
                        pltpu.VMEM((LANES, cols), F32), pltpu.VMEM((LANES, cols), F32),
                        pltpu.VMEM((LANES, cols), F32), pltpu.VMEM((1, cols), F32),
                        pltpu.VMEM((1, cols), F32)],
        compiler_params=pltpu.CompilerParams(dimension_semantics=("parallel", "parallel"),
                                             vmem_limit_bytes=VMEM_LIMIT),
        name="nsa",
    )(q, kcc, vcc, ks, vst, kw, vwt, gn, jnp.asarray(overlap_t))


POOL_HALO = 16


def _merge_kernel(x_ref, xh_ref, yn_ref, kvm_ref, g_ref, wpool_ref, wqx_ref, wgbr_ref, poolw_ref, pscale_ref,
                  wbp_ref, wbn_ref, wbx_ref, wout_ref, gpost_ref, o_ref, *, S):
    i = pl.program_id(0)
    tm = x_ref.shape[0]
    t0 = (i * tm) % S
    x = x_ref[...]
    g = g_ref[...]
    h = _rms(x, g)
    hb = h.astype(BF16)

    hh = _rms(xh_ref[...], g).astype(BF16)
    u_halo = jnp.dot(hh, wpool_ref[...], preferred_element_type=F32)
    u_halo = jnp.where(t0 > 0, u_halo, 0.0)
    u = jnp.dot(hb, wpool_ref[...], preferred_element_type=F32)
    ue = jnp.concatenate([u_halo, u], axis=0)
    trow = t0 + lax.broadcasted_iota(jnp.int32, (tm, 1), 0) + 1
    ypool = []
    for gi, w in enumerate(POOL_WINDOWS):
        acc = ue[:, gi * POOL_GROUP:(gi + 1) * POOL_GROUP]
        step = 1
        while step < w:
            acc = acc + pltpu.roll(acc, step, 0)
            step *= 2
        cnt = jnp.minimum(trow, w).astype(F32)
        p = acc[POOL_HALO:, :] / cnt - u[:, gi * POOL_GROUP:(gi + 1) * POOL_GROUP]
        ypool.append(jnp.dot(p.astype(BF16), poolw_ref[gi], preferred_element_type=F32))
    ypool = jnp.concatenate(ypool, axis=1) * pscale_ref[...]

    qx = jnp.dot(hb, wqx_ref[...], preferred_element_type=F32)
    ymem = []
    for hd in range(XA_HEADS):
        km = kvm_ref[0, :, hd * XA_HEAD_DIM:(hd + 1) * XA_HEAD_DIM]
        vm = kvm_ref[0, :, XA_WIDTH + hd * XA_HEAD_DIM:XA_WIDTH + (hd + 1) * XA_HEAD_DIM]
        s = _dot_nt(qx[:, hd * XA_HEAD_DIM:(hd + 1) * XA_HEAD_DIM].astype(BF16), km) * (XA_HEAD_DIM ** -0.5)
        e = jnp.exp(s - jnp.max(s, axis=-1, keepdims=True))
        p = e / jnp.sum(e, axis=-1, keepdims=True)
        ymem.append(jnp.dot(p.astype(BF16), vm, preferred_element_type=F32))
    ymem = jnp.concatenate(ymem, axis=1)

    gbr = jax.nn.sigmoid(jnp.dot(hb, wgbr_ref[...], preferred_element_type=F32))
    y = (gbr[:, 0:D_MODEL] * jnp.dot(ypool.astype(BF16), wbp_ref[...], preferred_element_type=F32)
         + gbr[:, D_MODEL:2 * D_MODEL] * jnp.dot(yn_ref[...], wbn_ref[...], preferred_element_type=F32)
         + gbr[:, 2 * D_MODEL:3 * D_MODEL] * jnp.dot(ymem.astype(BF16), wbx_ref[...], preferred_element_type=F32))
    o = jnp.dot(y.astype(BF16), wout_ref[...], preferred_element_type=F32)
    o_ref[...] = x + _rms(o, gpost_ref[...])


def _merge(x2, ynsa, kvm, g, wpool, wqx, wgbr, poolw, pscale, wbp, wbn, wbx, wout, gpost, S, tm=256):
    T = x2.shape[0]
    M = kvm.shape[1]
    row = lambda w: pl.BlockSpec((tm, w), lambda i: (i, 0))
    halo = pl.BlockSpec((POOL_HALO, D_MODEL), lambda i: (jnp.maximum(i * (tm // POOL_HALO) - 1, 0), 0))
    return pl.pallas_call(
        functools.partial(_merge_kernel, S=S),
        grid=(T // tm,),
        in_specs=[row(D_MODEL), halo, row(NSA_WIDTH),
                  pl.BlockSpec((1, M, 2 * XA_WIDTH), lambda i: ((i * tm) // S, 0, 0)),
                  _resident((1, D_MODEL)), _resident(wpool.shape), _resident(wqx.shape), _resident(wgbr.shape),
                  _resident(poolw.shape), _resident(pscale.shape), _resident(wbp.shape), _resident(wbn.shape),
                  _resident(wbx.shape), _resident(wout.shape), _resident((1, D_MODEL))],
        out_specs=row(D_MODEL),
        out_shape=jax.ShapeDtypeStruct((T, D_MODEL), F32),
        compiler_params=pltpu.CompilerParams(dimension_semantics=("parallel",), vmem_limit_bytes=VMEM_LIMIT),
        name="merge",
    )(x2, x2, ynsa, kvm, g, wpool, wqx, wgbr, poolw, pscale, wbp, wbn, wbx, wout, gpost)


FFN_HALO = 8
FFN_CHUNK = 1408


def _ffn_kernel(x_ref, xh_ref, g_ref, wup_ref, cw_ref, cb_ref, wdn_ref, gpost_ref, o_ref, *, S):
    i = pl.program_id(0)
    tm = x_ref.shape[0]
    t0 = (i * tm) % S
    x = x_ref[...]
    g = g_ref[...]
    hh = jnp.where(t0 > 0, _rms(xh_ref[...], g), 0.0)
    he = jnp.concatenate([hh, _rms(x, g)], axis=0).astype(BF16)

    def conv(col0):
        u = jnp.dot(he, wup_ref[:, col0:col0 + FFN_CHUNK], preferred_element_type=F32)
        c = cb_ref[:, col0:col0 + FFN_CHUNK] + cw_ref[CONV_WIDTH - 1:CONV_WIDTH, col0:col0 + FFN_CHUNK] * u
        for k in range(1, CONV_WIDTH):
            tap = cw_ref[CONV_WIDTH - 1 - k:CONV_WIDTH - k, col0:col0 + FFN_CHUNK]
            c = c + tap * pltpu.roll(u, k, 0)
        return c[FFN_HALO:, :]

    f = jnp.zeros((tm, D_MODEL), F32)
    for j in range(D_FF // FFN_CHUNK):
        act = _gelu_tanh(conv(j * FFN_CHUNK)) * conv(D_FF + j * FFN_CHUNK)
        f = f + jnp.dot(act.astype(BF16), wdn_ref[j * FFN_CHUNK:(j + 1) * FFN_CHUNK, :],
                        preferred_element_type=F32)
    o_ref[...] = x + _rms(f, gpost_ref[...])


def _ffn(x2, g, wup, cw, cb, wdn, gpost, S, tm=512):
    T = x2.shape[0]
    row = pl.BlockSpec((tm, D_MODEL), lambda i: (i, 0))
    halo = pl.BlockSpec((FFN_HALO, D_MODEL), lambda i: (jnp.maximum(i * (tm // FFN_HALO) - 1, 0), 0))
    return pl.pallas_call(
        functools.partial(_ffn_kernel, S=S),
        grid=(T // tm,),
        in_specs=[row, halo, _resident((1, D_MODEL)), _resident(wup.shape), _resident(cw.shape),
                  _resident(cb.shape), _resident(wdn.shape), _resident((1, D_MODEL))],
        out_specs=row,
        out_shape=jax.ShapeDtypeStruct((T, D_MODEL), F32),
        compiler_params=pltpu.CompilerParams(dimension_semantics=("parallel",), vmem_limit_bytes=VMEM_LIMIT),
        name="ffn",
    )(x2, x2, g, wup, cw, cb, wdn, gpost)


def _layer(x2, mem, pos2, B, S, pre_mix_g, w_in, pool_w, pool_scale, cmp_pe, cmp_w1, cmp_w2, mem_norm_g,
           w_mem_kv, w_br_pool, w_br_nsa, w_br_xa, w_out, post_mix_g, pre_ffn_g, w_up, conv_w, conv_b,
           w_down, post_ffn_g):
    o = np.cumsum((0,) + IN_SIZES)
    w_pool, w_q, w_kv, w_gn, w_qx, w_gbr = (w_in[:, o[k]:o[k + 1]] for k in range(6))
    w_gn = jnp.pad(w_gn, ((0, 0), (0, LANES - w_gn.shape[1])))
    vec = lambda a: a.reshape(1, -1)
    q, kc, vc, ks, vst, kw, vwt, gn = _in_proj(x2, pos2, vec(pre_mix_g), w_q, w_kv[:, :128],
                                               w_kv[:, 128:].astype(BF16), w_gn.astype(BF16))
    b3 = lambda a: a.reshape(B, S, a.shape[-1])
    kcc = _compress(b3(kc), cmp_pe[0], cmp_w1[0], cmp_w2[0])
    vcc = _compress(b3(vc), cmp_pe[1], cmp_w1[1], cmp_w2[1])
    ynsa = _nsa(b3(q), kcc, vcc, b3(ks), vst, b3(kw), vwt, b3(gn))
    kvm = _mem_kv(mem, vec(mem_norm_g), w_mem_kv.astype(BF16))
    x2 = _merge(x2, ynsa.reshape(B * S, NSA_WIDTH), kvm, vec(pre_mix_g), w_pool.astype(BF16), w_qx.astype(BF16),
                w_gbr.astype(BF16), pool_w.astype(BF16), vec(pool_scale), w_br_pool.astype(BF16),
                w_br_nsa.astype(BF16), w_br_xa.astype(BF16), w_out.astype(BF16), vec(post_mix_g), S)
    return _ffn(x2, vec(pre_ffn_g), w_up.astype(BF16), conv_w, vec(conv_b), w_down.astype(BF16),
                vec(post_ffn_g), S)


def kernel(x, mem, positions, pre_mix_g, w_in, pool_w, pool_scale, cmp_pe, cmp_w1, cmp_w2, mem_norm_g, w_mem_kv,
           w_br_pool, w_br_nsa, w_br_xa, w_out, post_mix_g, pre_ffn_g, w_up, conv_w, conv_b, w_down, post_ffn_g):
    B, S, D = x.shape
    x2 = x.reshape(B * S, D)
    pos2 = positions.reshape(B * S, 1)
    for l in range(pre_mix_g.shape[0]):
        x2 = _layer(x2, mem, pos2, B, S, pre_mix_g[l], w_in[l], pool_w[l], pool_scale[l], cmp_pe[l], cmp_w1[l],
                    cmp_w2[l], mem_norm_g[l], w_mem_kv[l], w_br_pool[l], w_br_nsa[l], w_br_xa[l], w_out[l],
                    post_mix_g[l], pre_ffn_g[l], w_up[l], conv_w[l], conv_b[l], w_down[l], post_ffn_g[l])
    return x2.reshape(B, S, D)
```

```python
import functools

import numpy as np
import jax
import jax.numpy as jnp
from jax import lax
from jax.experimental import pallas as pl
from jax.experimental.pallas import tpu as pltpu

F32 = jnp.float32
BF16 = jnp.bfloat16
HIGHEST = lax.Precision.HIGHEST

D_MODEL = 1024
EPS = 1e-6
POOL_WINDOWS = (2, 4, 8, 16)
POOL_GROUP = 128
POOL_WIDTH = POOL_GROUP * len(POOL_WINDOWS)
NSA_HEADS = 16
NSA_KV_GROUPS = 2
HEADS_PER_GROUP = NSA_HEADS // NSA_KV_GROUPS
HEAD_DIM = 64
NSA_WIDTH = NSA_HEADS * HEAD_DIM
NSA_KV_WIDTH = NSA_KV_GROUPS * HEAD_DIM
CMP_BLOCK = 32
CMP_STRIDE = 16
CMP_HIDDEN = 256
SEL_BLOCK = 64
SEL_TOPK = 16
WINDOW = 512
ROPE_THETA = 500000.0
ROT_DIM = HEAD_DIM // 4
XA_HEADS = 4
XA_HEAD_DIM = 128
XA_WIDTH = XA_HEADS * XA_HEAD_DIM
N_BRANCHES = 3
D_FF = 2816
CONV_WIDTH = 3
IN_SIZES = (POOL_WIDTH, NSA_WIDTH, 6 * NSA_KV_WIDTH, 3 * NSA_HEADS, XA_WIDTH, N_BRANCHES * D_MODEL)

LANES = 128
Q_TILE = 128
SEL_KEY_TILE = 512
COL_CHUNK = 2048
MASK_NEG = -1e9
SCORE_FLOOR = -1e30
VMEM_LIMIT = 56 * 1024 * 1024
Q_SCALE = HEAD_DIM ** -0.5 * 1.4426950408889634


def _rms(x, g):
    return x * lax.rsqrt(jnp.mean(x * x, axis=-1, keepdims=True) + EPS) * g


def _gelu_tanh(x):
    return 0.5 * x * (1.0 + jnp.tanh(0.7978845608028654 * (x + 0.044715 * (x * x * x))))


def _split_bf16(x):
    hi = x.astype(BF16)
    return hi, (x - hi.astype(F32)).astype(BF16)


def _weights3(w):
    hi, lo = _split_bf16(w)
    return jnp.concatenate([hi, hi, lo], axis=0)


def _dot3(x, w3):
    hi, lo = _split_bf16(x)
    return jnp.dot(jnp.concatenate([hi, lo, hi], axis=1), w3, preferred_element_type=F32)


def _dot_nt(a, b, precision=None):
    return lax.dot_general(a, b, (((1,), (1,)), ((), ())), precision=precision,
                           preferred_element_type=F32)


def _resident(shape):
    return pl.BlockSpec(shape, lambda *_: (0,) * len(shape), pipeline_mode=pl.Buffered(1))


def _inproj_kernel(x_ref, pos_ref, g_ref, wq_ref, wkc_ref, wkv_ref, wgn_ref, frq_ref, sg1_ref, sg2_ref,
                   q_ref, kc_ref, vc_ref, ks_ref, vs_ref, kw_ref, vw_ref, gn_ref):
    h = _rms(x_ref[...], g_ref[...])
    hb = h.astype(BF16)
    ang = pos_ref[...].astype(F32) * frq_ref[...]
    c = jnp.cos(ang)
    s = jnp.sin(ang)
    s1 = s * sg1_ref[...]
    s2 = s * sg2_ref[...]

    def rope(v):
        return v * c + pltpu.roll(v, LANES - ROT_DIM // 2, 1) * s1 + pltpu.roll(v, ROT_DIM // 2, 1) * s2

    h_hi, h_lo = _split_bf16(h)
    h3 = jnp.concatenate([h_hi, h_lo, h_hi], axis=1)
    q = jnp.dot(h3, wq_ref[...], preferred_element_type=F32)
    for j in range(NSA_WIDTH // LANES):
        q_ref[:, j * LANES:(j + 1) * LANES] = rope(q[:, j * LANES:(j + 1) * LANES]) * Q_SCALE
    kc_ref[...] = rope(jnp.dot(h3, wkc_ref[...], preferred_element_type=F32))
    kv = jnp.dot(hb, wkv_ref[...], preferred_element_type=F32)
    vc_ref[...] = kv[:, 0:128]
    ks_ref[...] = rope(kv[:, 128:256]).astype(BF16)
    vs_ref[...] = kv[:, 256:384].T.astype(BF16)
    kw_ref[...] = rope(kv[:, 384:512]).astype(BF16)
    vw_ref[...] = kv[:, 512:640].T.astype(BF16)
    gn_ref[...] = jax.nn.sigmoid(jnp.dot(hb, wgn_ref[...], preferred_element_type=F32))


def _in_proj(x2, pos2, g, wq, wkc, wkv, wgn, tm=512):
    T = x2.shape[0]
    half = ROT_DIM // 2
    d = np.arange(LANES) % HEAD_DIM
    inv_freq = ROPE_THETA ** (-jnp.arange(half, dtype=F32) * (2.0 / ROT_DIM))
    frq = jnp.where(d < ROT_DIM, inv_freq[d % half], 0.0)[None, :]
    sg1 = np.where(d < half, -1.0, 0.0).astype(np.float32)[None, :]
    sg2 = np.where((d >= half) & (d < ROT_DIM), 1.0, 0.0).astype(np.float32)[None, :]
    row = lambda w: pl.BlockSpec((tm, w), lambda i: (i, 0))
    col = pl.BlockSpec((128, tm), lambda i: (0, i))
    out_shape = [jax.ShapeDtypeStruct((T, NSA_WIDTH), F32),
                 jax.ShapeDtypeStruct((T, 128), F32), jax.ShapeDtypeStruct((T, 128), F32),
                 jax.ShapeDtypeStruct((T, 128), BF16), jax.ShapeDtypeStruct((128, T), BF16),
                 jax.ShapeDtypeStruct((T, 128), BF16), jax.ShapeDtypeStruct((128, T), BF16),
                 jax.ShapeDtypeStruct((T, 128), F32)]
    return pl.pallas_call(
        _inproj_kernel,
        grid=(T // tm,),
        in_specs=[row(D_MODEL), row(1), _resident((1, D_MODEL)), _resident(wq.shape), _resident(wkc.shape),
                  _resident(wkv.shape), _resident(wgn.shape), _resident((1, LANES)), _resident((1, LANES)),
                  _resident((1, LANES))],
        out_specs=[row(NSA_WIDTH), row(128), row(128), row(128), col, row(128), col, row(128)],
        out_shape=out_shape,
        compiler_params=pltpu.CompilerParams(dimension_semantics=("parallel",), vmem_limit_bytes=VMEM_LIMIT),
        name="in_proj",
    )(x2, pos2, g, wq, wkc, wkv, wgn, frq, jnp.asarray(sg1), jnp.asarray(sg2))


def _compress_kernel(c_ref, pet_ref, peb_ref, wt_ref, wb_ref, w2_ref, o_ref):
    c = c_ref[0]
    a = _dot3(c + pet_ref[...], wt_ref[...])
    b = _dot3(c + peb_ref[...], wb_ref[...])
    n = c.shape[0]
    hid = a + pltpu.roll(b, n - 1, 0)
    o_ref[0] = _dot3(_gelu_tanh(hid), w2_ref[...])


def _compress(kv, pe, w1, w2):
    B, S, _ = kv.shape
    n = S // CMP_STRIDE
    G = NSA_KV_GROUPS
    half = CMP_BLOCK // 2
    eye = jnp.eye(G, dtype=F32)

    def blockdiag_w1(w):
        w = w.reshape(half, 1, HEAD_DIM, 1, CMP_HIDDEN) * eye.reshape(1, G, 1, G, 1)
        return w.reshape(half * G * HEAD_DIM, G * CMP_HIDDEN)

    wt = _weights3(blockdiag_w1(w1[:half * HEAD_DIM]))
    wb = _weights3(blockdiag_w1(w1[half * HEAD_DIM:]))
    w2b = _weights3((w2.reshape(1, CMP_HIDDEN, 1, HEAD_DIM) * eye.reshape(G, 1, G, 1))
                    .reshape(G * CMP_HIDDEN, G * HEAD_DIM))
    pet = jnp.broadcast_to(pe[:half, None, :], (half, G, HEAD_DIM)).reshape(1, half * G * HEAD_DIM)
    peb = jnp.broadcast_to(pe[half:, None, :], (half, G, HEAD_DIM)).reshape(1, half * G * HEAD_DIM)
    c = kv.reshape(B, n, CMP_STRIDE * 128)
    return pl.pallas_call(
        _compress_kernel,
        grid=(B,),
        in_specs=[pl.BlockSpec((1, n, CMP_STRIDE * 128), lambda b: (b, 0, 0)),
                  _resident(pet.shape), _resident(peb.shape), _resident(wt.shape), _resident(wb.shape),
                  _resident(w2b.shape)],
        out_specs=pl.BlockSpec((1, n, 128), lambda b: (b, 0, 0)),
        out_shape=jax.ShapeDtypeStruct((B, n, 128), F32),
        compiler_params=pltpu.CompilerParams(dimension_semantics=("parallel",), vmem_limit_bytes=VMEM_LIMIT),
        name="compress",
    )(c, pet, peb, wt, wb, w2b)


def _memkv_kernel(m_ref, g_ref, w_ref, o_ref):
    mn = _rms(m_ref[0], g_ref[...])
    o_ref[0] = jnp.dot(mn.astype(BF16), w_ref[...], preferred_element_type=F32).astype(BF16)


def _mem_kv(mem, g, w):
    B, M, _ = mem.shape
    return pl.pallas_call(
        _memkv_kernel,
        grid=(B,),
        in_specs=[pl.BlockSpec((1, M, D_MODEL), lambda b: (b, 0, 0)), _resident((1, D_MODEL)),
                  _resident(w.shape)],
        out_specs=pl.BlockSpec((1, M, 2 * XA_WIDTH), lambda b: (b, 0, 0)),
        out_shape=jax.ShapeDtypeStruct((B, M, 2 * XA_WIDTH), BF16),
        compiler_params=pltpu.CompilerParams(dimension_semantics=("parallel",), vmem_limit_bytes=VMEM_LIMIT),
        name="mem_kv",
    )(mem, g, w)


def _nsa_kernel(q_ref, kcc_ref, vcc_ref, ks_ref, vst_ref, kw_ref, vwt_ref, gn_ref, ovt_ref, blk_ref, o_ref,
                qst_ref, qat_ref, oct_ref, ost_ref, owt_ref, m_ref, l_ref, sa_ref, sb_ref):
    qt = pl.program_id(1)
    q0 = qt * Q_TILE
    n_cmp = kcc_ref.shape[1]
    heads_per_chunk = COL_CHUNK // Q_TILE
    n_chunks = NSA_HEADS * Q_TILE // COL_CHUNK
    chunk_cols = [slice(c * COL_CHUNK, (c + 1) * COL_CHUNK) for c in range(n_chunks)]
    head_cols = [slice(h * Q_TILE, (h + 1) * Q_TILE) for h in range(NSA_HEADS)]
    group_cols = [slice(g * HEADS_PER_GROUP * Q_TILE, (g + 1) * HEADS_PER_GROUP * Q_TILE)
                  for g in range(NSA_KV_GROUPS)]
    per_head = lambda a, n=heads_per_chunk: jnp.concatenate([a] * n, axis=1)

    zeros = jnp.zeros((HEAD_DIM, Q_TILE), F32)
    for j in range(NSA_HEADS // 2):
        t = q_ref[0, :, j * LANES:(j + 1) * LANES].T
        for k in range(2):
            h = 2 * j + k
            part = t[k * HEAD_DIM:(k + 1) * HEAD_DIM, :]
            pair = [part, zeros] if h // HEADS_PER_GROUP == 0 else [zeros, part]
            qst_ref[:, head_cols[h]] = jnp.concatenate(pair, axis=0)
    qat_ref[0:LANES, :] = qst_ref[...].astype(BF16)

    crow = lax.broadcasted_iota(jnp.int32, (n_cmp, Q_TILE), 0)
    cq = lax.broadcasted_iota(jnp.int32, (n_cmp, Q_TILE), 1) + q0
    valid_c = crow * CMP_STRIDE + (CMP_BLOCK - 1) <= cq
    bias_c = per_head(jnp.where(valid_c, 0.0, SCORE_FLOOR), HEADS_PER_GROUP)
    keep_c = per_head(jnp.where(valid_c, 1.0, 0.0), HEADS_PER_GROUP)
    k_hi, k_lo = _split_bf16(kcc_ref[0])
    kcat = jnp.concatenate([k_hi, k_hi, k_lo], axis=1)
    vcct = vcc_ref[0].T.astype(BF16)
    brow = lax.broadcasted_iota(jnp.int32, (LANES, Q_TILE), 0)
    cur = (lax.broadcasted_iota(jnp.int32, (LANES, Q_TILE), 1) + q0) // SEL_BLOCK
    forced = (brow == 0) | (brow == cur) | (brow == cur - 1)
    brow_f = brow.astype(F32)
    for g in range(NSA_KV_GROUPS):
        cols = group_cols[g]
        q_hi, q_lo = _split_bf16(qst_ref[:, cols])
        s = jnp.dot(kcat, jnp.concatenate([q_hi, q_lo, q_hi], axis=0), preferred_element_type=F32) + bias_c
        p = jnp.exp2(s - jnp.max(s, axis=0, keepdims=True)) * keep_c
        p = p / jnp.maximum(jnp.sum(p, axis=0, keepdims=True), jnp.finfo(F32).tiny)
        pc_sum = p[:, 0:Q_TILE]
        for k in range(1, HEADS_PER_GROUP):
            pc_sum = pc_sum + p[:, k * Q_TILE:(k + 1) * Q_TILE]
        oct_ref[:, cols] = jnp.dot(vcct, p.astype(BF16), preferred_element_type=F32)
        imp = jnp.dot(ovt_ref[...], pc_sum, precision=HIGHEST, preferred_element_type=F32)
        v = jnp.where(brow > cur, -1.0, jnp.where(forced, jnp.inf, imp))
        sel = jnp.zeros((LANES, Q_TILE), F32)
        for _ in range(SEL_TOPK):
            mx = jnp.max(v, axis=0, keepdims=True)
            first = jnp.min(jnp.where(v == mx, brow_f, float(LANES)), axis=0, keepdims=True)
            pick = brow_f == first
            sel = jnp.where(pick, 1.0, sel)
            v = jnp.where(pick, -2.0, v)
        neg = jnp.where((sel > 0.0) & (brow < 2 * qt), 0.0, MASK_NEG).astype(BF16)
        for h in range(HEADS_PER_GROUP):
            qat_ref[LANES:2 * LANES, head_cols[g * HEADS_PER_GROUP + h]] = neg

    m_ref[...] = jnp.full(m_ref.shape, SCORE_FLOOR, F32)
    l_ref[...] = jnp.zeros(l_ref.shape, F32)
    ost_ref[...] = jnp.zeros(ost_ref.shape, F32)

    def scores(kt):
        k0 = pl.multiple_of(kt * SEL_KEY_TILE, SEL_KEY_TILE)
        ka = jnp.concatenate([ks_ref[0, pl.ds(k0, SEL_KEY_TILE), :], blk_ref[pl.ds(k0, SEL_KEY_TILE), :]],
                             axis=1)
        return jnp.dot(ka, qat_ref[...], preferred_element_type=F32)

    def values(start, n):
        return jnp.concatenate([vst_ref[:, pl.ds(start, n)], jnp.ones((16, n), BF16)], axis=0)

    def online_update(s, vt):
        m_old = m_ref[...]
        m_new = jnp.maximum(m_old, jnp.max(s, axis=0, keepdims=True))
        alpha = jnp.exp2(m_old - m_new)
        pv = jnp.dot(vt, jnp.exp2(s - m_new).astype(BF16), preferred_element_type=F32)
        ost_ref[...] = alpha * ost_ref[...] + pv[0:LANES]
        l_ref[...] = alpha * l_ref[...] + pv[LANES:LANES + 1]
        m_ref[...] = m_new

    n_tiles = (q0 + SEL_KEY_TILE - 1) // SEL_KEY_TILE
    last_tile = ks_ref.shape[1] // SEL_KEY_TILE - 1
    sa_ref[...] = scores(0)

    def tile_pair(j, carry):
        t0 = 2 * j
        sb_ref[...] = scores(t0 + 1)
        online_update(sa_ref[...], values(pl.multiple_of(t0 * SEL_KEY_TILE, SEL_KEY_TILE), SEL_KEY_TILE))
        sa_ref[...] = scores(jnp.minimum(t0 + 2, last_tile))
        online_update(sb_ref[...], values(pl.multiple_of((t0 + 1) * SEL_KEY_TILE, SEL_KEY_TILE), SEL_KEY_TILE))
        return carry

    lax.fori_loop(0, (n_tiles + 1) // 2, tile_pair, 0)

    qd = pl.multiple_of(q0, Q_TILE)
    drow = lax.broadcasted_iota(jnp.int32, (Q_TILE, Q_TILE), 0)
    dcol = lax.broadcasted_iota(jnp.int32, (Q_TILE, Q_TILE), 1)
    bias_d = per_head(jnp.where(drow <= dcol, 0.0, SCORE_FLOOR), NSA_HEADS)
    s_diag = jnp.dot(ks_ref[0, pl.ds(qd, Q_TILE), :], qat_ref[0:LANES, :], preferred_element_type=F32) + bias_d
    online_update(s_diag, values(qd, Q_TILE))

    wlen = WINDOW + Q_TILE
    w0 = pl.multiple_of(jnp.maximum(q0 - WINDOW, 0), Q_TILE)
    kwn = kw_ref[0, pl.ds(w0, wlen), :]
    vwnt = vwt_ref[:, pl.ds(w0, wlen)]
    kidx = w0 + lax.broadcasted_iota(jnp.int32, (wlen, Q_TILE), 0)
    tqw = q0 + lax.broadcasted_iota(jnp.int32, (wlen, Q_TILE), 1)
    bias_w = per_head(jnp.where((kidx <= tqw) & (tqw - kidx < WINDOW), 0.0, SCORE_FLOOR))
    for cols in chunk_cols:
        s = jnp.dot(kwn, qat_ref[0:LANES, cols], preferred_element_type=F32) + bias_w
        p = jnp.exp2(s - jnp.max(s, axis=0, keepdims=True))
        o = jnp.dot(vwnt, p.astype(BF16), preferred_element_type=F32)
        owt_ref[:, cols] = o / jnp.sum(p, axis=0, keepdims=True)

    gt = gn_ref[0].T
    for j in range(NSA_HEADS // 2):
        parts = []
        for h in (2 * j, 2 * j + 1):
            g = h // HEADS_PER_GROUP
            rows = slice(g * HEAD_DIM, (g + 1) * HEAD_DIM)
            cols = head_cols[h]
            o_sel = ost_ref[rows, cols] / l_ref[:, cols]
            parts.append(gt[3 * h:3 * h + 1, :] * oct_ref[rows, cols] + gt[3 * h + 1:3 * h + 2, :] * o_sel
                         + gt[3 * h + 2:3 * h + 3, :] * owt_ref[rows, cols])
        o_ref[0, :, j * LANES:(j + 1) * LANES] = jnp.concatenate(parts, axis=0).T.astype(o_ref.dtype)


def _nsa(q, kcc, vcc, ks, vst, kw, vwt, gn):
    B, S, _ = q.shape
    n_cmp = kcc.shape[1]
    n_sel = S // SEL_BLOCK
    assert n_sel <= LANES and S % (2 * SEL_KEY_TILE) == 0 and S >= WINDOW + Q_TILE
    ci = np.arange(n_cmp)[None, :] * CMP_STRIDE
    sj = np.arange(LANES)[:, None] * SEL_BLOCK
    overlap_t = ((ci < sj + SEL_BLOCK) & (ci + CMP_BLOCK > sj)).astype(np.float32)
    block_onehot = (np.arange(S)[:, None] // SEL_BLOCK == np.arange(LANES)[None, :])
    block_onehot = jnp.asarray(block_onehot.astype(np.float32), dtype=BF16)
    cols = NSA_HEADS * Q_TILE
    full = lambda n, w: pl.BlockSpec((1, n, w), lambda b, i: (b, 0, 0))
    full_t = pl.BlockSpec((128, S), lambda b, i: (0, b))
    tile = lambda w: pl.BlockSpec((1, Q_TILE, w), lambda b, i: (b, i, 0))
    return pl.pallas_call(
        _nsa_kernel,
        grid=(B, S // Q_TILE),
        in_specs=[tile(NSA_WIDTH), full(n_cmp, 128), full(n_cmp, 128), full(S, 128), full_t,
                  full(S, 128), full_t, tile(128), _resident(overlap_t.shape), _resident((S, LANES))],
        out_specs=tile(NSA_WIDTH),
        out_shape=jax.ShapeDtypeStruct((B, S, NSA_WIDTH), BF16),
        scratch_shapes=[pltpu.VMEM((LANES, cols), F32), pltpu.VMEM((2 * LANES, cols), BF16),
                        pltpu.VMEM((LANES, cols), F32), pltpu.VMEM((LANES, cols), F32),
                        pltpu.VMEM((LANES, cols), F32), pltpu.VMEM((1, cols), F32),
                        pltpu.VMEM((1, cols), F32), pltpu.VMEM((SEL_KEY_TILE, cols), F32),
                        pltpu.VMEM((SEL_KEY_TILE, cols), F32)],
        compiler_params=pltpu.CompilerParams(dimension_semantics=("parallel", "parallel"),
                                             vmem_limit_bytes=VMEM_LIMIT),
        name="nsa",
    )(q, kcc, vcc, ks, vst, kw, vwt, gn, jnp.asarray(overlap_t), block_onehot)


POOL_HALO = 16


def _merge_kernel(x_ref, xh_ref, yn_ref, kvm_ref, g_ref, wpool_ref, wqx_ref, wgbr_ref, poolw_ref, pscale_ref,
                  wbp_ref, wbn_ref, wbx_ref, wout_ref, gpost_ref, o_ref, *, S):
    i = pl.program_id(0)
    tm = x_ref.shape[0]
    t0 = (i * tm) % S
    x = x_ref[...]
    g = g_ref[...]
    h = _rms(x, g)
    hb = h.astype(BF16)

    hh = _rms(xh_ref[...], g).astype(BF16)
    u_halo = jnp.dot(hh, wpool_ref[...], preferred_element_type=F32)
    u_halo = jnp.where(t0 > 0, u_halo, 0.0)
    u = jnp.dot(hb, wpool_ref[...], preferred_element_type=F32)
    ue = jnp.concatenate([u_halo, u], axis=0)
    trow = t0 + lax.broadcasted_iota(jnp.int32, (tm, 1), 0) + 1
    ypool = []
    for gi, w in enumerate(POOL_WINDOWS):
        acc = ue[:, gi * POOL_GROUP:(gi + 1) * POOL_GROUP]
        step = 1
        while step < w:
            acc = acc + pltpu.roll(acc, step, 0)
            step *= 2
        cnt = jnp.minimum(trow, w).astype(F32)
        p = acc[POOL_HALO:, :] / cnt - u[:, gi * POOL_GROUP:(gi + 1) * POOL_GROUP]
        ypool.append(jnp.dot(p.astype(BF16), poolw_ref[gi], preferred_element_type=F32))
    ypool = jnp.concatenate(ypool, axis=1) * pscale_ref[...]

    qx = jnp.dot(hb, wqx_ref[...], preferred_element_type=F32)
    ymem = []
    for hd in range(XA_HEADS):
        km = kvm_ref[0, :, hd * XA_HEAD_DIM:(hd + 1) * XA_HEAD_DIM]
        vm = kvm_ref[0, :, XA_WIDTH + hd * XA_HEAD_DIM:XA_WIDTH + (hd + 1) * XA_HEAD_DIM]
        s = _dot_nt(qx[:, hd * XA_HEAD_DIM:(hd + 1) * XA_HEAD_DIM].astype(BF16), km) * (XA_HEAD_DIM ** -0.5)
        e = jnp.exp(s - jnp.max(s, axis=-1, keepdims=True))
        p = e / jnp.sum(e, axis=-1, keepdims=True)
        ymem.append(jnp.dot(p.astype(BF16), vm, preferred_element_type=F32))
    ymem = jnp.concatenate(ymem, axis=1)

    gbr = jax.nn.sigmoid(jnp.dot(hb, wgbr_ref[...], preferred_element_type=F32))
    y = (gbr[:, 0:D_MODEL] * jnp.dot(ypool.astype(BF16), wbp_ref[...], preferred_element_type=F32)
         + gbr[:, D_MODEL:2 * D_MODEL] * jnp.dot(yn_ref[...], wbn_ref[...], preferred_element_type=F32)
         + gbr[:, 2 * D_MODEL:3 * D_MODEL] * jnp.dot(ymem.astype(BF16), wbx_ref[...], preferred_element_type=F32))
    o = jnp.dot(y.astype(BF16), wout_ref[...], preferred_element_type=F32)
    o_ref[...] = x + _rms(o, gpost_ref[...])


def _merge(x2, ynsa, kvm, g, wpool, wqx, wgbr, poolw, pscale, wbp, wbn, wbx, wout, gpost, S, tm=256):
    T = x2.shape[0]
    M = kvm.shape[1]
    row = lambda w: pl.BlockSpec((tm, w), lambda i: (i, 0))
    halo = pl.BlockSpec((POOL_HALO, D_MODEL), lambda i: (jnp.maximum(i * (tm // POOL_HALO) - 1, 0), 0))
    return pl.pallas_call(
        functools.partial(_merge_kernel, S=S),
        grid=(T // tm,),
        in_specs=[row(D_MODEL), halo, row(NSA_WIDTH),
                  pl.BlockSpec((1, M, 2 * XA_WIDTH), lambda i: ((i * tm) // S, 0, 0)),
                  _resident((1, D_MODEL)), _resident(wpool.shape), _resident(wqx.shape), _resident(wgbr.shape),
                  _resident(poolw.shape), _resident(pscale.shape), _resident(wbp.shape), _resident(wbn.shape),
                  _resident(wbx.shape), _resident(wout.shape), _resident((1, D_MODEL))],
        out_specs=row(D_MODEL),
        out_shape=jax.ShapeDtypeStruct((T, D_MODEL), F32),
        compiler_params=pltpu.CompilerParams(dimension_semantics=("parallel",), vmem_limit_bytes=VMEM_LIMIT),
        name="merge",
    )(x2, x2, ynsa, kvm, g, wpool, wqx, wgbr, poolw, pscale, wbp, wbn, wbx, wout, gpost)


FFN_HALO = 8
FFN_CHUNK = 1408


def _ffn_kernel(x_ref, xh_ref, g_ref, wup_ref, cw_ref, cb_ref, wdn_ref, gpost_ref, o_ref, *, S):
    i = pl.program_id(0)
    tm = x_ref.shape[0]
    t0 = (i * tm) % S
    x = x_ref[...]
    g = g_ref[...]
    hh = jnp.where(t0 > 0, _rms(xh_ref[...], g), 0.0)
    he = jnp.concatenate([hh, _rms(x, g)], axis=0).astype(BF16)

    def conv(col0):
        u = jnp.dot(he, wup_ref[:, col0:col0 + FFN_CHUNK], preferred_element_type=F32)
        c = cb_ref[:, col0:col0 + FFN_CHUNK] + cw_ref[CONV_WIDTH - 1:CONV_WIDTH, col0:col0 + FFN_CHUNK] * u
        for k in range(1, CONV_WIDTH):
            tap = cw_ref[CONV_WIDTH - 1 - k:CONV_WIDTH - k, col0:col0 + FFN_CHUNK]
            c = c + tap * pltpu.roll(u, k, 0)
        return c[FFN_HALO:, :]

    f = jnp.zeros((tm, D_MODEL), F32)
    for j in range(D_FF // FFN_CHUNK):
        act = _gelu_tanh(conv(j * FFN_CHUNK)) * conv(D_FF + j * FFN_CHUNK)
        f = f + jnp.dot(act.astype(BF16), wdn_ref[j * FFN_CHUNK:(j + 1) * FFN_CHUNK, :],
                        preferred_element_type=F32)
    o_ref[...] = x + _rms(f, gpost_ref[...])


def _ffn(x2, g, wup, cw, cb, wdn, gpost, S, tm=512):
    T = x2.shape[0]
    row = pl.BlockSpec((tm, D_MODEL), lambda i: (i, 0))
    halo = pl.BlockSpec((FFN_HALO, D_MODEL), lambda i: (jnp.maximum(i * (tm // FFN_HALO) - 1, 0), 0))
    return pl.pallas_call(
        functools.partial(_ffn_kernel, S=S),
        grid=(T // tm,),
        in_specs=[row, halo, _resident((1, D_MODEL)), _resident(wup.shape), _resident(cw.shape),
                  _resident(cb.shape), _resident(wdn.shape), _resident((1, D_MODEL))],
        out_specs=row,
        out_shape=jax.ShapeDtypeStruct((T, D_MODEL), F32),
        compiler_params=pltpu.CompilerParams(dimension_semantics=("parallel",), vmem_limit_bytes=VMEM_LIMIT),
        name="ffn",
    )(x2, x2, g, wup, cw, cb, wdn, gpost)


def _layer(x2, mem, pos2, B, S, pre_mix_g, w_in, pool_w, pool_scale, cmp_pe, cmp_w1, cmp_w2, mem_norm_g,
           w_mem_kv, w_br_pool, w_br_nsa, w_br_xa, w_out, post_mix_g, pre_ffn_g, w_up, conv_w, conv_b,
           w_down, post_ffn_g):
    o = np.cumsum((0,) + IN_SIZES)
    w_pool, w_q, w_kv, w_gn, w_qx, w_gbr = (w_in[:, o[k]:o[k + 1]] for k in range(6))
    w_gn = jnp.pad(w_gn, ((0, 0), (0, LANES - w_gn.shape[1])))
    vec = lambda a: a.reshape(1, -1)
    q, kc, vc, ks, vst, kw, vwt, gn = _in_proj(x2, pos2, vec(pre_mix_g), _weights3(w_q), _weights3(w_kv[:, :128]),
                                               w_kv[:, 128:].astype(BF16), w_gn.astype(BF16))
    b3 = lambda a: a.reshape(B, S, a.shape[-1])
    kcc = _compress(b3(kc), cmp_pe[0], cmp_w1[0], cmp_w2[0])
    vcc = _compress(b3(vc), cmp_pe[1], cmp_w1[1], cmp_w2[1])
    ynsa = _nsa(b3(q), kcc, vcc, b3(ks), vst, b3(kw), vwt, b3(gn))
    kvm = _mem_kv(mem, vec(mem_norm_g), w_mem_kv.astype(BF16))
    x2 = _merge(x2, ynsa.reshape(B * S, NSA_WIDTH), kvm, vec(pre_mix_g), w_pool.astype(BF16), w_qx.astype(BF16),
                w_gbr.astype(BF16), pool_w.astype(BF16), vec(pool_scale), w_br_pool.astype(BF16),
                w_br_nsa.astype(BF16), w_br_xa.astype(BF16), w_out.astype(BF16), vec(post_mix_g), S)
    return _ffn(x2, vec(pre_ffn_g), w_up.astype(BF16), conv_w, vec(conv_b), w_down.astype(BF16),
                vec(post_ffn_g), S)


def kernel(x, mem, positions, pre_mix_g, w_in, pool_w, pool_scale, cmp_pe, cmp_w1, cmp_w2, mem_norm_g, w_mem_kv,
           w_br_pool, w_br_nsa, w_br_xa, w_out, post_mix_g, pre_ffn_g, w_up, conv_w, conv_b, w_down, post_ffn_g):
    B, S, D = x.shape
    x2 = x.reshape(B * S, D)
    pos2 = positions.reshape(B * S, 1)
    for l in range(pre_mix_g.shape[0]):
        x2 = _layer(x2, mem, pos2, B, S, pre_mix_g[l], w_in[l], pool_w[l], pool_scale[l], cmp_pe[l], cmp_w1[l],
                    cmp_w2[l], mem_norm_g[l], w_mem_kv[l], w_br_pool[l], w_br_nsa[l], w_br_xa[l], w_out[l],
                    post_mix_g[l], pre_ffn_g[l], w_up[l], conv_w[l], conv_b[l], w_down[l], post_ffn_g[l])
    return x2.reshape(B, S, D)
```

```python
import functools

import numpy as np
import jax
import jax.numpy as jnp
from jax import lax
from jax.experimental import pallas as pl
from jax.experimental.pallas import tpu as pltpu

F32 = jnp.float32
BF16 = jnp.bfloat16
HIGHEST = lax.Precision.HIGHEST

D_MODEL = 1024
EPS = 1e-6
POOL_WINDOWS = (2, 4, 8, 16)
POOL_GROUP = 128
POOL_WIDTH = POOL_GROUP * len(POOL_WINDOWS)
NSA_HEADS = 16
NSA_KV_GROUPS = 2
HEADS_PER_GROUP = NSA_HEADS // NSA_KV_GROUPS
HEAD_DIM = 64
NSA_WIDTH = NSA_HEADS * HEAD_DIM
NSA_KV_WIDTH = NSA_KV_GROUPS * HEAD_DIM
CMP_BLOCK = 32
CMP_STRIDE = 16
CMP_HIDDEN = 256
SEL_BLOCK = 64
SEL_TOPK = 16
WINDOW = 512
ROPE_THETA = 500000.0
ROT_DIM = HEAD_DIM // 4
XA_HEADS = 4
XA_HEAD_DIM = 128
XA_WIDTH = XA_HEADS * XA_HEAD_DIM
N_BRANCHES = 3
D_FF = 2816
CONV_WIDTH = 3
IN_SIZES = (POOL_WIDTH, NSA_WIDTH, 6 * NSA_KV_WIDTH, 3 * NSA_HEADS, XA_WIDTH, N_BRANCHES * D_MODEL)

LANES = 128
Q_TILE = 128
SEL_KEY_TILE = 512
COL_CHUNK = 2048
MASK_NEG = -1e9
SCORE_FLOOR = -1e30
LAZY_MAX_HEADROOM = 64.0
VMEM_LIMIT = 56 * 1024 * 1024
Q_SCALE = HEAD_DIM ** -0.5 * 1.4426950408889634


def _rms(x, g):
    return x * lax.rsqrt(jnp.mean(x * x, axis=-1, keepdims=True) + EPS) * g


def _gelu_tanh(x):
    return 0.5 * x * (1.0 + jnp.tanh(0.7978845608028654 * (x + 0.044715 * (x * x * x))))


def _split_bf16(x):
    hi = x.astype(BF16)
    return hi, (x - hi.astype(F32)).astype(BF16)


def _weights3(w):
    hi, lo = _split_bf16(w)
    return jnp.concatenate([hi, hi, lo], axis=0)


def _dot3(x, w3):
    hi, lo = _split_bf16(x)
    return jnp.dot(jnp.concatenate([hi, lo, hi], axis=1), w3, preferred_element_type=F32)


def _dot_nt(a, b, precision=None):
    return lax.dot_general(a, b, (((1,), (1,)), ((), ())), precision=precision,
                           preferred_element_type=F32)


def _resident(shape):
    return pl.BlockSpec(shape, lambda *_: (0,) * len(shape), pipeline_mode=pl.Buffered(1))


def _inproj_kernel(x_ref, pos_ref, g_ref, wq_ref, wkc_ref, wkv_ref, wgn_ref, frq_ref, sg1_ref, sg2_ref,
                   q_ref, kc_ref, vc_ref, ks_ref, vs_ref, kw_ref, vw_ref, gn_ref):
    h = _rms(x_ref[...], g_ref[...])
    hb = h.astype(BF16)
    ang = pos_ref[...].astype(F32) * frq_ref[...]
    c = jnp.cos(ang)
    s = jnp.sin(ang)
    s1 = s * sg1_ref[...]
    s2 = s * sg2_ref[...]

    def rope(v):
        return v * c + pltpu.roll(v, LANES - ROT_DIM // 2, 1) * s1 + pltpu.roll(v, ROT_DIM // 2, 1) * s2

    h_hi, h_lo = _split_bf16(h)
    h3 = jnp.concatenate([h_hi, h_lo, h_hi], axis=1)
    q = jnp.dot(h3, wq_ref[...], preferred_element_type=F32)
    for j in range(NSA_WIDTH // LANES):
        q_ref[:, j * LANES:(j + 1) * LANES] = rope(q[:, j * LANES:(j + 1) * LANES]) * Q_SCALE
    kc_ref[...] = rope(jnp.dot(h3, wkc_ref[...], preferred_element_type=F32))
    kv = jnp.dot(hb, wkv_ref[...], preferred_element_type=F32)
    vc_ref[...] = kv[:, 0:128]
    ks_ref[...] = rope(kv[:, 128:256]).astype(BF16)
    vs_ref[...] = kv[:, 256:384].T.astype(BF16)
    kw_ref[...] = rope(kv[:, 384:512]).astype(BF16)
    vw_ref[...] = kv[:, 512:640].T.astype(BF16)
    gn_ref[...] = jax.nn.sigmoid(jnp.dot(hb, wgn_ref[...], preferred_element_type=F32))


def _in_proj(x2, pos2, g, wq, wkc, wkv, wgn, tm=512):
    T = x2.shape[0]
    half = ROT_DIM // 2
    d = np.arange(LANES) % HEAD_DIM
    inv_freq = ROPE_THETA ** (-jnp.arange(half, dtype=F32) * (2.0 / ROT_DIM))
    frq = jnp.where(d < ROT_DIM, inv_freq[d % half], 0.0)[None, :]
    sg1 = np.where(d < half, -1.0, 0.0).astype(np.float32)[None, :]
    sg2 = np.where((d >= half) & (d < ROT_DIM), 1.0, 0.0).astype(np.float32)[None, :]
    row = lambda w: pl.BlockSpec((tm, w), lambda i: (i, 0))
    col = pl.BlockSpec((128, tm), lambda i: (0, i))
    out_shape = [jax.ShapeDtypeStruct((T, NSA_WIDTH), F32),
                 jax.ShapeDtypeStruct((T, 128), F32), jax.ShapeDtypeStruct((T, 128), F32),
                 jax.ShapeDtypeStruct((T, 128), BF16), jax.ShapeDtypeStruct((128, T), BF16),
                 jax.ShapeDtypeStruct((T, 128), BF16), jax.ShapeDtypeStruct((128, T), BF16),
                 jax.ShapeDtypeStruct((T, 128), F32)]
    return pl.pallas_call(
        _inproj_kernel,
        grid=(T // tm,),
        in_specs=[row(D_MODEL), row(1), _resident((1, D_MODEL)), _resident(wq.shape), _resident(wkc.shape),
                  _resident(wkv.shape), _resident(wgn.shape), _resident((1, LANES)), _resident((1, LANES)),
                  _resident((1, LANES))],
        out_specs=[row(NSA_WIDTH), row(128), row(128), row(128), col, row(128), col, row(128)],
        out_shape=out_shape,
        compiler_params=pltpu.CompilerParams(dimension_semantics=("parallel",), vmem_limit_bytes=VMEM_LIMIT),
        name="in_proj",
    )(x2, pos2, g, wq, wkc, wkv, wgn, frq, jnp.asarray(sg1), jnp.asarray(sg2))


def _compress_kernel(c_ref, pet_ref, peb_ref, wt_ref, wb_ref, w2_ref, o_ref):
    c = c_ref[0]
    a = _dot3(c + pet_ref[...], wt_ref[...])
    b = _dot3(c + peb_ref[...], wb_ref[...])
    n = c.shape[0]
    hid = a + pltpu.roll(b, n - 1, 0)
    o_ref[0] = _dot3(_gelu_tanh(hid), w2_ref[...])


def _compress(kv, pe, w1, w2):
    B, S, _ = kv.shape
    n = S // CMP_STRIDE
    G = NSA_KV_GROUPS
    half = CMP_BLOCK // 2
    eye = jnp.eye(G, dtype=F32)

    def blockdiag_w1(w):
        w = w.reshape(half, 1, HEAD_DIM, 1, CMP_HIDDEN) * eye.reshape(1, G, 1, G, 1)
        return w.reshape(half * G * HEAD_DIM, G * CMP_HIDDEN)

    wt = _weights3(blockdiag_w1(w1[:half * HEAD_DIM]))
    wb = _weights3(blockdiag_w1(w1[half * HEAD_DIM:]))
    w2b = _weights3((w2.reshape(1, CMP_HIDDEN, 1, HEAD_DIM) * eye.reshape(G, 1, G, 1))
                    .reshape(G * CMP_HIDDEN, G * HEAD_DIM))
    pet = jnp.broadcast_to(pe[:half, None, :], (half, G, HEAD_DIM)).reshape(1, half * G * HEAD_DIM)
    peb = jnp.broadcast_to(pe[half:, None, :], (half, G, HEAD_DIM)).reshape(1, half * G * HEAD_DIM)
    c = kv.reshape(B, n, CMP_STRIDE * 128)
    return pl.pallas_call(
        _compress_kernel,
        grid=(B,),
        in_specs=[pl.BlockSpec((1, n, CMP_STRIDE * 128), lambda b: (b, 0, 0)),
                  _resident(pet.shape), _resident(peb.shape), _resident(wt.shape), _resident(wb.shape),
                  _resident(w2b.shape)],
        out_specs=pl.BlockSpec((1, n, 128), lambda b: (b, 0, 0)),
        out_shape=jax.ShapeDtypeStruct((B, n, 128), F32),
        compiler_params=pltpu.CompilerParams(dimension_semantics=("parallel",), vmem_limit_bytes=VMEM_LIMIT),
        name="compress",
    )(c, pet, peb, wt, wb, w2b)


def _memkv_kernel(m_ref, g_ref, w_ref, o_ref):
    mn = _rms(m_ref[0], g_ref[...])
    o_ref[0] = jnp.dot(mn.astype(BF16), w_ref[...], preferred_element_type=F32).astype(BF16)


def _mem_kv(mem, g, w):
    B, M, _ = mem.shape
    return pl.pallas_call(
        _memkv_kernel,
        grid=(B,),
        in_specs=[pl.BlockSpec((1, M, D_MODEL), lambda b: (b, 0, 0)), _resident((1, D_MODEL)),
                  _resident(w.shape)],
        out_specs=pl.BlockSpec((1, M, 2 * XA_WIDTH), lambda b: (b, 0, 0)),
        out_shape=jax.ShapeDtypeStruct((B, M, 2 * XA_WIDTH), BF16),
        compiler_params=pltpu.CompilerParams(dimension_semantics=("parallel",), vmem_limit_bytes=VMEM_LIMIT),
        name="mem_kv",
    )(mem, g, w)


def _nsa_kernel(q_ref, kcc_ref, vcc_ref, ks_ref, vst_ref, kw_ref, vwt_ref, gn_ref, ovt_ref, blk_ref, o_ref,
                qst_ref, qat_ref, oct_ref, ost_ref, owt_ref, m_ref, l_ref):
    qt = pl.program_id(1)
    q0 = qt * Q_TILE
    n_cmp = kcc_ref.shape[1]
    heads_per_chunk = COL_CHUNK // Q_TILE
    n_chunks = NSA_HEADS * Q_TILE // COL_CHUNK
    chunk_cols = [slice(c * COL_CHUNK, (c + 1) * COL_CHUNK) for c in range(n_chunks)]
    head_cols = [slice(h * Q_TILE, (h + 1) * Q_TILE) for h in range(NSA_HEADS)]
    group_cols = [slice(g * HEADS_PER_GROUP * Q_TILE, (g + 1) * HEADS_PER_GROUP * Q_TILE)
                  for g in range(NSA_KV_GROUPS)]
    per_head = lambda a, n=heads_per_chunk: jnp.concatenate([a] * n, axis=1)

    zeros = jnp.zeros((HEAD_DIM, Q_TILE), F32)
    for j in range(NSA_HEADS // 2):
        t = q_ref[0, :, j * LANES:(j + 1) * LANES].T
        for k in range(2):
            h = 2 * j + k
            part = t[k * HEAD_DIM:(k + 1) * HEAD_DIM, :]
            pair = [part, zeros] if h // HEADS_PER_GROUP == 0 else [zeros, part]
            qst_ref[:, head_cols[h]] = jnp.concatenate(pair, axis=0)
    qat_ref[0:LANES, :] = qst_ref[...].astype(BF16)

    crow = lax.broadcasted_iota(jnp.int32, (n_cmp, Q_TILE), 0)
    cq = lax.broadcasted_iota(jnp.int32, (n_cmp, Q_TILE), 1) + q0
    valid_c = crow * CMP_STRIDE + (CMP_BLOCK - 1) <= cq
    bias_c = per_head(jnp.where(valid_c, 0.0, SCORE_FLOOR), HEADS_PER_GROUP)
    keep_c = per_head(jnp.where(valid_c, 1.0, 0.0), HEADS_PER_GROUP)
    k_hi, k_lo = _split_bf16(kcc_ref[0])
    kcat = jnp.concatenate([k_hi, k_hi, k_lo], axis=1)
    vcct = vcc_ref[0].T.astype(BF16)
    brow = lax.broadcasted_iota(jnp.int32, (LANES, Q_TILE), 0)
    cur = (lax.broadcasted_iota(jnp.int32, (LANES, Q_TILE), 1) + q0) // SEL_BLOCK
    forced = (brow == 0) | (brow == cur) | (brow == cur - 1)
    brow_f = brow.astype(F32)
    for g in range(NSA_KV_GROUPS):
        cols = group_cols[g]
        q_hi, q_lo = _split_bf16(qst_ref[:, cols])
        s = jnp.dot(kcat, jnp.concatenate([q_hi, q_lo, q_hi], axis=0), preferred_element_type=F32) + bias_c
        p = jnp.exp2(s - jnp.max(s, axis=0, keepdims=True)) * keep_c
        p = p / jnp.maximum(jnp.sum(p, axis=0, keepdims=True), jnp.finfo(F32).tiny)
        pc_sum = p[:, 0:Q_TILE]
        for k in range(1, HEADS_PER_GROUP):
            pc_sum = pc_sum + p[:, k * Q_TILE:(k + 1) * Q_TILE]
        oct_ref[:, cols] = jnp.dot(vcct, p.astype(BF16), preferred_element_type=F32)
        imp = jnp.dot(ovt_ref[...], pc_sum, precision=HIGHEST, preferred_element_type=F32)
        v = jnp.where(brow > cur, -1.0, jnp.where(forced, jnp.inf, imp))
        sel = jnp.zeros((LANES, Q_TILE), F32)
        for _ in range(SEL_TOPK):
            mx = jnp.max(v, axis=0, keepdims=True)
            first = jnp.min(jnp.where(v == mx, brow_f, float(LANES)), axis=0, keepdims=True)
            pick = brow_f == first
            sel = jnp.where(pick, 1.0, sel)
            v = jnp.where(pick, -2.0, v)
        neg = jnp.where((sel > 0.0) & (brow < 2 * qt), 0.0, MASK_NEG).astype(BF16)
        for h in range(HEADS_PER_GROUP):
            qat_ref[LANES:2 * LANES, head_cols[g * HEADS_PER_GROUP + h]] = neg

    l_ref[...] = jnp.zeros(l_ref.shape, F32)
    ost_ref[...] = jnp.zeros(ost_ref.shape, F32)

    def scores(kt):
        k0 = pl.multiple_of(kt * SEL_KEY_TILE, SEL_KEY_TILE)
        ka = jnp.concatenate([ks_ref[0, pl.ds(k0, SEL_KEY_TILE), :], blk_ref[pl.ds(k0, SEL_KEY_TILE), :]],
                             axis=1)
        return jnp.dot(ka, qat_ref[...], preferred_element_type=F32)

    def values(start, n):
        return jnp.concatenate([vst_ref[:, pl.ds(start, n)], jnp.ones((16, n), BF16)], axis=0)

    def online_update(s, vt):
        m_old = m_ref[...]
        m_new = jnp.maximum(m_old, jnp.max(s, axis=0, keepdims=True))
        alpha = jnp.exp2(m_old - m_new)
        pv = jnp.dot(vt, jnp.exp2(s - m_new).astype(BF16), preferred_element_type=F32)
        ost_ref[...] = alpha * ost_ref[...] + pv[0:LANES]
        l_ref[...] = alpha * l_ref[...] + pv[LANES:LANES + 1]
        m_ref[...] = m_new

    qd = pl.multiple_of(q0, Q_TILE)
    drow = lax.broadcasted_iota(jnp.int32, (Q_TILE, Q_TILE), 0)
    dcol = lax.broadcasted_iota(jnp.int32, (Q_TILE, Q_TILE), 1)
    bias_d = per_head(jnp.where(drow <= dcol, 0.0, SCORE_FLOOR), NSA_HEADS)
    m_ref[...] = jnp.full(m_ref.shape, SCORE_FLOOR, F32)
    online_update(jnp.dot(ks_ref[0, pl.ds(qd, Q_TILE), :], qat_ref[0:LANES, :], preferred_element_type=F32) + bias_d,
                  values(qd, Q_TILE))

    def key_tile(kt, carry):
        k0 = pl.multiple_of(kt * SEL_KEY_TILE, SEL_KEY_TILE)
        s = scores(kt)
        m_old = m_ref[...]
        pv = jnp.dot(values(k0, SEL_KEY_TILE), jnp.exp2(s - m_old).astype(BF16), preferred_element_type=F32)
        t_max = jnp.max(s, axis=0, keepdims=True)
        safe = jnp.max(t_max - m_old) <= LAZY_MAX_HEADROOM

        @pl.when(safe)
        def _():
            m_new = jnp.maximum(m_old, t_max)
            alpha = jnp.exp2(m_old - m_new)
            ost_ref[...] = alpha * (ost_ref[...] + pv[0:LANES])
            l_ref[...] = alpha * (l_ref[...] + pv[LANES:LANES + 1])
            m_ref[...] = m_new

        @pl.when(jnp.logical_not(safe))
        def _():
            online_update(scores(kt), values(k0, SEL_KEY_TILE))

        return carry

    lax.fori_loop(0, (q0 + SEL_KEY_TILE - 1) // SEL_KEY_TILE, key_tile, 0)

    wlen = WINDOW + Q_TILE
    w0 = pl.multiple_of(jnp.maximum(q0 - WINDOW, 0), Q_TILE)
    kwn = kw_ref[0, pl.ds(w0, wlen), :]
    vwnt = vwt_ref[:, pl.ds(w0, wlen)]
    kidx = w0 + lax.broadcasted_iota(jnp.int32, (wlen, Q_TILE), 0)
    tqw = q0 + lax.broadcasted_iota(jnp.int32, (wlen, Q_TILE), 1)
    bias_w = per_head(jnp.where((kidx <= tqw) & (tqw - kidx < WINDOW), 0.0, SCORE_FLOOR))
    for cols in chunk_cols:
        s = jnp.dot(kwn, qat_ref[0:LANES, cols], preferred_element_type=F32) + bias_w
        p = jnp.exp2(s - jnp.max(s, axis=0, keepdims=True))
        o = jnp.dot(vwnt, p.astype(BF16), preferred_element_type=F32)
        owt_ref[:, cols] = o / jnp.sum(p, axis=0, keepdims=True)

    gt = gn_ref[0].T
    for j in range(NSA_HEADS // 2):
        parts = []
        for h in (2 * j, 2 * j + 1):
            g = h // HEADS_PER_GROUP
            rows = slice(g * HEAD_DIM, (g + 1) * HEAD_DIM)
            cols = head_cols[h]
            o_sel = ost_ref[rows, cols] / l_ref[:, cols]
            parts.append(gt[3 * h:3 * h + 1, :] * oct_ref[rows, cols] + gt[3 * h + 1:3 * h + 2, :] * o_sel
                         + gt[3 * h + 2:3 * h + 3, :] * owt_ref[rows, cols])
        o_ref[0, :, j * LANES:(j + 1) * LANES] = jnp.concatenate(parts, axis=0).T.astype(o_ref.dtype)


def _nsa(q, kcc, vcc, ks, vst, kw, vwt, gn):
    B, S, _ = q.shape
    n_cmp = kcc.shape[1]
    n_sel = S // SEL_BLOCK
    assert n_sel <= LANES and S % (2 * SEL_KEY_TILE) == 0 and S >= WINDOW + Q_TILE
    ci = np.arange(n_cmp)[None, :] * CMP_STRIDE
    sj = np.arange(LANES)[:, None] * SEL_BLOCK
    overlap_t = ((ci < sj + SEL_BLOCK) & (ci + CMP_BLOCK > sj)).astype(np.float32)
    block_onehot = (np.arange(S)[:, None] // SEL_BLOCK == np.arange(LANES)[None, :])
    block_onehot = jnp.asarray(block_onehot.astype(np.float32), dtype=BF16)
    cols = NSA_HEADS * Q_TILE
    full = lambda n, w: pl.BlockSpec((1, n, w), lambda b, i: (b, 0, 0))
    full_t = pl.BlockSpec((128, S), lambda b, i: (0, b))
    tile = lambda w: pl.BlockSpec((1, Q_TILE, w), lambda b, i: (b, i, 0))
    return pl.pallas_call(
        _nsa_kernel,
        grid=(B, S // Q_TILE),
        in_specs=[tile(NSA_WIDTH), full(n_cmp, 128), full(n_cmp, 128), full(S, 128), full_t,
                  full(S, 128), full_t, tile(128), _resident(overlap_t.shape), _resident((S, LANES))],
        out_specs=tile(NSA_WIDTH),
        out_shape=jax.ShapeDtypeStruct((B, S, NSA_WIDTH), BF16),
        scratch_shapes=[pltpu.VMEM((LANES, cols), F32), pltpu.VMEM((2 * LANES, cols), BF16),
                        pltpu.VMEM((LANES, cols), F32), pltpu.VMEM((LANES, cols), F32),
                        pltpu.VMEM((LANES, cols), F32), pltpu.VMEM((1, cols), F32),
                        pltpu.VMEM((1, cols), F32)],
        compiler_params=pltpu.CompilerParams(dimension_semantics=("parallel", "parallel"),
                                             vmem_limit_bytes=VMEM_LIMIT),
        name="nsa",
    )(q, kcc, vcc, ks, vst, kw, vwt, gn, jnp.asarray(overlap_t), block_onehot)


POOL_HALO = 16


def _merge_kernel(x_ref, xh_ref, yn_ref, kvm_ref, g_ref, wpool_ref, wqx_ref, wgbr_ref, poolw_ref, pscale_ref,
                  wbp_ref, wbn_ref, wbx_ref, wout_ref, gpost_ref, o_ref, *, S):
    i = pl.program_id(0)
    tm = x_ref.shape[0]
    t0 = (i * tm) % S
    x = x_ref[...]
    g = g_ref[...]
    h = _rms(x, g)
    hb = h.astype(BF16)

    hh = _rms(xh_ref[...], g).astype(BF16)
    u_halo = jnp.dot(hh, wpool_ref[...], preferred_element_type=F32)
    u_halo = jnp.where(t0 > 0, u_halo, 0.0)
    u = jnp.dot(hb, wpool_ref[...], preferred_element_type=F32)
    ue = jnp.concatenate([u_halo, u], axis=0)
    trow = t0 + lax.broadcasted_iota(jnp.int32, (tm, 1), 0) + 1
    ypool = []
    for gi, w in enumerate(POOL_WINDOWS):
        acc = ue[:, gi * POOL_GROUP:(gi + 1) * POOL_GROUP]
        step = 1
        while step < w:
            acc = acc + pltpu.roll(acc, step, 0)
            step *= 2
        cnt = jnp.minimum(trow, w).astype(F32)
        p = acc[POOL_HALO:, :] / cnt - u[:, gi * POOL_GROUP:(gi + 1) * POOL_GROUP]
        ypool.append(jnp.dot(p.astype(BF16), poolw_ref[gi], preferred_element_type=F32))
    ypool = jnp.concatenate(ypool, axis=1) * pscale_ref[...]

    qx = jnp.dot(hb, wqx_ref[...], preferred_element_type=F32)
    ymem = []
    for hd in range(XA_HEADS):
        km = kvm_ref[0, :, hd * XA_HEAD_DIM:(hd + 1) * XA_HEAD_DIM]
        vm = kvm_ref[0, :, XA_WIDTH + hd * XA_HEAD_DIM:XA_WIDTH + (hd + 1) * XA_HEAD_DIM]
        s = _dot_nt(qx[:, hd * XA_HEAD_DIM:(hd + 1) * XA_HEAD_DIM].astype(BF16), km) * (XA_HEAD_DIM ** -0.5)
        e = jnp.exp(s - jnp.max(s, axis=-1, keepdims=True))
        p = e / jnp.sum(e, axis=-1, keepdims=True)
        ymem.append(jnp.dot(p.astype(BF16), vm, preferred_element_type=F32))
    ymem = jnp.concatenate(ymem, axis=1)

    gbr = jax.nn.sigmoid(jnp.dot(hb, wgbr_ref[...], preferred_element_type=F32))
    y = (gbr[:, 0:D_MODEL] * jnp.dot(ypool.astype(BF16), wbp_ref[...], preferred_element_type=F32)
         + gbr[:, D_MODEL:2 * D_MODEL] * jnp.dot(yn_ref[...], wbn_ref[...], preferred_element_type=F32)
         + gbr[:, 2 * D_MODEL:3 * D_MODEL] * jnp.dot(ymem.astype(BF16), wbx_ref[...], preferred_element_type=F32))
    o = jnp.dot(y.astype(BF16), wout_ref[...], preferred_element_type=F32)
    o_ref[...] = x + _rms(o, gpost_ref[...])


def _merge(x2, ynsa, kvm, g, wpool, wqx, wgbr, poolw, pscale, wbp, wbn, wbx, wout, gpost, S, tm=256):
    T = x2.shape[0]
    M = kvm.shape[1]
    row = lambda w: pl.BlockSpec((tm, w), lambda i: (i, 0))
    halo = pl.BlockSpec((POOL_HALO, D_MODEL), lambda i: (jnp.maximum(i * (tm // POOL_HALO) - 1, 0), 0))
    return pl.pallas_call(
        functools.partial(_merge_kernel, S=S),
        grid=(T // tm,),
        in_specs=[row(D_MODEL), halo, row(NSA_WIDTH),
                  pl.BlockSpec((1, M, 2 * XA_WIDTH), lambda i: ((i * tm) // S, 0, 0)),
                  _resident((1, D_MODEL)), _resident(wpool.shape), _resident(wqx.shape), _resident(wgbr.shape),
                  _resident(poolw.shape), _resident(pscale.shape), _resident(wbp.shape), _resident(wbn.shape),
                  _resident(wbx.shape), _resident(wout.shape), _resident((1, D_MODEL))],
        out_specs=row(D_MODEL),
        out_shape=jax.ShapeDtypeStruct((T, D_MODEL), F32),
        compiler_params=pltpu.CompilerParams(dimension_semantics=("parallel",), vmem_limit_bytes=VMEM_LIMIT),
        name="merge",
    )(x2, x2, ynsa, kvm, g, wpool, wqx, wgbr, poolw, pscale, wbp, wbn, wbx, wout, gpost)


FFN_HALO = 8
FFN_CHUNK = 1408


def _ffn_kernel(x_ref, xh_ref, g_ref, wup_ref, cw_ref, cb_ref, wdn_ref, gpost_ref, o_ref, *, S):
    i = pl.program_id(0)
    tm = x_ref.shape[0]
    t0 = (i * tm) % S
    x = x_ref[...]
    g = g_ref[...]
    hh = jnp.where(t0 > 0, _rms(xh_ref[...], g), 0.0)
    he = jnp.concatenate([hh, _rms(x, g)], axis=0).astype(BF16)

    def conv(col0):
        u = jnp.dot(he, wup_ref[:, col0:col0 + FFN_CHUNK], preferred_element_type=F32)
        c = cb_ref[:, col0:col0 + FFN_CHUNK] + cw_ref[CONV_WIDTH - 1:CONV_WIDTH, col0:col0 + FFN_CHUNK] * u
        for k in range(1, CONV_WIDTH):
            tap = cw_ref[CONV_WIDTH - 1 - k:CONV_WIDTH - k, col0:col0 + FFN_CHUNK]
            c = c + tap * pltpu.roll(u, k, 0)
        return c[FFN_HALO:, :]

    f = jnp.zeros((tm, D_MODEL), F32)
    for j in range(D_FF // FFN_CHUNK):
        act = _gelu_tanh(conv(j * FFN_CHUNK)) * conv(D_FF + j * FFN_CHUNK)
        f = f + jnp.dot(act.astype(BF16), wdn_ref[j * FFN_CHUNK:(j + 1) * FFN_CHUNK, :],
                        preferred_element_type=F32)
    o_ref[...] = x + _rms(f, gpost_ref[...])


def _ffn(x2, g, wup, cw, cb, wdn, gpost, S, tm=512):
    T = x2.shape[0]
    row = pl.BlockSpec((tm, D_MODEL), lambda i: (i, 0))
    halo = pl.BlockSpec((FFN_HALO, D_MODEL), lambda i: (jnp.maximum(i * (tm // FFN_HALO) - 1, 0), 0))
    return pl.pallas_call(
        functools.partial(_ffn_kernel, S=S),
        grid=(T // tm,),
        in_specs=[row, halo, _resident((1, D_MODEL)), _resident(wup.shape), _resident(cw.shape),
                  _resident(cb.shape), _resident(wdn.shape), _resident((1, D_MODEL))],
        out_specs=row,
        out_shape=jax.ShapeDtypeStruct((T, D_MODEL), F32),
        compiler_params=pltpu.CompilerParams(dimension_semantics=("parallel",), vmem_limit_bytes=VMEM_LIMIT),
        name="ffn",
    )(x2, x2, g, wup, cw, cb, wdn, gpost)


def _layer(x2, mem, pos2, B, S, pre_mix_g, w_in, pool_w, pool_scale, cmp_pe, cmp_w1, cmp_w2, mem_norm_g,
           w_mem_kv, w_br_pool, w_br_nsa, w_br_xa, w_out, post_mix_g, pre_ffn_g, w_up, conv_w, conv_b,
           w_down, post_ffn_g):
    o = np.cumsum((0,) + IN_SIZES)
    w_pool, w_q, w_kv, w_gn, w_qx, w_gbr = (w_in[:, o[k]:o[k + 1]] for k in range(6))
    w_gn = jnp.pad(w_gn, ((0, 0), (0, LANES - w_gn.shape[1])))
    vec = lambda a: a.reshape(1, -1)
    q, kc, vc, ks, vst, kw, vwt, gn = _in_proj(x2, pos2, vec(pre_mix_g), _weights3(w_q), _weights3(w_kv[:, :128]),
                                               w_kv[:, 128:].astype(BF16), w_gn.astype(BF16))
    b3 = lambda a: a.reshape(B, S, a.shape[-1])
    kcc = _compress(b3(kc), cmp_pe[0], cmp_w1[0], cmp_w2[0])
    vcc = _compress(b3(vc), cmp_pe[1], cmp_w1[1], cmp_w2[1])
    ynsa = _nsa(b3(q), kcc, vcc, b3(ks), vst, b3(kw), vwt, b3(gn))
    kvm = _mem_kv(mem, vec(mem_norm_g), w_mem_kv.astype(BF16))
    x2 = _merge(x2, ynsa.reshape(B * S, NSA_WIDTH), kvm, vec(pre_mix_g), w_pool.astype(BF16), w_qx.astype(BF16),
                w_gbr.astype(BF16), pool_w.astype(BF16), vec(pool_scale), w_br_pool.astype(BF16),
                w_br_nsa.astype(BF16), w_br_xa.astype(BF16), w_out.astype(BF16), vec(post_mix_g), S)
    return _ffn(x2, vec(pre_ffn_g), w_up.astype(BF16), conv_w, vec(conv_b), w_down.astype(BF16),
                vec(post_ffn_g), S)


def kernel(x, mem, positions, pre_mix_g, w_in, pool_w, pool_scale, cmp_pe, cmp_w1, cmp_w2, mem_norm_g, w_mem_kv,
           w_br_pool, w_br_nsa, w_br_xa, w_out, post_mix_g, pre_ffn_g, w_up, conv_w, conv_b, w_down, post_ffn_g):
    B, S, D = x.shape
    x2 = x.reshape(B * S, D)
    pos2 = positions.reshape(B * S, 1)
    for l in range(pre_mix_g.shape[0]):
        x2 = _layer(x2, mem, pos2, B, S, pre_mix_g[l], w_in[l], pool_w[l], pool_scale[l], cmp_pe[l], cmp_w1[l],
                    cmp_w2[l], mem_norm_g[l], w_mem_kv[l], w_br_pool[l], w_br_nsa[l], w_br_xa[l], w_out[l],
                    post_mix_g[l], pre_ffn_g[l], w_up[l], conv_w[l], conv_b[l], w_down[l], post_ffn_g[l])
    return x2.reshape(B, S, D)
```

```python
import functools

import numpy as np
import jax
import jax.numpy as jnp
from jax import lax
from jax.experimental import pallas as pl
from jax.experimental.pallas import tpu as pltpu

F32 = jnp.float32
BF16 = jnp.bfloat16

D_MODEL = 1024
EPS = 1e-6
POOL_WINDOWS = (2, 4, 8, 16)
POOL_GROUP = 128
POOL_WIDTH = POOL_GROUP * len(POOL_WINDOWS)
NSA_HEADS = 16
NSA_KV_GROUPS = 2
HEADS_PER_GROUP = NSA_HEADS // NSA_KV_GROUPS
HEAD_DIM = 64
NSA_WIDTH = NSA_HEADS * HEAD_DIM
NSA_KV_WIDTH = NSA_KV_GROUPS * HEAD_DIM
CMP_BLOCK = 32
CMP_STRIDE = 16
CMP_HIDDEN = 256
SEL_BLOCK = 64
SEL_TOPK = 16
WINDOW = 512
ROPE_THETA = 500000.0
ROT_DIM = HEAD_DIM // 4
XA_HEADS = 4
XA_HEAD_DIM = 128
XA_WIDTH = XA_HEADS * XA_HEAD_DIM
N_BRANCHES = 3
D_FF = 2816
CONV_WIDTH = 3
IN_SIZES = (POOL_WIDTH, NSA_WIDTH, 6 * NSA_KV_WIDTH, 3 * NSA_HEADS, XA_WIDTH, N_BRANCHES * D_MODEL)

LANES = 128
Q_TILE = 128
SEL_KEY_TILE = 512
COL_CHUNK = 2048
MASK_NEG = -1e9
SCORE_FLOOR = -1e30
LAZY_MAX_HEADROOM = 64.0
VMEM_LIMIT = 56 * 1024 * 1024
Q_SCALE = HEAD_DIM ** -0.5 * 1.4426950408889634


def _rms(x, g):
    return x * lax.rsqrt(jnp.mean(x * x, axis=-1, keepdims=True) + EPS) * g


def _gelu_tanh(x):
    return 0.5 * x * (1.0 + jnp.tanh(0.7978845608028654 * (x + 0.044715 * (x * x * x))))


def _dot_bf16(x, w):
    return jnp.dot(x.astype(BF16), w, preferred_element_type=F32)


def _dot_nt(a, b, precision=None):
    return lax.dot_general(a, b, (((1,), (1,)), ((), ())), precision=precision,
                           preferred_element_type=F32)


def _resident(shape):
    return pl.BlockSpec(shape, lambda *_: (0,) * len(shape), pipeline_mode=pl.Buffered(1))


def _inproj_kernel(x_ref, pos_ref, g_ref, wq_ref, wkc_ref, wkv_ref, wgn_ref, frq_ref, sg1_ref, sg2_ref,
                   q_ref, kc_ref, vc_ref, ks_ref, vs_ref, kw_ref, vw_ref, gn_ref):
    h = _rms(x_ref[...], g_ref[...])
    hb = h.astype(BF16)
    ang = pos_ref[...].astype(F32) * frq_ref[...]
    c = jnp.cos(ang)
    s = jnp.sin(ang)
    s1 = s * sg1_ref[...]
    s2 = s * sg2_ref[...]

    def rope(v):
        return v * c + pltpu.roll(v, LANES - ROT_DIM // 2, 1) * s1 + pltpu.roll(v, ROT_DIM // 2, 1) * s2

    q = jnp.dot(hb, wq_ref[...], preferred_element_type=F32)
    for j in range(NSA_WIDTH // LANES):
        q_ref[:, j * LANES:(j + 1) * LANES] = rope(q[:, j * LANES:(j + 1) * LANES]) * Q_SCALE
    kc_ref[...] = rope(jnp.dot(hb, wkc_ref[...], preferred_element_type=F32))
    kv = jnp.dot(hb, wkv_ref[...], preferred_element_type=F32)
    vc_ref[...] = kv[:, 0:128]
    ks_ref[...] = rope(kv[:, 128:256]).astype(BF16)
    vs_ref[...] = kv[:, 256:384].T.astype(BF16)
    kw_ref[...] = rope(kv[:, 384:512]).astype(BF16)
    vw_ref[...] = kv[:, 512:640].T.astype(BF16)
    gn_ref[...] = jax.nn.sigmoid(jnp.dot(hb, wgn_ref[...], preferred_element_type=F32))


def _in_proj(x2, pos2, g, wq, wkc, wkv, wgn, tm=512):
    T = x2.shape[0]
    half = ROT_DIM // 2
    d = np.arange(LANES) % HEAD_DIM
    inv_freq = ROPE_THETA ** (-jnp.arange(half, dtype=F32) * (2.0 / ROT_DIM))
    frq = jnp.where(d < ROT_DIM, inv_freq[d % half], 0.0)[None, :]
    sg1 = np.where(d < half, -1.0, 0.0).astype(np.float32)[None, :]
    sg2 = np.where((d >= half) & (d < ROT_DIM), 1.0, 0.0).astype(np.float32)[None, :]
    row = lambda w: pl.BlockSpec((tm, w), lambda i: (i, 0))
    col = pl.BlockSpec((128, tm), lambda i: (0, i))
    out_shape = [jax.ShapeDtypeStruct((T, NSA_WIDTH), F32),
                 jax.ShapeDtypeStruct((T, 128), F32), jax.ShapeDtypeStruct((T, 128), F32),
                 jax.ShapeDtypeStruct((T, 128), BF16), jax.ShapeDtypeStruct((128, T), BF16),
                 jax.ShapeDtypeStruct((T, 128), BF16), jax.ShapeDtypeStruct((128, T), BF16),
                 jax.ShapeDtypeStruct((T, 128), F32)]
    return pl.pallas_call(
        _inproj_kernel,
        grid=(T // tm,),
        in_specs=[row(D_MODEL), row(1), _resident((1, D_MODEL)), _resident(wq.shape), _resident(wkc.shape),
                  _resident(wkv.shape), _resident(wgn.shape), _resident((1, LANES)), _resident((1, LANES)),
                  _resident((1, LANES))],
        out_specs=[row(NSA_WIDTH), row(128), row(128), row(128), col, row(128), col, row(128)],
        out_shape=out_shape,
        compiler_params=pltpu.CompilerParams(dimension_semantics=("parallel",), vmem_limit_bytes=VMEM_LIMIT),
        name="in_proj",
    )(x2, pos2, g, wq, wkc, wkv, wgn, frq, jnp.asarray(sg1), jnp.asarray(sg2))


def _compress_kernel(c_ref, pet_ref, peb_ref, wt_ref, wb_ref, w2_ref, o_ref):
    c = c_ref[0]
    a = _dot_bf16(c + pet_ref[...], wt_ref[...])
    b = _dot_bf16(c + peb_ref[...], wb_ref[...])
    n = c.shape[0]
    hid = a + pltpu.roll(b, n - 1, 0)
    o_ref[0] = _dot_bf16(_gelu_tanh(hid), w2_ref[...])


def _compress(kv, pe, w1, w2):
    B, S, _ = kv.shape
    n = S // CMP_STRIDE
    G = NSA_KV_GROUPS
    half = CMP_BLOCK // 2
    eye = jnp.eye(G, dtype=F32)

    def blockdiag_w1(w):
        w = w.reshape(half, 1, HEAD_DIM, 1, CMP_HIDDEN) * eye.reshape(1, G, 1, G, 1)
        return w.reshape(half * G * HEAD_DIM, G * CMP_HIDDEN)

    wt = blockdiag_w1(w1[:half * HEAD_DIM]).astype(BF16)
    wb = blockdiag_w1(w1[half * HEAD_DIM:]).astype(BF16)
    w2b = ((w2.reshape(1, CMP_HIDDEN, 1, HEAD_DIM) * eye.reshape(G, 1, G, 1))
           .reshape(G * CMP_HIDDEN, G * HEAD_DIM).astype(BF16))
    pet = jnp.broadcast_to(pe[:half, None, :], (half, G, HEAD_DIM)).reshape(1, half * G * HEAD_DIM)
    peb = jnp.broadcast_to(pe[half:, None, :], (half, G, HEAD_DIM)).reshape(1, half * G * HEAD_DIM)
    c = kv.reshape(B, n, CMP_STRIDE * 128)
    return pl.pallas_call(
        _compress_kernel,
        grid=(B,),
        in_specs=[pl.BlockSpec((1, n, CMP_STRIDE * 128), lambda b: (b, 0, 0)),
                  _resident(pet.shape), _resident(peb.shape), _resident(wt.shape), _resident(wb.shape),
                  _resident(w2b.shape)],
        out_specs=pl.BlockSpec((1, n, 128), lambda b: (b, 0, 0)),
        out_shape=jax.ShapeDtypeStruct((B, n, 128), F32),
        compiler_params=pltpu.CompilerParams(dimension_semantics=("parallel",), vmem_limit_bytes=VMEM_LIMIT),
        name="compress",
    )(c, pet, peb, wt, wb, w2b)


def _memkv_kernel(m_ref, g_ref, w_ref, o_ref):
    mn = _rms(m_ref[0], g_ref[...])
    o_ref[0] = jnp.dot(mn.astype(BF16), w_ref[...], preferred_element_type=F32).astype(BF16)


def _mem_kv(mem, g, w):
    B, M, _ = mem.shape
    return pl.pallas_call(
        _memkv_kernel,
        grid=(B,),
        in_specs=[pl.BlockSpec((1, M, D_MODEL), lambda b: (b, 0, 0)), _resident((1, D_MODEL)),
                  _resident(w.shape)],
        out_specs=pl.BlockSpec((1, M, 2 * XA_WIDTH), lambda b: (b, 0, 0)),
        out_shape=jax.ShapeDtypeStruct((B, M, 2 * XA_WIDTH), BF16),
        compiler_params=pltpu.CompilerParams(dimension_semantics=("parallel",), vmem_limit_bytes=VMEM_LIMIT),
        name="mem_kv",
    )(mem, g, w)


def _nsa_kernel(q_ref, kcc_ref, vcc_ref, ks_ref, vst_ref, kw_ref, vwt_ref, gn_ref, ovt_ref, blk_ref, o_ref,
                qat_ref, oct_ref, ost_ref, owt_ref, m_ref, l_ref, ap_ref, pvp_ref):
    qt = pl.program_id(1)
    q0 = qt * Q_TILE
    n_cmp = kcc_ref.shape[1]
    heads_per_chunk = COL_CHUNK // Q_TILE
    n_chunks = NSA_HEADS * Q_TILE // COL_CHUNK
    chunk_cols = [slice(c * COL_CHUNK, (c + 1) * COL_CHUNK) for c in range(n_chunks)]
    head_cols = [slice(h * Q_TILE, (h + 1) * Q_TILE) for h in range(NSA_HEADS)]
    group_cols = [slice(g * HEADS_PER_GROUP * Q_TILE, (g + 1) * HEADS_PER_GROUP * Q_TILE)
                  for g in range(NSA_KV_GROUPS)]
    per_head = lambda a, n=heads_per_chunk: jnp.concatenate([a] * n, axis=1)

    zeros = jnp.zeros((HEAD_DIM, Q_TILE), F32)
    for j in range(NSA_HEADS // 2):
        t = q_ref[0, :, j * LANES:(j + 1) * LANES].T
        for k in range(2):
            h = 2 * j + k
            part = t[k * HEAD_DIM:(k + 1) * HEAD_DIM, :]
            pair = [part, zeros] if h // HEADS_PER_GROUP == 0 else [zeros, part]
            qat_ref[0:LANES, head_cols[h]] = jnp.concatenate(pair, axis=0).astype(BF16)

    crow = lax.broadcasted_iota(jnp.int32, (n_cmp, Q_TILE), 0)
    cq = lax.broadcasted_iota(jnp.int32, (n_cmp, Q_TILE), 1) + q0
    valid_c = crow * CMP_STRIDE + (CMP_BLOCK - 1) <= cq
    bias_c = per_head(jnp.where(valid_c, 0.0, SCORE_FLOOR), HEADS_PER_GROUP)
    keep_c = per_head(jnp.where(valid_c, 1.0, 0.0), HEADS_PER_GROUP)
    kcc = kcc_ref[0].astype(BF16)
    vcct = vcc_ref[0].T.astype(BF16)
    brow = lax.broadcasted_iota(jnp.int32, (LANES, Q_TILE), 0)
    cur = (lax.broadcasted_iota(jnp.int32, (LANES, Q_TILE), 1) + q0) // SEL_BLOCK
    forced = (brow == 0) | (brow == cur) | (brow == cur - 1)
    brow_f = brow.astype(F32)
    for g in range(NSA_KV_GROUPS):
        cols = group_cols[g]
        s = jnp.dot(kcc, qat_ref[0:LANES, cols], preferred_element_type=F32) + bias_c
        p = jnp.exp2(s - jnp.max(s, axis=0, keepdims=True)) * keep_c
        p = p / jnp.maximum(jnp.sum(p, axis=0, keepdims=True), jnp.finfo(F32).tiny)
        pc_sum = p[:, 0:Q_TILE]
        for k in range(1, HEADS_PER_GROUP):
            pc_sum = pc_sum + p[:, k * Q_TILE:(k + 1) * Q_TILE]
        oct_ref[:, cols] = jnp.dot(vcct, p.astype(BF16), preferred_element_type=F32)
        imp = jnp.dot(ovt_ref[...], pc_sum.astype(BF16), preferred_element_type=F32)
        v = jnp.where(brow > cur, -1.0, jnp.where(forced, jnp.inf, imp))
        sel = jnp.zeros((LANES, Q_TILE), F32)
        for _ in range(SEL_TOPK):
            mx = jnp.max(v, axis=0, keepdims=True)
            first = jnp.min(jnp.where(v == mx, brow_f, float(LANES)), axis=0, keepdims=True)
            pick = brow_f == first
            sel = jnp.where(pick, 1.0, sel)
            v = jnp.where(pick, -2.0, v)
        neg = jnp.where((sel > 0.0) & (brow < 2 * qt), 0.0, MASK_NEG).astype(BF16)
        for h in range(HEADS_PER_GROUP):
            qat_ref[LANES:2 * LANES, head_cols[g * HEADS_PER_GROUP + h]] = neg

    l_ref[...] = jnp.zeros(l_ref.shape, F32)
    ost_ref[...] = jnp.zeros(ost_ref.shape, F32)

    def scores(kt):
        k0 = pl.multiple_of(kt * SEL_KEY_TILE, SEL_KEY_TILE)
        ka = jnp.concatenate([ks_ref[0, pl.ds(k0, SEL_KEY_TILE), :], blk_ref[pl.ds(k0, SEL_KEY_TILE), :]],
                             axis=1)
        return jnp.dot(ka, qat_ref[...], preferred_element_type=F32)

    def values(start, n):
        return jnp.concatenate([vst_ref[:, pl.ds(start, n)], jnp.ones((16, n), BF16)], axis=0)

    def online_update(s, vt):
        m_old = m_ref[...]
        m_new = jnp.maximum(m_old, jnp.max(s, axis=0, keepdims=True))
        alpha = jnp.exp2(m_old - m_new)
        pv = jnp.dot(vt, jnp.exp2(s - m_new).astype(BF16), preferred_element_type=F32)
        ost_ref[...] = alpha * ost_ref[...] + pv[0:LANES]
        l_ref[...] = alpha * l_ref[...] + pv[LANES:LANES + 1]
        m_ref[...] = m_new

    qd = pl.multiple_of(q0, Q_TILE)
    drow = lax.broadcasted_iota(jnp.int32, (Q_TILE, Q_TILE), 0)
    dcol = lax.broadcasted_iota(jnp.int32, (Q_TILE, Q_TILE), 1)
    bias_d = per_head(jnp.where(drow <= dcol, 0.0, SCORE_FLOOR), NSA_HEADS)
    m_ref[...] = jnp.full(m_ref.shape, SCORE_FLOOR, F32)
    online_update(jnp.dot(ks_ref[0, pl.ds(qd, Q_TILE), :], qat_ref[0:LANES, :], preferred_element_type=F32) + bias_d,
                  values(qd, Q_TILE))

    pvp_ref[...] = jnp.zeros(pvp_ref.shape, F32)
    ap_ref[...] = jnp.ones(ap_ref.shape, F32)

    def fold_parked():
        alpha = ap_ref[...]
        ost_ref[...] = alpha * (ost_ref[...] + pvp_ref[0:LANES, :])
        l_ref[...] = alpha * (l_ref[...] + pvp_ref[LANES:LANES + 1, :])

    def key_tile(kt, carry):
        k0 = pl.multiple_of(kt * SEL_KEY_TILE, SEL_KEY_TILE)
        fold_parked()
        s = scores(kt)
        m_old = m_ref[...]
        pvp_ref[...] = jnp.dot(values(k0, SEL_KEY_TILE), jnp.exp2(s - m_old).astype(BF16),
                               preferred_element_type=F32)
        t_max = jnp.max(s, axis=0, keepdims=True)
        safe = jnp.max(t_max - m_old) <= LAZY_MAX_HEADROOM

        @pl.when(safe)
        def _():
            m_new = jnp.maximum(m_old, t_max)
            ap_ref[...] = jnp.exp2(m_old - m_new)
            m_ref[...] = m_new

        @pl.when(jnp.logical_not(safe))
        def _():
            pvp_ref[...] = jnp.zeros(pvp_ref.shape, F32)
            ap_ref[...] = jnp.ones(ap_ref.shape, F32)
            online_update(scores(kt), values(k0, SEL_KEY_TILE))

        return carry

    lax.fori_loop(0, (q0 + SEL_KEY_TILE - 1) // SEL_KEY_TILE, key_tile, 0)
    fold_parked()

    wlen = WINDOW + Q_TILE
    w0 = pl.multiple_of(jnp.maximum(q0 - WINDOW, 0), Q_TILE)
    kwn = kw_ref[0, pl.ds(w0, wlen), :]
    vwnt = vwt_ref[:, pl.ds(w0, wlen)]
    kidx = w0 + lax.broadcasted_iota(jnp.int32, (wlen, Q_TILE), 0)
    tqw = q0 + lax.broadcasted_iota(jnp.int32, (wlen, Q_TILE), 1)
    bias_w = per_head(jnp.where((kidx <= tqw) & (tqw - kidx < WINDOW), 0.0, SCORE_FLOOR))
    for cols in chunk_cols:
        s = jnp.dot(kwn, qat_ref[0:LANES, cols], preferred_element_type=F32) + bias_w
        p = jnp.exp2(s - jnp.max(s, axis=0, keepdims=True))
        o = jnp.dot(vwnt, p.astype(BF16), preferred_element_type=F32)
        owt_ref[:, cols] = o / jnp.sum(p, axis=0, keepdims=True)

    gt = gn_ref[0].T
    for j in range(NSA_HEADS // 2):
        parts = []
        for h in (2 * j, 2 * j + 1):
            g = h // HEADS_PER_GROUP
            rows = slice(g * HEAD_DIM, (g + 1) * HEAD_DIM)
            cols = head_cols[h]
            o_sel = ost_ref[rows, cols] / l_ref[:, cols]
            parts.append(gt[3 * h:3 * h + 1, :] * oct_ref[rows, cols] + gt[3 * h + 1:3 * h + 2, :] * o_sel
                         + gt[3 * h + 2:3 * h + 3, :] * owt_ref[rows, cols])
        o_ref[0, :, j * LANES:(j + 1) * LANES] = jnp.concatenate(parts, axis=0).T.astype(o_ref.dtype)


def _nsa(q, kcc, vcc, ks, vst, kw, vwt, gn):
    B, S, _ = q.shape
    n_cmp = kcc.shape[1]
    n_sel = S // SEL_BLOCK
    assert n_sel <= LANES and S % (2 * SEL_KEY_TILE) == 0 and S >= WINDOW + Q_TILE
    ci = np.arange(n_cmp)[None, :] * CMP_STRIDE
    sj = np.arange(LANES)[:, None] * SEL_BLOCK
    overlap_t = ((ci < sj + SEL_BLOCK) & (ci + CMP_BLOCK > sj)).astype(np.float32)
    block_onehot = (np.arange(S)[:, None] // SEL_BLOCK == np.arange(LANES)[None, :])
    block_onehot = jnp.asarray(block_onehot.astype(np.float32), dtype=BF16)
    cols = NSA_HEADS * Q_TILE
    full = lambda n, w: pl.BlockSpec((1, n, w), lambda b, i: (b, 0, 0))
    full_t = pl.BlockSpec((128, S), lambda b, i: (0, b))
    tile = lambda w: pl.BlockSpec((1, Q_TILE, w), lambda b, i: (b, i, 0))
    return pl.pallas_call(
        _nsa_kernel,
        grid=(B, S // Q_TILE),
        in_specs=[tile(NSA_WIDTH), full(n_cmp, 128), full(n_cmp, 128), full(S, 128), full_t,
                  full(S, 128), full_t, tile(128), _resident(overlap_t.shape), _resident((S, LANES))],
        out_specs=tile(NSA_WIDTH),
        out_shape=jax.ShapeDtypeStruct((B, S, NSA_WIDTH), BF16),
        scratch_shapes=[pltpu.VMEM((2 * LANES, cols), BF16),
                        pltpu.VMEM((LANES, cols), F32), pltpu.VMEM((LANES, cols), F32),
                        pltpu.VMEM((LANES, cols), F32), pltpu.VMEM((1, cols), F32),
                        pltpu.VMEM((1, cols), F32), pltpu.VMEM((1, cols), F32),
                        pltpu.VMEM((LANES + 16, cols), F32)],
        compiler_params=pltpu.CompilerParams(dimension_semantics=("parallel", "parallel"),
                                             vmem_limit_bytes=VMEM_LIMIT),
        name="nsa",
    )(q, kcc, vcc, ks, vst, kw, vwt, gn, jnp.asarray(overlap_t, dtype=BF16), block_onehot)


POOL_HALO = 16


def _merge_kernel(x_ref, xh_ref, yn_ref, kvm_ref, g_ref, wpool_ref, wqx_ref, wgbr_ref, poolw_ref, pscale_ref,
                  wbp_ref, wbn_ref, wbx_ref, wout_ref, gpost_ref, o_ref, *, S):
    i = pl.program_id(0)
    tm = x_ref.shape[0]
    t0 = (i * tm) % S
    x = x_ref[...]
    g = g_ref[...]
    h = _rms(x, g)
    hb = h.astype(BF16)

    hh = _rms(xh_ref[...], g).astype(BF16)
    u_halo = jnp.dot(hh, wpool_ref[...], preferred_element_type=F32)
    u_halo = jnp.where(t0 > 0, u_halo, 0.0)
    u = jnp.dot(hb, wpool_ref[...], preferred_element_type=F32)
    ue = jnp.concatenate([u_halo, u], axis=0)
    trow = t0 + lax.broadcasted_iota(jnp.int32, (tm, 1), 0) + 1
    ypool = []
    for gi, w in enumerate(POOL_WINDOWS):
        acc = ue[:, gi * POOL_GROUP:(gi + 1) * POOL_GROUP]
        step = 1
        while step < w:
            acc = acc + pltpu.roll(acc, step, 0)
            step *= 2
        cnt = jnp.minimum(trow, w).astype(F32)
        p = acc[POOL_HALO:, :] / cnt - u[:, gi * POOL_GROUP:(gi + 1) * POOL_GROUP]
        ypool.append(jnp.dot(p.astype(BF16), poolw_ref[gi], preferred_element_type=F32))
    ypool = jnp.concatenate(ypool, axis=1) * pscale_ref[...]

    qx = jnp.dot(hb, wqx_ref[...], preferred_element_type=F32)
    ymem = []
    for hd in range(XA_HEADS):
        km = kvm_ref[0, :, hd * XA_HEAD_DIM:(hd + 1) * XA_HEAD_DIM]
        vm = kvm_ref[0, :, XA_WIDTH + hd * XA_HEAD_DIM:XA_WIDTH + (hd + 1) * XA_HEAD_DIM]
        s = _dot_nt(qx[:, hd * XA_HEAD_DIM:(hd + 1) * XA_HEAD_DIM].astype(BF16), km) * (XA_HEAD_DIM ** -0.5)
        e = jnp.exp(s - jnp.max(s, axis=-1, keepdims=True))
        p = e / jnp.sum(e, axis=-1, keepdims=True)
        ymem.append(jnp.dot(p.astype(BF16), vm, preferred_element_type=F32))
    ymem = jnp.concatenate(ymem, axis=1)

    gbr = jax.nn.sigmoid(jnp.dot(hb, wgbr_ref[...], preferred_element_type=F32))
    y = (gbr[:, 0:D_MODEL] * jnp.dot(ypool.astype(BF16), wbp_ref[...], preferred_element_type=F32)
         + gbr[:, D_MODEL:2 * D_MODEL] * jnp.dot(yn_ref[...], wbn_ref[...], preferred_element_type=F32)
         + gbr[:, 2 * D_MODEL:3 * D_MODEL] * jnp.dot(ymem.astype(BF16), wbx_ref[...], preferred_element_type=F32))
    o = jnp.dot(y.astype(BF16), wout_ref[...], preferred_element_type=F32)
    o_ref[...] = x + _rms(o, gpost_ref[...])


def _merge(x2, ynsa, kvm, g, wpool, wqx, wgbr, poolw, pscale, wbp, wbn, wbx, wout, gpost, S, tm=256):
    T = x2.shape[0]
    M = kvm.shape[1]
    row = lambda w: pl.BlockSpec((tm, w), lambda i: (i, 0))
    halo = pl.BlockSpec((POOL_HALO, D_MODEL), lambda i: (jnp.maximum(i * (tm // POOL_HALO) - 1, 0), 0))
    return pl.pallas_call(
        functools.partial(_merge_kernel, S=S),
        grid=(T // tm,),
        in_specs=[row(D_MODEL), halo, row(NSA_WIDTH),
                  pl.BlockSpec((1, M, 2 * XA_WIDTH), lambda i: ((i * tm) // S, 0, 0)),
                  _resident((1, D_MODEL)), _resident(wpool.shape), _resident(wqx.shape), _resident(wgbr.shape),
                  _resident(poolw.shape), _resident(pscale.shape), _resident(wbp.shape), _resident(wbn.shape),
                  _resident(wbx.shape), _resident(wout.shape), _resident((1, D_MODEL))],
        out_specs=row(D_MODEL),
        out_shape=jax.ShapeDtypeStruct((T, D_MODEL), F32),
        compiler_params=pltpu.CompilerParams(dimension_semantics=("parallel",), vmem_limit_bytes=VMEM_LIMIT),
        name="merge",
    )(x2, x2, ynsa, kvm, g, wpool, wqx, wgbr, poolw, pscale, wbp, wbn, wbx, wout, gpost)


FFN_HALO = 8
FFN_CHUNK = 1408


def _ffn_kernel(x_ref, xh_ref, g_ref, wup_ref, cw_ref, cb_ref, wdn_ref, gpost_ref, o_ref, *, S):
    i = pl.program_id(0)
    tm = x_ref.shape[0]
    t0 = (i * tm) % S
    x = x_ref[...]
    g = g_ref[...]
    hh = jnp.where(t0 > 0, _rms(xh_ref[...], g), 0.0)
    he = jnp.concatenate([hh, _rms(x, g)], axis=0).astype(BF16)

    def conv(col0):
        u = jnp.dot(he, wup_ref[:, col0:col0 + FFN_CHUNK], preferred_element_type=F32)
        c = cb_ref[:, col0:col0 + FFN_CHUNK] + cw_ref[CONV_WIDTH - 1:CONV_WIDTH, col0:col0 + FFN_CHUNK] * u
        for k in range(1, CONV_WIDTH):
            tap = cw_ref[CONV_WIDTH - 1 - k:CONV_WIDTH - k, col0:col0 + FFN_CHUNK]
            c = c + tap * pltpu.roll(u, k, 0)
        return c[FFN_HALO:, :]

    f = jnp.zeros((tm, D_MODEL), F32)
    for j in range(D_FF // FFN_CHUNK):
        act = _gelu_tanh(conv(j * FFN_CHUNK)) * conv(D_FF + j * FFN_CHUNK)
        f = f + jnp.dot(act.astype(BF16), wdn_ref[j * FFN_CHUNK:(j + 1) * FFN_CHUNK, :],
                        preferred_element_type=F32)
    o_ref[...] = x + _rms(f, gpost_ref[...])


def _ffn(x2, g, wup, cw, cb, wdn, gpost, S, tm=512):
    T = x2.shape[0]
    row = pl.BlockSpec((tm, D_MODEL), lambda i: (i, 0))
    halo = pl.BlockSpec((FFN_HALO, D_MODEL), lambda i: (jnp.maximum(i * (tm // FFN_HALO) - 1, 0), 0))
    return pl.pallas_call(
        functools.partial(_ffn_kernel, S=S),
        grid=(T // tm,),
        in_specs=[row, halo, _resident((1, D_MODEL)), _resident(wup.shape), _resident(cw.shape),
                  _resident(cb.shape), _resident(wdn.shape), _resident((1, D_MODEL))],
        out_specs=row,
        out_shape=jax.ShapeDtypeStruct((T, D_MODEL), F32),
        compiler_params=pltpu.CompilerParams(dimension_semantics=("parallel",), vmem_limit_bytes=VMEM_LIMIT),
        name="ffn",
    )(x2, x2, g, wup, cw, cb, wdn, gpost)


def _layer(x2, mem, pos2, B, S, pre_mix_g, w_in, pool_w, pool_scale, cmp_pe, cmp_w1, cmp_w2, mem_norm_g,
           w_mem_kv, w_br_pool, w_br_nsa, w_br_xa, w_out, post_mix_g, pre_ffn_g, w_up, conv_w, conv_b,
           w_down, post_ffn_g):
    o = np.cumsum((0,) + IN_SIZES)
    w_pool, w_q, w_kv, w_gn, w_qx, w_gbr = (w_in[:, o[k]:o[k + 1]] for k in range(6))
    w_gn = jnp.pad(w_gn, ((0, 0), (0, LANES - w_gn.shape[1])))
    vec = lambda a: a.reshape(1, -1)
    q, kc, vc, ks, vst, kw, vwt, gn = _in_proj(x2, pos2, vec(pre_mix_g), w_q.astype(BF16), w_kv[:, :128].astype(BF16),
                                               w_kv[:, 128:].astype(BF16), w_gn.astype(BF16))
    b3 = lambda a: a.reshape(B, S, a.shape[-1])
    kcc = _compress(b3(kc), cmp_pe[0], cmp_w1[0], cmp_w2[0])
    vcc = _compress(b3(vc), cmp_pe[1], cmp_w1[1], cmp_w2[1])
    ynsa = _nsa(b3(q), kcc, vcc, b3(ks), vst, b3(kw), vwt, b3(gn))
    kvm = _mem_kv(mem, vec(mem_norm_g), w_mem_kv.astype(BF16))
    x2 = _merge(x2, ynsa.reshape(B * S, NSA_WIDTH), kvm, vec(pre_mix_g), w_pool.astype(BF16), w_qx.astype(BF16),
                w_gbr.astype(BF16), pool_w.astype(BF16), vec(pool_scale), w_br_pool.astype(BF16),
                w_br_nsa.astype(BF16), w_br_xa.astype(BF16), w_out.astype(BF16), vec(post_mix_g), S)
    return _ffn(x2, vec(pre_ffn_g), w_up.astype(BF16), conv_w, vec(conv_b), w_down.astype(BF16),
                vec(post_ffn_g), S)


def kernel(x, mem, positions, pre_mix_g, w_in, pool_w, pool_scale, cmp_pe, cmp_w1, cmp_w2, mem_norm_g, w_mem_kv,
           w_br_pool, w_br_nsa, w_br_xa, w_out, post_mix_g, pre_ffn_g, w_up, conv_w, conv_b, w_down, post_ffn_g):
    B, S, D = x.shape
    x2 = x.reshape(B * S, D)
    pos2 = positions.reshape(B * S, 1)
    for l in range(pre_mix_g.shape[0]):
        x2 = _layer(x2, mem, pos2, B, S, pre_mix_g[l], w_in[l], pool_w[l], pool_scale[l], cmp_pe[l], cmp_w1[l],
                    cmp_w2[l], mem_norm_g[l], w_mem_kv[l], w_br_pool[l], w_br_nsa[l], w_br_xa[l], w_out[l],
                    post_mix_g[l], pre_ffn_g[l], w_up[l], conv_w[l], conv_b[l], w_down[l], post_ffn_g[l])
    return x2.reshape(B, S, D)
```

```python
import functools

import numpy as np
import jax
import jax.numpy as jnp
from jax import lax
from jax.experimental import pallas as pl
from jax.experimental.pallas import tpu as pltpu

F32 = jnp.float32
BF16 = jnp.bfloat16

D_MODEL = 1024
EPS = 1e-6
POOL_WINDOWS = (2, 4, 8, 16)
POOL_GROUP = 128
POOL_WIDTH = POOL_GROUP * len(POOL_WINDOWS)
NSA_HEADS = 16
NSA_KV_GROUPS = 2
HEADS_PER_GROUP = NSA_HEADS // NSA_KV_GROUPS
HEAD_DIM = 64
NSA_WIDTH = NSA_HEADS * HEAD_DIM
NSA_KV_WIDTH = NSA_KV_GROUPS * HEAD_DIM
CMP_BLOCK = 32
CMP_STRIDE = 16
CMP_HIDDEN = 256
SEL_BLOCK = 64
SEL_TOPK = 16
WINDOW = 512
ROPE_THETA = 500000.0
ROT_DIM = HEAD_DIM // 4
XA_HEADS = 4
XA_HEAD_DIM = 128
XA_WIDTH = XA_HEADS * XA_HEAD_DIM
N_BRANCHES = 3
D_FF = 2816
CONV_WIDTH = 3
IN_SIZES = (POOL_WIDTH, NSA_WIDTH, 6 * NSA_KV_WIDTH, 3 * NSA_HEADS, XA_WIDTH, N_BRANCHES * D_MODEL)

LANES = 128
Q_TILE = 128
SEL_KEY_TILE = 512
MASK_NEG = -1e9
SCORE_FLOOR = -1e30
LAZY_MAX_HEADROOM = 64.0
VMEM_LIMIT = 56 * 1024 * 1024
Q_SCALE = HEAD_DIM ** -0.5 * 1.4426950408889634


def _rms(x, g):
    return x * lax.rsqrt(jnp.mean(x * x, axis=-1, keepdims=True) + EPS) * g


def _gelu_tanh(x):
    return 0.5 * x * (1.0 + jnp.tanh(0.7978845608028654 * (x + 0.044715 * (x * x * x))))


def _dot_bf16(x, w):
    return jnp.dot(x.astype(BF16), w, preferred_element_type=F32)


def _dot_nt(a, b, precision=None):
    return lax.dot_general(a, b, (((1,), (1,)), ((), ())), precision=precision,
                           preferred_element_type=F32)


def _resident(shape):
    return pl.BlockSpec(shape, lambda *_: (0,) * len(shape), pipeline_mode=pl.Buffered(1))


def _inproj_kernel(x_ref, pos_ref, g_ref, wq_ref, wkc_ref, wkv_ref, wgn_ref, frq_ref, sg1_ref, sg2_ref,
                   q_ref, kc_ref, vc_ref, ks_ref, vs_ref, kw_ref, vw_ref, gn_ref):
    h = _rms(x_ref[...], g_ref[...])
    hb = h.astype(BF16)
    ang = pos_ref[...].astype(F32) * frq_ref[...]
    c = jnp.cos(ang)
    s = jnp.sin(ang)
    s1 = s * sg1_ref[...]
    s2 = s * sg2_ref[...]

    def rope(v):
        return v * c + pltpu.roll(v, LANES - ROT_DIM // 2, 1) * s1 + pltpu.roll(v, ROT_DIM // 2, 1) * s2

    q = jnp.dot(hb, wq_ref[...], preferred_element_type=F32)
    for j in range(NSA_WIDTH // LANES):
        q_ref[:, j * LANES:(j + 1) * LANES] = rope(q[:, j * LANES:(j + 1) * LANES]) * Q_SCALE
    kc_ref[...] = rope(jnp.dot(hb, wkc_ref[...], preferred_element_type=F32))
    kv = jnp.dot(hb, wkv_ref[...], preferred_element_type=F32)
    vc_ref[...] = kv[:, 0:128]
    ks_ref[...] = rope(kv[:, 128:256]).astype(BF16)
    vs_ref[...] = kv[:, 256:384].T.astype(BF16)
    kw_ref[...] = rope(kv[:, 384:512]).astype(BF16)
    vw_ref[...] = kv[:, 512:640].T.astype(BF16)
    gn_ref[...] = jax.nn.sigmoid(jnp.dot(hb, wgn_ref[...], preferred_element_type=F32))


def _in_proj(x2, pos2, g, wq, wkc, wkv, wgn, tm=512):
    T = x2.shape[0]
    half = ROT_DIM // 2
    d = np.arange(LANES) % HEAD_DIM
    inv_freq = ROPE_THETA ** (-jnp.arange(half, dtype=F32) * (2.0 / ROT_DIM))
    frq = jnp.where(d < ROT_DIM, inv_freq[d % half], 0.0)[None, :]
    sg1 = np.where(d < half, -1.0, 0.0).astype(np.float32)[None, :]
    sg2 = np.where((d >= half) & (d < ROT_DIM), 1.0, 0.0).astype(np.float32)[None, :]
    row = lambda w: pl.BlockSpec((tm, w), lambda i: (i, 0))
    col = pl.BlockSpec((128, tm), lambda i: (0, i))
    out_shape = [jax.ShapeDtypeStruct((T, NSA_WIDTH), F32),
                 jax.ShapeDtypeStruct((T, 128), F32), jax.ShapeDtypeStruct((T, 128), F32),
                 jax.ShapeDtypeStruct((T, 128), BF16), jax.ShapeDtypeStruct((128, T), BF16),
                 jax.ShapeDtypeStruct((T, 128), BF16), jax.ShapeDtypeStruct((128, T), BF16),
                 jax.ShapeDtypeStruct((T, 128), F32)]
    return pl.pallas_call(
        _inproj_kernel,
        grid=(T // tm,),
        in_specs=[row(D_MODEL), row(1), _resident((1, D_MODEL)), _resident(wq.shape), _resident(wkc.shape),
                  _resident(wkv.shape), _resident(wgn.shape), _resident((1, LANES)), _resident((1, LANES)),
                  _resident((1, LANES))],
        out_specs=[row(NSA_WIDTH), row(128), row(128), row(128), col, row(128), col, row(128)],
        out_shape=out_shape,
        compiler_params=pltpu.CompilerParams(dimension_semantics=("parallel",), vmem_limit_bytes=VMEM_LIMIT),
        name="in_proj",
    )(x2, pos2, g, wq, wkc, wkv, wgn, frq, jnp.asarray(sg1), jnp.asarray(sg2))


def _compress_kernel(c_ref, pet_ref, peb_ref, wt_ref, wb_ref, w2_ref, o_ref):
    c = c_ref[0]
    a = _dot_bf16(c + pet_ref[...], wt_ref[...])
    b = _dot_bf16(c + peb_ref[...], wb_ref[...])
    n = c.shape[0]
    hid = a + pltpu.roll(b, n - 1, 0)
    o_ref[0] = _dot_bf16(_gelu_tanh(hid), w2_ref[...])


def _compress(kv, pe, w1, w2):
    B, S, _ = kv.shape
    n = S // CMP_STRIDE
    G = NSA_KV_GROUPS
    half = CMP_BLOCK // 2
    eye = jnp.eye(G, dtype=F32)

    def blockdiag_w1(w):
        w = w.reshape(half, 1, HEAD_DIM, 1, CMP_HIDDEN) * eye.reshape(1, G, 1, G, 1)
        return w.reshape(half * G * HEAD_DIM, G * CMP_HIDDEN)

    wt = blockdiag_w1(w1[:half * HEAD_DIM]).astype(BF16)
    wb = blockdiag_w1(w1[half * HEAD_DIM:]).astype(BF16)
    w2b = ((w2.reshape(1, CMP_HIDDEN, 1, HEAD_DIM) * eye.reshape(G, 1, G, 1))
           .reshape(G * CMP_HIDDEN, G * HEAD_DIM).astype(BF16))
    pet = jnp.broadcast_to(pe[:half, None, :], (half, G, HEAD_DIM)).reshape(1, half * G * HEAD_DIM)
    peb = jnp.broadcast_to(pe[half:, None, :], (half, G, HEAD_DIM)).reshape(1, half * G * HEAD_DIM)
    c = kv.reshape(B, n, CMP_STRIDE * 128)
    return pl.pallas_call(
        _compress_kernel,
        grid=(B,),
        in_specs=[pl.BlockSpec((1, n, CMP_STRIDE * 128), lambda b: (b, 0, 0)),
                  _resident(pet.shape), _resident(peb.shape), _resident(wt.shape), _resident(wb.shape),
                  _resident(w2b.shape)],
        out_specs=pl.BlockSpec((1, n, 128), lambda b: (b, 0, 0)),
        out_shape=jax.ShapeDtypeStruct((B, n, 128), F32),
        compiler_params=pltpu.CompilerParams(dimension_semantics=("parallel",), vmem_limit_bytes=VMEM_LIMIT),
        name="compress",
    )(c, pet, peb, wt, wb, w2b)


def _memkv_kernel(m_ref, g_ref, w_ref, o_ref):
    mn = _rms(m_ref[0], g_ref[...])
    o_ref[0] = jnp.dot(mn.astype(BF16), w_ref[...], preferred_element_type=F32).astype(BF16)


def _mem_kv(mem, g, w):
    B, M, _ = mem.shape
    return pl.pallas_call(
        _memkv_kernel,
        grid=(B,),
        in_specs=[pl.BlockSpec((1, M, D_MODEL), lambda b: (b, 0, 0)), _resident((1, D_MODEL)),
                  _resident(w.shape)],
        out_specs=pl.BlockSpec((1, M, 2 * XA_WIDTH), lambda b: (b, 0, 0)),
        out_shape=jax.ShapeDtypeStruct((B, M, 2 * XA_WIDTH), BF16),
        compiler_params=pltpu.CompilerParams(dimension_semantics=("parallel",), vmem_limit_bytes=VMEM_LIMIT),
        name="mem_kv",
    )(mem, g, w)


def _nsa_kernel(q_ref, kcc_ref, vcc_ref, ks_ref, vst_ref, kw_ref, vwt_ref, gn_ref, ovt_ref, blk_ref, o_ref,
                qat_ref, oct_ref, ost_ref, owt_ref, m_ref, l_ref, ap_ref, pvp_ref):
    qt = pl.program_id(1)
    q0 = qt * Q_TILE
    n_cmp = kcc_ref.shape[1]
    head_cols = [slice(h * Q_TILE, (h + 1) * Q_TILE) for h in range(NSA_HEADS)]
    group_cols = [slice(g * HEADS_PER_GROUP * Q_TILE, (g + 1) * HEADS_PER_GROUP * Q_TILE)
                  for g in range(NSA_KV_GROUPS)]
    per_head = lambda a, n: jnp.concatenate([a] * n, axis=1)

    zeros = jnp.zeros((HEAD_DIM, Q_TILE), F32)
    for j in range(NSA_HEADS // 2):
        t = q_ref[0, :, j * LANES:(j + 1) * LANES].T
        for k in range(2):
            h = 2 * j + k
            part = t[k * HEAD_DIM:(k + 1) * HEAD_DIM, :]
            pair = [part, zeros] if h // HEADS_PER_GROUP == 0 else [zeros, part]
            qat_ref[0:LANES, head_cols[h]] = jnp.concatenate(pair, axis=0).astype(BF16)

    crow = lax.broadcasted_iota(jnp.int32, (n_cmp, Q_TILE), 0)
    cq = lax.broadcasted_iota(jnp.int32, (n_cmp, Q_TILE), 1) + q0
    valid_c = crow * CMP_STRIDE + (CMP_BLOCK - 1) <= cq
    bias_c = per_head(jnp.where(valid_c, 0.0, SCORE_FLOOR), HEADS_PER_GROUP)
    seen_c = per_head(jnp.where(cq[0:1, :] >= CMP_BLOCK - 1, 1.0, 0.0), HEADS_PER_GROUP)
    kcc = kcc_ref[0].astype(BF16)
    vcc_ones_ov = jnp.concatenate([vcc_ref[0].T.astype(BF16), jnp.ones((16, n_cmp), BF16), ovt_ref[...]], axis=0)
    brow = lax.broadcasted_iota(jnp.int32, (LANES, Q_TILE), 0)
    cur = (lax.broadcasted_iota(jnp.int32, (LANES, Q_TILE), 1) + q0) // SEL_BLOCK
    forced = (brow == 0) | (brow == cur) | (brow == cur - 1)
    brow_f = brow.astype(F32)
    for g in range(NSA_KV_GROUPS):
        cols = group_cols[g]
        s = jnp.dot(kcc, qat_ref[0:LANES, cols], preferred_element_type=F32) + bias_c
        p = jnp.exp2(s - jnp.max(s, axis=0, keepdims=True)).astype(BF16)
        prod = jnp.dot(vcc_ones_ov, p, preferred_element_type=F32)
        norm = seen_c / jnp.maximum(prod[LANES:LANES + 1], jnp.finfo(F32).tiny)
        oct_ref[:, cols] = prod[0:LANES] * norm
        imp_heads = prod[LANES + 16:2 * LANES + 16] * norm
        imp = imp_heads[:, 0:Q_TILE]
        for k in range(1, HEADS_PER_GROUP):
            imp = imp + imp_heads[:, k * Q_TILE:(k + 1) * Q_TILE]
        v = jnp.where(brow > cur, -1.0, jnp.where(forced, jnp.inf, imp))
        sel = jnp.zeros((LANES, Q_TILE), F32)
        for _ in range(SEL_TOPK):
            mx = jnp.max(v, axis=0, keepdims=True)
            first = jnp.min(jnp.where(v == mx, brow_f, float(LANES)), axis=0, keepdims=True)
            pick = brow_f == first
            sel = jnp.where(pick, 1.0, sel)
            v = jnp.where(pick, -2.0, v)
        neg = jnp.where((sel > 0.0) & (brow < 2 * qt), 0.0, MASK_NEG).astype(BF16)
        for h in range(HEADS_PER_GROUP):
            qat_ref[LANES:2 * LANES, head_cols[g * HEADS_PER_GROUP + h]] = neg

    l_ref[...] = jnp.zeros(l_ref.shape, F32)
    ost_ref[...] = jnp.zeros(ost_ref.shape, F32)

    def scores(kt):
        k0 = pl.multiple_of(kt * SEL_KEY_TILE, SEL_KEY_TILE)
        ka = jnp.concatenate([ks_ref[0, pl.ds(k0, SEL_KEY_TILE), :], blk_ref[pl.ds(k0, SEL_KEY_TILE), :]],
                             axis=1)
        return jnp.dot(ka, qat_ref[...], preferred_element_type=F32)

    def values(start, n):
        return jnp.concatenate([vst_ref[:, pl.ds(start, n)], jnp.ones((16, n), BF16)], axis=0)

    def online_update(s, vt):
        m_old = m_ref[...]
        m_new = jnp.maximum(m_old, jnp.max(s, axis=0, keepdims=True))
        alpha = jnp.exp2(m_old - m_new)
        pv = jnp.dot(vt, jnp.exp2(s - m_new).astype(BF16), preferred_element_type=F32)
        ost_ref[...] = alpha * ost_ref[...] + pv[0:LANES]
        l_ref[...] = alpha * l_ref[...] + pv[LANES:LANES + 1]
        m_ref[...] = m_new

    qd = pl.multiple_of(q0, Q_TILE)
    drow = lax.broadcasted_iota(jnp.int32, (Q_TILE, Q_TILE), 0)
    dcol = lax.broadcasted_iota(jnp.int32, (Q_TILE, Q_TILE), 1)
    bias_d = per_head(jnp.where(drow <= dcol, 0.0, SCORE_FLOOR), NSA_HEADS)
    m_ref[...] = jnp.full(m_ref.shape, SCORE_FLOOR, F32)
    online_update(jnp.dot(ks_ref[0, pl.ds(qd, Q_TILE), :], qat_ref[0:LANES, :], preferred_element_type=F32) + bias_d,
                  values(qd, Q_TILE))

    pvp_ref[...] = jnp.zeros(pvp_ref.shape, F32)
    ap_ref[...] = jnp.ones(ap_ref.shape, F32)

    def fold_parked():
        alpha = ap_ref[...]
        ost_ref[...] = alpha * (ost_ref[...] + pvp_ref[0:LANES, :])
        l_ref[...] = alpha * (l_ref[...] + pvp_ref[LANES:LANES + 1, :])

    def key_tile(kt, carry):
        k0 = pl.multiple_of(kt * SEL_KEY_TILE, SEL_KEY_TILE)
        fold_parked()
        s = scores(kt)
        m_old = m_ref[...]
        pvp_ref[...] = jnp.dot(values(k0, SEL_KEY_TILE), jnp.exp2(s - m_old).astype(BF16),
                               preferred_element_type=F32)
        t_max = jnp.max(s, axis=0, keepdims=True)
        safe = jnp.max(t_max - m_old) <= LAZY_MAX_HEADROOM

        @pl.when(safe)
        def _():
            m_new = jnp.maximum(m_old, t_max)
            ap_ref[...] = jnp.exp2(m_old - m_new)
            m_ref[...] = m_new

        @pl.when(jnp.logical_not(safe))
        def _():
            pvp_ref[...] = jnp.zeros(pvp_ref.shape, F32)
            ap_ref[...] = jnp.ones(ap_ref.shape, F32)
            online_update(scores(kt), values(k0, SEL_KEY_TILE))

        return carry

    lax.fori_loop(0, (q0 + SEL_KEY_TILE - 1) // SEL_KEY_TILE, key_tile, 0)
    fold_parked()

    wlen = WINDOW + Q_TILE
    w0 = pl.multiple_of(jnp.maximum(q0 - WINDOW, 0), Q_TILE)
    kwn = kw_ref[0, pl.ds(w0, wlen), :]
    vwnt = jnp.concatenate([vwt_ref[:, pl.ds(w0, wlen)], jnp.ones((16, wlen), BF16)], axis=0)
    kidx = w0 + lax.broadcasted_iota(jnp.int32, (wlen, Q_TILE), 0)
    tqw = q0 + lax.broadcasted_iota(jnp.int32, (wlen, Q_TILE), 1)
    bias_w = per_head(jnp.where((kidx <= tqw) & (tqw - kidx < WINDOW), 0.0, SCORE_FLOOR), NSA_HEADS)
    s = jnp.dot(kwn, qat_ref[0:LANES, :], preferred_element_type=F32) + bias_w
    p = jnp.exp2(s - jnp.max(s, axis=0, keepdims=True)).astype(BF16)
    o = jnp.dot(vwnt, p, preferred_element_type=F32)
    owt_ref[...] = o[0:LANES] / o[LANES:LANES + 1]

    gt = gn_ref[0].T
    for j in range(NSA_HEADS // 2):
        parts = []
        for h in (2 * j, 2 * j + 1):
            g = h // HEADS_PER_GROUP
            rows = slice(g * HEAD_DIM, (g + 1) * HEAD_DIM)
            cols = head_cols[h]
            o_sel = ost_ref[rows, cols] / l_ref[:, cols]
            parts.append(gt[3 * h:3 * h + 1, :] * oct_ref[rows, cols] + gt[3 * h + 1:3 * h + 2, :] * o_sel
                         + gt[3 * h + 2:3 * h + 3, :] * owt_ref[rows, cols])
        o_ref[0, :, j * LANES:(j + 1) * LANES] = jnp.concatenate(parts, axis=0).T.astype(o_ref.dtype)


def _nsa(q, kcc, vcc, ks, vst, kw, vwt, gn):
    B, S, _ = q.shape
    n_cmp = kcc.shape[1]
    n_sel = S // SEL_BLOCK
    assert n_sel <= LANES and S % (2 * SEL_KEY_TILE) == 0 and S >= WINDOW + Q_TILE
    ci = np.arange(n_cmp)[None, :] * CMP_STRIDE
    sj = np.arange(LANES)[:, None] * SEL_BLOCK
    overlap_t = ((ci < sj + SEL_BLOCK) & (ci + CMP_BLOCK > sj)).astype(np.float32)
    block_onehot = (np.arange(S)[:, None] // SEL_BLOCK == np.arange(LANES)[None, :])
    block_onehot = jnp.asarray(block_onehot.astype(np.float32), dtype=BF16)
    cols = NSA_HEADS * Q_TILE
    full = lambda n, w: pl.BlockSpec((1, n, w), lambda b, i: (b, 0, 0))
    full_t = pl.BlockSpec((128, S), lambda b, i: (0, b))
    tile = lambda w: pl.BlockSpec((1, Q_TILE, w), lambda b, i: (b, i, 0))
    return pl.pallas_call(
        _nsa_kernel,
        grid=(B, S // Q_TILE),
        in_specs=[tile(NSA_WIDTH), full(n_cmp, 128), full(n_cmp, 128), full(S, 128), full_t,
                  full(S, 128), full_t, tile(128), _resident(overlap_t.shape), _resident((S, LANES))],
        out_specs=tile(NSA_WIDTH),
        out_shape=jax.ShapeDtypeStruct((B, S, NSA_WIDTH), BF16),
        scratch_shapes=[pltpu.VMEM((2 * LANES, cols), BF16),
                        pltpu.VMEM((LANES, cols), F32), pltpu.VMEM((LANES, cols), F32),
                        pltpu.VMEM((LANES, cols), F32), pltpu.VMEM((1, cols), F32),
                        pltpu.VMEM((1, cols), F32), pltpu.VMEM((1, cols), F32),
                        pltpu.VMEM((LANES + 16, cols), F32)],
        compiler_params=pltpu.CompilerParams(dimension_semantics=("parallel", "parallel"),
                                             vmem_limit_bytes=VMEM_LIMIT),
        name="nsa",
    )(q, kcc, vcc, ks, vst, kw, vwt, gn, jnp.asarray(overlap_t, dtype=BF16), block_onehot)


POOL_HALO = 16


def _merge_kernel(x_ref, xh_ref, yn_ref, kvm_ref, g_ref, wpool_ref, wqx_ref, wgbr_ref, poolw_ref, pscale_ref,
                  wbp_ref, wbn_ref, wbx_ref, wout_ref, gpost_ref, o_ref, *, S):
    i = pl.program_id(0)
    tm = x_ref.shape[0]
    t0 = (i * tm) % S
    x = x_ref[...]
    g = g_ref[...]
    h = _rms(x, g)
    hb = h.astype(BF16)

    hh = _rms(xh_ref[...], g).astype(BF16)
    u_halo = jnp.dot(hh, wpool_ref[...], preferred_element_type=F32)
    u_halo = jnp.where(t0 > 0, u_halo, 0.0)
    u = jnp.dot(hb, wpool_ref[...], preferred_element_type=F32)
    ue = jnp.concatenate([u_halo, u], axis=0)
    trow = t0 + lax.broadcasted_iota(jnp.int32, (tm, 1), 0) + 1
    ypool = []
    for gi, w in enumerate(POOL_WINDOWS):
        acc = ue[:, gi * POOL_GROUP:(gi + 1) * POOL_GROUP]
        step = 1
        while step < w:
            acc = acc + pltpu.roll(acc, step, 0)
            step *= 2
        cnt = jnp.minimum(trow, w).astype(F32)
        p = acc[POOL_HALO:, :] / cnt - u[:, gi * POOL_GROUP:(gi + 1) * POOL_GROUP]
        ypool.append(jnp.dot(p.astype(BF16), poolw_ref[gi], preferred_element_type=F32))
    ypool = jnp.concatenate(ypool, axis=1) * pscale_ref[...]

    qx = jnp.dot(hb, wqx_ref[...], preferred_element_type=F32)
    ymem = []
    for hd in range(XA_HEADS):
        km = kvm_ref[0, :, hd * XA_HEAD_DIM:(hd + 1) * XA_HEAD_DIM]
        vm = kvm_ref[0, :, XA_WIDTH + hd * XA_HEAD_DIM:XA_WIDTH + (hd + 1) * XA_HEAD_DIM]
        s = _dot_nt(qx[:, hd * XA_HEAD_DIM:(hd + 1) * XA_HEAD_DIM].astype(BF16), km) * (XA_HEAD_DIM ** -0.5)
        e = jnp.exp(s - jnp.max(s, axis=-1, keepdims=True))
        p = e / jnp.sum(e, axis=-1, keepdims=True)
        ymem.append(jnp.dot(p.astype(BF16), vm, preferred_element_type=F32))
    ymem = jnp.concatenate(ymem, axis=1)

    gbr = jax.nn.sigmoid(jnp.dot(hb, wgbr_ref[...], preferred_element_type=F32))
    y = (gbr[:, 0:D_MODEL] * jnp.dot(ypool.astype(BF16), wbp_ref[...], preferred_element_type=F32)
         + gbr[:, D_MODEL:2 * D_MODEL] * jnp.dot(yn_ref[...], wbn_ref[...], preferred_element_type=F32)
         + gbr[:, 2 * D_MODEL:3 * D_MODEL] * jnp.dot(ymem.astype(BF16), wbx_ref[...], preferred_element_type=F32))
    o = jnp.dot(y.astype(BF16), wout_ref[...], preferred_element_type=F32)
    o_ref[...] = x + _rms(o, gpost_ref[...])


def _merge(x2, ynsa, kvm, g, wpool, wqx, wgbr, poolw, pscale, wbp, wbn, wbx, wout, gpost, S, tm=512):
    T = x2.shape[0]
    M = kvm.shape[1]
    row = lambda w: pl.BlockSpec((tm, w), lambda i: (i, 0))
    halo = pl.BlockSpec((POOL_HALO, D_MODEL), lambda i: (jnp.maximum(i * (tm // POOL_HALO) - 1, 0), 0))
    return pl.pallas_call(
        functools.partial(_merge_kernel, S=S),
        grid=(T // tm,),
        in_specs=[row(D_MODEL), halo, row(NSA_WIDTH),
                  pl.BlockSpec((1, M, 2 * XA_WIDTH), lambda i: ((i * tm) // S, 0, 0)),
                  _resident((1, D_MODEL)), _resident(wpool.shape), _resident(wqx.shape), _resident(wgbr.shape),
                  _resident(poolw.shape), _resident(pscale.shape), _resident(wbp.shape), _resident(wbn.shape),
                  _resident(wbx.shape), _resident(wout.shape), _resident((1, D_MODEL))],
        out_specs=row(D_MODEL),
        out_shape=jax.ShapeDtypeStruct((T, D_MODEL), F32),
        compiler_params=pltpu.CompilerParams(dimension_semantics=("parallel",), vmem_limit_bytes=VMEM_LIMIT),
        name="merge",
    )(x2, x2, ynsa, kvm, g, wpool, wqx, wgbr, poolw, pscale, wbp, wbn, wbx, wout, gpost)


FFN_HALO = 8
FFN_CHUNK = 2816


def _ffn_kernel(x_ref, xh_ref, g_ref, wup_ref, cw_ref, cb_ref, wdn_ref, gpost_ref, o_ref, *, S):
    i = pl.program_id(0)
    tm = x_ref.shape[0]
    t0 = (i * tm) % S
    x = x_ref[...]
    g = g_ref[...]
    hh = jnp.where(t0 > 0, _rms(xh_ref[...], g), 0.0)
    he = jnp.concatenate([hh, _rms(x, g)], axis=0).astype(BF16)

    def conv(col0):
        u = jnp.dot(he, wup_ref[:, col0:col0 + FFN_CHUNK], preferred_element_type=F32)
        c = cb_ref[:, col0:col0 + FFN_CHUNK] + cw_ref[CONV_WIDTH - 1:CONV_WIDTH, col0:col0 + FFN_CHUNK] * u
        for k in range(1, CONV_WIDTH):
            tap = cw_ref[CONV_WIDTH - 1 - k:CONV_WIDTH - k, col0:col0 + FFN_CHUNK]
            c = c + tap * pltpu.roll(u, k, 0)
        return c[FFN_HALO:, :]

    f = jnp.zeros((tm, D_MODEL), F32)
    for j in range(D_FF // FFN_CHUNK):
        act = _gelu_tanh(conv(j * FFN_CHUNK)) * conv(D_FF + j * FFN_CHUNK)
        f = f + jnp.dot(act.astype(BF16), wdn_ref[j * FFN_CHUNK:(j + 1) * FFN_CHUNK, :],
                        preferred_element_type=F32)
    o_ref[...] = x + _rms(f, gpost_ref[...])


def _ffn(x2, g, wup, cw, cb, wdn, gpost, S, tm=512):
    T = x2.shape[0]
    row = pl.BlockSpec((tm, D_MODEL), lambda i: (i, 0))
    halo = pl.BlockSpec((FFN_HALO, D_MODEL), lambda i: (jnp.maximum(i * (tm // FFN_HALO) - 1, 0), 0))
    return pl.pallas_call(
        functools.partial(_ffn_kernel, S=S),
        grid=(T // tm,),
        in_specs=[row, halo, _resident((1, D_MODEL)), _resident(wup.shape), _resident(cw.shape),
                  _resident(cb.shape), _resident(wdn.shape), _resident((1, D_MODEL))],
        out_specs=row,
        out_shape=jax.ShapeDtypeStruct((T, D_MODEL), F32),
        compiler_params=pltpu.CompilerParams(dimension_semantics=("parallel",), vmem_limit_bytes=VMEM_LIMIT),
        name="ffn",
    )(x2, x2, g, wup, cw, cb, wdn, gpost)


def _layer(x2, mem, pos2, B, S, pre_mix_g, w_in, pool_w, pool_scale, cmp_pe, cmp_w1, cmp_w2, mem_norm_g,
           w_mem_kv, w_br_pool, w_br_nsa, w_br_xa, w_out, post_mix_g, pre_ffn_g, w_up, conv_w, conv_b,
           w_down, post_ffn_g):
    o = np.cumsum((0,) + IN_SIZES)
    w_pool, w_q, w_kv, w_gn, w_qx, w_gbr = (w_in[:, o[k]:o[k + 1]] for k in range(6))
    w_gn = jnp.pad(w_gn, ((0, 0), (0, LANES - w_gn.shape[1])))
    vec = lambda a: a.reshape(1, -1)
    q, kc, vc, ks, vst, kw, vwt, gn = _in_proj(x2, pos2, vec(pre_mix_g), w_q.astype(BF16), w_kv[:, :128].astype(BF16),
                                               w_kv[:, 128:].astype(BF16), w_gn.astype(BF16))
    b3 = lambda a: a.reshape(B, S, a.shape[-1])
    kcc = _compress(b3(kc), cmp_pe[0], cmp_w1[0], cmp_w2[0])
    vcc = _compress(b3(vc), cmp_pe[1], cmp_w1[1], cmp_w2[1])
    ynsa = _nsa(b3(q), kcc, vcc, b3(ks), vst, b3(kw), vwt, b3(gn))
    kvm = _mem_kv(mem, vec(mem_norm_g), w_mem_kv.astype(BF16))
    x2 = _merge(x2, ynsa.reshape(B * S, NSA_WIDTH), kvm, vec(pre_mix_g), w_pool.astype(BF16), w_qx.astype(BF16),
                w_gbr.astype(BF16), pool_w.astype(BF16), vec(pool_scale), w_br_pool.astype(BF16),
                w_br_nsa.astype(BF16), w_br_xa.astype(BF16), w_out.astype(BF16), vec(post_mix_g), S)
    return _ffn(x2, vec(pre_ffn_g), w_up.astype(BF16), conv_w, vec(conv_b), w_down.astype(BF16),
                vec(post_ffn_g), S)


def kernel(x, mem, positions, pre_mix_g, w_in, pool_w, pool_scale, cmp_pe, cmp_w1, cmp_w2, mem_norm_g, w_mem_kv,
           w_br_pool, w_br_nsa, w_br_xa, w_out, post_mix_g, pre_ffn_g, w_up, conv_w, conv_b, w_down, post_ffn_g):
    B, S, D = x.shape
    x2 = x.reshape(B * S, D)
    pos2 = positions.reshape(B * S, 1)
    for l in range(pre_mix_g.shape[0]):
        x2 = _layer(x2, mem, pos2, B, S, pre_mix_g[l], w_in[l], pool_w[l], pool_scale[l], cmp_pe[l], cmp_w1[l],
                    cmp_w2[l], mem_norm_g[l], w_mem_kv[l], w_br_pool[l], w_br_nsa[l], w_br_xa[l], w_out[l],
                    post_mix_g[l], pre_ffn_g[l], w_up[l], conv_w[l], conv_b[l], w_down[l], post_ffn_g[l])
    return x2.reshape(B, S, D)
```

```python
import functools

import numpy as np
import jax
import jax.numpy as jnp
from jax import lax
from jax.experimental import pallas as pl
from jax.experimental.pallas import tpu as pltpu

F32 = jnp.float32
BF16 = jnp.bfloat16

D_MODEL = 1024
EPS = 1e-6
POOL_WINDOWS = (2, 4, 8, 16)
POOL_GROUP = 128
POOL_WIDTH = POOL_GROUP * len(POOL_WINDOWS)
NSA_HEADS = 16
NSA_KV_GROUPS = 2
HEADS_PER_GROUP = NSA_HEADS // NSA_KV_GROUPS
HEAD_DIM = 64
NSA_WIDTH = NSA_HEADS * HEAD_DIM
NSA_KV_WIDTH = NSA_KV_GROUPS * HEAD_DIM
CMP_BLOCK = 32
CMP_STRIDE = 16
CMP_HIDDEN = 256
SEL_BLOCK = 64
SEL_TOPK = 16
WINDOW = 512
ROPE_THETA = 500000.0
ROT_DIM = HEAD_DIM // 4
XA_HEADS = 4
XA_HEAD_DIM = 128
XA_WIDTH = XA_HEADS * XA_HEAD_DIM
N_BRANCHES = 3
D_FF = 2816
CONV_WIDTH = 3
IN_SIZES = (POOL_WIDTH, NSA_WIDTH, 6 * NSA_KV_WIDTH, 3 * NSA_HEADS, XA_WIDTH, N_BRANCHES * D_MODEL)

LANES = 128
Q_TILE = 128
SEL_KEY_TILE = 512
MASK_NEG = -1e9
SCORE_FLOOR = -1e30
LAZY_MAX_HEADROOM = 64.0
VMEM_LIMIT = 56 * 1024 * 1024
Q_SCALE = HEAD_DIM ** -0.5 * 1.4426950408889634


def _rms(x, g):
    return x * lax.rsqrt(jnp.mean(x * x, axis=-1, keepdims=True) + EPS) * g


def _gelu_tanh(x):
    return 0.5 * x * (1.0 + jnp.tanh(0.7978845608028654 * (x + 0.044715 * (x * x * x))))


def _dot_bf16(x, w):
    return jnp.dot(x.astype(BF16), w, preferred_element_type=F32)


def _dot_nt(a, b, precision=None):
    return lax.dot_general(a, b, (((1,), (1,)), ((), ())), precision=precision,
                           preferred_element_type=F32)


def _resident(shape):
    return pl.BlockSpec(shape, lambda *_: (0,) * len(shape), pipeline_mode=pl.Buffered(1))


def _inproj_kernel(x_ref, pos_ref, g_ref, wq_ref, wkc_ref, wkv_ref, wgn_ref, frq_ref, sg1_ref, sg2_ref,
                   q_ref, kc_ref, vc_ref, ks_ref, vs_ref, kw_ref, vw_ref, gn_ref):
    h = _rms(x_ref[...], g_ref[...])
    hb = h.astype(BF16)
    ang = pos_ref[...].astype(F32) * frq_ref[...]
    c = jnp.cos(ang)
    s = jnp.sin(ang)
    s1 = s * sg1_ref[...]
    s2 = s * sg2_ref[...]

    def rope(v):
        return v * c + pltpu.roll(v, LANES - ROT_DIM // 2, 1) * s1 + pltpu.roll(v, ROT_DIM // 2, 1) * s2

    q = jnp.dot(hb, wq_ref[...], preferred_element_type=F32)
    for j in range(NSA_WIDTH // LANES):
        q_ref[:, j * LANES:(j + 1) * LANES] = rope(q[:, j * LANES:(j + 1) * LANES]) * Q_SCALE
    kc_ref[...] = rope(jnp.dot(hb, wkc_ref[...], preferred_element_type=F32))
    kv = jnp.dot(hb, wkv_ref[...], preferred_element_type=F32)
    vc_ref[...] = kv[:, 0:128]
    ks_ref[...] = rope(kv[:, 128:256]).astype(BF16)
    vs_ref[...] = kv[:, 256:384].T.astype(BF16)
    kw_ref[...] = rope(kv[:, 384:512]).astype(BF16)
    vw_ref[...] = kv[:, 512:640].T.astype(BF16)
    gn_ref[...] = jax.nn.sigmoid(jnp.dot(hb, wgn_ref[...], preferred_element_type=F32))


def _in_proj(x2, pos2, g, wq, wkc, wkv, wgn, tm=512):
    T = x2.shape[0]
    half = ROT_DIM // 2
    d = np.arange(LANES) % HEAD_DIM
    inv_freq = ROPE_THETA ** (-jnp.arange(half, dtype=F32) * (2.0 / ROT_DIM))
    frq = jnp.where(d < ROT_DIM, inv_freq[d % half], 0.0)[None, :]
    sg1 = np.where(d < half, -1.0, 0.0).astype(np.float32)[None, :]
    sg2 = np.where((d >= half) & (d < ROT_DIM), 1.0, 0.0).astype(np.float32)[None, :]
    row = lambda w: pl.BlockSpec((tm, w), lambda i: (i, 0))
    col = pl.BlockSpec((128, tm), lambda i: (0, i))
    out_shape = [jax.ShapeDtypeStruct((T, NSA_WIDTH), F32),
                 jax.ShapeDtypeStruct((T, 128), F32), jax.ShapeDtypeStruct((T, 128), F32),
                 jax.ShapeDtypeStruct((T, 128), BF16), jax.ShapeDtypeStruct((128, T), BF16),
                 jax.ShapeDtypeStruct((T, 128), BF16), jax.ShapeDtypeStruct((128, T), BF16),
                 jax.ShapeDtypeStruct((T, 128), F32)]
    return pl.pallas_call(
        _inproj_kernel,
        grid=(T // tm,),
        in_specs=[row(D_MODEL), row(1), _resident((1, D_MODEL)), _resident(wq.shape), _resident(wkc.shape),
                  _resident(wkv.shape), _resident(wgn.shape), _resident((1, LANES)), _resident((1, LANES)),
                  _resident((1, LANES))],
        out_specs=[row(NSA_WIDTH), row(128), row(128), row(128), col, row(128), col, row(128)],
        out_shape=out_shape,
        compiler_params=pltpu.CompilerParams(dimension_semantics=("parallel",), vmem_limit_bytes=VMEM_LIMIT),
        name="in_proj",
    )(x2, pos2, g, wq, wkc, wkv, wgn, frq, jnp.asarray(sg1), jnp.asarray(sg2))


def _compress_kernel(c_ref, pet_ref, peb_ref, wt_ref, wb_ref, w2_ref, o_ref):
    c = c_ref[0]
    a = _dot_bf16(c + pet_ref[...], wt_ref[...])
    b = _dot_bf16(c + peb_ref[...], wb_ref[...])
    n = c.shape[0]
    hid = a + pltpu.roll(b, n - 1, 0)
    o_ref[0] = _dot_bf16(_gelu_tanh(hid), w2_ref[...])


def _compress(kv, pe, w1, w2):
    B, S, _ = kv.shape
    n = S // CMP_STRIDE
    G = NSA_KV_GROUPS
    half = CMP_BLOCK // 2
    eye = jnp.eye(G, dtype=F32)

    def blockdiag_w1(w):
        w = w.reshape(half, 1, HEAD_DIM, 1, CMP_HIDDEN) * eye.reshape(1, G, 1, G, 1)
        return w.reshape(half * G * HEAD_DIM, G * CMP_HIDDEN)

    wt = blockdiag_w1(w1[:half * HEAD_DIM]).astype(BF16)
    wb = blockdiag_w1(w1[half * HEAD_DIM:]).astype(BF16)
    w2b = ((w2.reshape(1, CMP_HIDDEN, 1, HEAD_DIM) * eye.reshape(G, 1, G, 1))
           .reshape(G * CMP_HIDDEN, G * HEAD_DIM).astype(BF16))
    pet = jnp.broadcast_to(pe[:half, None, :], (half, G, HEAD_DIM)).reshape(1, half * G * HEAD_DIM)
    peb = jnp.broadcast_to(pe[half:, None, :], (half, G, HEAD_DIM)).reshape(1, half * G * HEAD_DIM)
    c = kv.reshape(B, n, CMP_STRIDE * 128)
    return pl.pallas_call(
        _compress_kernel,
        grid=(B,),
        in_specs=[pl.BlockSpec((1, n, CMP_STRIDE * 128), lambda b: (b, 0, 0)),
                  _resident(pet.shape), _resident(peb.shape), _resident(wt.shape), _resident(wb.shape),
                  _resident(w2b.shape)],
        out_specs=pl.BlockSpec((1, n, 128), lambda b: (b, 0, 0)),
        out_shape=jax.ShapeDtypeStruct((B, n, 128), F32),
        compiler_params=pltpu.CompilerParams(dimension_semantics=("parallel",), vmem_limit_bytes=VMEM_LIMIT),
        name="compress",
    )(c, pet, peb, wt, wb, w2b)


def _memkv_kernel(m_ref, g_ref, w_ref, o_ref):
    mn = _rms(m_ref[0], g_ref[...])
    o_ref[0] = jnp.dot(mn.astype(BF16), w_ref[...], preferred_element_type=F32).astype(BF16)


def _mem_kv(mem, g, w):
    B, M, _ = mem.shape
    return pl.pallas_call(
        _memkv_kernel,
        grid=(B,),
        in_specs=[pl.BlockSpec((1, M, D_MODEL), lambda b: (b, 0, 0)), _resident((1, D_MODEL)),
                  _resident(w.shape)],
        out_specs=pl.BlockSpec((1, M, 2 * XA_WIDTH), lambda b: (b, 0, 0)),
        out_shape=jax.ShapeDtypeStruct((B, M, 2 * XA_WIDTH), BF16),
        compiler_params=pltpu.CompilerParams(dimension_semantics=("parallel",), vmem_limit_bytes=VMEM_LIMIT),
        name="mem_kv",
    )(mem, g, w)


def _nsa_kernel(q_ref, kcc_ref, vcc_ref, ks_ref, vst_ref, kw_ref, vwt_ref, gn_ref, ovt_ref, blk_ref, o_ref,
                qat_ref, oct_ref, ost_ref, owt_ref, m_ref, l_ref, ap_ref, pvp_ref, over_ref):
    qt = pl.program_id(1)
    q0 = qt * Q_TILE
    n_cmp = kcc_ref.shape[1]
    head_cols = [slice(h * Q_TILE, (h + 1) * Q_TILE) for h in range(NSA_HEADS)]
    group_cols = [slice(g * HEADS_PER_GROUP * Q_TILE, (g + 1) * HEADS_PER_GROUP * Q_TILE)
                  for g in range(NSA_KV_GROUPS)]
    per_head = lambda a, n: jnp.concatenate([a] * n, axis=1)

    zeros = jnp.zeros((HEAD_DIM, Q_TILE), F32)
    for j in range(NSA_HEADS // 2):
        t = q_ref[0, :, j * LANES:(j + 1) * LANES].T
        for k in range(2):
            h = 2 * j + k
            part = t[k * HEAD_DIM:(k + 1) * HEAD_DIM, :]
            pair = [part, zeros] if h // HEADS_PER_GROUP == 0 else [zeros, part]
            qat_ref[0:LANES, head_cols[h]] = jnp.concatenate(pair, axis=0).astype(BF16)

    crow = lax.broadcasted_iota(jnp.int32, (n_cmp, Q_TILE), 0)
    cq = lax.broadcasted_iota(jnp.int32, (n_cmp, Q_TILE), 1) + q0
    valid_c = crow * CMP_STRIDE + (CMP_BLOCK - 1) <= cq
    bias_c = per_head(jnp.where(valid_c, 0.0, SCORE_FLOOR), HEADS_PER_GROUP)
    seen_c = per_head(jnp.where(cq[0:1, :] >= CMP_BLOCK - 1, 1.0, 0.0), HEADS_PER_GROUP)
    kcc = kcc_ref[0].astype(BF16)
    vcc_ones_ov = jnp.concatenate([vcc_ref[0].T.astype(BF16), jnp.ones((16, n_cmp), BF16), ovt_ref[...]], axis=0)
    brow = lax.broadcasted_iota(jnp.int32, (LANES, Q_TILE), 0)
    cur = (lax.broadcasted_iota(jnp.int32, (LANES, Q_TILE), 1) + q0) // SEL_BLOCK
    forced = (brow == 0) | (brow == cur) | (brow == cur - 1)
    brow_f = brow.astype(F32)
    for g in range(NSA_KV_GROUPS):
        cols = group_cols[g]
        s = jnp.dot(kcc, qat_ref[0:LANES, cols], preferred_element_type=F32) + bias_c
        p = jnp.exp2(s - jnp.max(s, axis=0, keepdims=True)).astype(BF16)
        prod = jnp.dot(vcc_ones_ov, p, preferred_element_type=F32)
        norm = seen_c / jnp.maximum(prod[LANES:LANES + 1], jnp.finfo(F32).tiny)
        oct_ref[:, cols] = prod[0:LANES] * norm
        imp_heads = prod[LANES + 16:2 * LANES + 16] * norm
        imp = imp_heads[:, 0:Q_TILE]
        for k in range(1, HEADS_PER_GROUP):
            imp = imp + imp_heads[:, k * Q_TILE:(k + 1) * Q_TILE]
        v = jnp.where(brow > cur, -1.0, jnp.where(forced, -2.0, imp))
        for _ in range(SEL_TOPK - 3):
            mx = jnp.max(v, axis=0, keepdims=True)
            first = jnp.min(jnp.where(v == mx, brow_f, float(LANES)), axis=0, keepdims=True)
            v = jnp.where(brow_f == first, -2.0, v)
        neg = jnp.where((v == -2.0) & (brow < 2 * qt), 0.0, MASK_NEG).astype(BF16)
        for h in range(HEADS_PER_GROUP):
            qat_ref[LANES:2 * LANES, head_cols[g * HEADS_PER_GROUP + h]] = neg

    def scores(kt):
        k0 = pl.multiple_of(kt * SEL_KEY_TILE, SEL_KEY_TILE)
        ka = jnp.concatenate([ks_ref[0, pl.ds(k0, SEL_KEY_TILE), :], blk_ref[pl.ds(k0, SEL_KEY_TILE), :]],
                             axis=1)
        return jnp.dot(ka, qat_ref[...], preferred_element_type=F32)

    def values(start, n):
        return jnp.concatenate([vst_ref[:, pl.ds(start, n)], jnp.ones((16, n), BF16)], axis=0)

    def online_update(s, vt):
        m_old = m_ref[...]
        m_new = jnp.maximum(m_old, jnp.max(s, axis=0, keepdims=True))
        alpha = jnp.exp2(m_old - m_new)
        pv = jnp.dot(vt, jnp.exp2(s - m_new).astype(BF16), preferred_element_type=F32)
        ost_ref[...] = alpha * ost_ref[...] + pv[0:LANES]
        l_ref[...] = alpha * l_ref[...] + pv[LANES:LANES + 1]
        m_ref[...] = m_new

    qd = pl.multiple_of(q0, Q_TILE)
    drow = lax.broadcasted_iota(jnp.int32, (Q_TILE, Q_TILE), 0)
    dcol = lax.broadcasted_iota(jnp.int32, (Q_TILE, Q_TILE), 1)
    bias_d = per_head(jnp.where(drow <= dcol, 0.0, SCORE_FLOOR), NSA_HEADS)
    n_tiles = (q0 + SEL_KEY_TILE - 1) // SEL_KEY_TILE

    def start_with_diagonal():
        l_ref[...] = jnp.zeros(l_ref.shape, F32)
        ost_ref[...] = jnp.zeros(ost_ref.shape, F32)
        m_ref[...] = jnp.full(m_ref.shape, SCORE_FLOOR, F32)
        online_update(jnp.dot(ks_ref[0, pl.ds(qd, Q_TILE), :], qat_ref[0:LANES, :],
                              preferred_element_type=F32) + bias_d, values(qd, Q_TILE))

    start_with_diagonal()
    pvp_ref[...] = jnp.zeros(pvp_ref.shape, F32)
    ap_ref[...] = jnp.ones(ap_ref.shape, F32)
    over_ref[...] = jnp.full(over_ref.shape, SCORE_FLOOR, F32)

    def fold_parked():
        alpha = ap_ref[...]
        ost_ref[...] = alpha * (ost_ref[...] + pvp_ref[0:LANES, :])
        l_ref[...] = alpha * (l_ref[...] + pvp_ref[LANES:LANES + 1, :])

    def lazy_tile(kt, carry):
        k0 = pl.multiple_of(kt * SEL_KEY_TILE, SEL_KEY_TILE)
        fold_parked()
        s = scores(kt)
        m_old = m_ref[...]
        pvp_ref[...] = jnp.dot(values(k0, SEL_KEY_TILE), jnp.exp2(s - m_old).astype(BF16),
                               preferred_element_type=F32)
        t_max = jnp.max(s, axis=0, keepdims=True)
        m_new = jnp.maximum(m_old, t_max)
        over_ref[...] = jnp.maximum(over_ref[...], t_max - m_old)
        ap_ref[...] = jnp.exp2(m_old - m_new)
        m_ref[...] = m_new
        return carry

    def lazy_pair(j, carry):
        return lazy_tile(2 * j + 1, lazy_tile(2 * j, carry))

    lax.fori_loop(0, n_tiles // 2, lazy_pair, 0)

    @pl.when(n_tiles % 2 == 1)
    def _():
        lazy_tile(n_tiles - 1, 0)

    fold_parked()

    @pl.when(jnp.max(over_ref[...]) > LAZY_MAX_HEADROOM)
    def _():
        start_with_diagonal()

        def textbook_tile(kt, carry):
            online_update(scores(kt), values(pl.multiple_of(kt * SEL_KEY_TILE, SEL_KEY_TILE), SEL_KEY_TILE))
            return carry

        lax.fori_loop(0, n_tiles, textbook_tile, 0)

    wlen = WINDOW + Q_TILE
    w0 = pl.multiple_of(jnp.maximum(q0 - WINDOW, 0), Q_TILE)
    kwn = kw_ref[0, pl.ds(w0, wlen), :]
    vwnt = jnp.concatenate([vwt_ref[:, pl.ds(w0, wlen)], jnp.ones((16, wlen), BF16)], axis=0)
    kidx = w0 + lax.broadcasted_iota(jnp.int32, (wlen, Q_TILE), 0)
    tqw = q0 + lax.broadcasted_iota(jnp.int32, (wlen, Q_TILE), 1)
    bias_w = per_head(jnp.where((kidx <= tqw) & (tqw - kidx < WINDOW), 0.0, SCORE_FLOOR), NSA_HEADS)
    s = jnp.dot(kwn, qat_ref[0:LANES, :], preferred_element_type=F32) + bias_w
    p = jnp.exp2(s - jnp.max(s, axis=0, keepdims=True)).astype(BF16)
    o = jnp.dot(vwnt, p, preferred_element_type=F32)
    owt_ref[...] = o[0:LANES] / o[LANES:LANES + 1]

    gt = gn_ref[0].T
    for j in range(NSA_HEADS // 2):
        parts = []
        for h in (2 * j, 2 * j + 1):
            g = h // HEADS_PER_GROUP
            rows = slice(g * HEAD_DIM, (g + 1) * HEAD_DIM)
            cols = head_cols[h]
            o_sel = ost_ref[rows, cols] / l_ref[:, cols]
            parts.append(gt[3 * h:3 * h + 1, :] * oct_ref[rows, cols] + gt[3 * h + 1:3 * h + 2, :] * o_sel
                         + gt[3 * h + 2:3 * h + 3, :] * owt_ref[rows, cols])
        o_ref[0, :, j * LANES:(j + 1) * LANES] = jnp.concatenate(parts, axis=0).T.astype(o_ref.dtype)


def _nsa(q, kcc, vcc, ks, vst, kw, vwt, gn):
    B, S, _ = q.shape
    n_cmp = kcc.shape[1]
    n_sel = S // SEL_BLOCK
    assert n_sel <= LANES and S % (2 * SEL_KEY_TILE) == 0 and S >= WINDOW + Q_TILE
    ci = np.arange(n_cmp)[None, :] * CMP_STRIDE
    sj = np.arange(LANES)[:, None] * SEL_BLOCK
    overlap_t = ((ci < sj + SEL_BLOCK) & (ci + CMP_BLOCK > sj)).astype(np.float32)
    block_onehot = (np.arange(S)[:, None] // SEL_BLOCK == np.arange(LANES)[None, :])
    block_onehot = jnp.asarray(block_onehot.astype(np.float32), dtype=BF16)
    cols = NSA_HEADS * Q_TILE
    full = lambda n, w: pl.BlockSpec((1, n, w), lambda b, i: (b, 0, 0))
    full_t = pl.BlockSpec((128, S), lambda b, i: (0, b))
    tile = lambda w: pl.BlockSpec((1, Q_TILE, w), lambda b, i: (b, i, 0))
    return pl.pallas_call(
        _nsa_kernel,
        grid=(B, S // Q_TILE),
        in_specs=[tile(NSA_WIDTH), full(n_cmp, 128), full(n_cmp, 128), full(S, 128), full_t,
                  full(S, 128), full_t, tile(128), _resident(overlap_t.shape), _resident((S, LANES))],
        out_specs=tile(NSA_WIDTH),
        out_shape=jax.ShapeDtypeStruct((B, S, NSA_WIDTH), BF16),
        scratch_shapes=[pltpu.VMEM((2 * LANES, cols), BF16),
                        pltpu.VMEM((LANES, cols), F32), pltpu.VMEM((LANES, cols), F32),
                        pltpu.VMEM((LANES, cols), F32), pltpu.VMEM((1, cols), F32),
                        pltpu.VMEM((1, cols), F32), pltpu.VMEM((1, cols), F32),
                        pltpu.VMEM((LANES + 16, cols), F32), pltpu.VMEM((1, cols), F32)],
        compiler_params=pltpu.CompilerParams(dimension_semantics=("parallel", "parallel"),
                                             vmem_limit_bytes=VMEM_LIMIT),
        name="nsa",
    )(q, kcc, vcc, ks, vst, kw, vwt, gn, jnp.asarray(overlap_t, dtype=BF16), block_onehot)


POOL_HALO = 16


def _merge_kernel(x_ref, xh_ref, yn_ref, kvm_ref, g_ref, wpool_ref, wqx_ref, wgbr_ref, poolw_ref, pscale_ref,
                  wbp_ref, wbn_ref, wbx_ref, wout_ref, gpost_ref, o_ref, *, S):
    i = pl.program_id(0)
    tm = x_ref.shape[0]
    t0 = (i * tm) % S
    x = x_ref[...]
    g = g_ref[...]
    h = _rms(x, g)
    hb = h.astype(BF16)

    hh = _rms(xh_ref[...], g).astype(BF16)
    u_halo = jnp.dot(hh, wpool_ref[...], preferred_element_type=F32)
    u_halo = jnp.where(t0 > 0, u_halo, 0.0)
    u = jnp.dot(hb, wpool_ref[...], preferred_element_type=F32)
    ue = jnp.concatenate([u_halo, u], axis=0)
    trow = t0 + lax.broadcasted_iota(jnp.int32, (tm, 1), 0) + 1
    ypool = []
    for gi, w in enumerate(POOL_WINDOWS):
        acc = ue[:, gi * POOL_GROUP:(gi + 1) * POOL_GROUP]
        step = 1
        while step < w:
            acc = acc + pltpu.roll(acc, step, 0)
            step *= 2
        cnt = jnp.minimum(trow, w).astype(F32)
        p = acc[POOL_HALO:, :] / cnt - u[:, gi * POOL_GROUP:(gi + 1) * POOL_GROUP]
        ypool.append(jnp.dot(p.astype(BF16), poolw_ref[gi], preferred_element_type=F32))
    ypool = jnp.concatenate(ypool, axis=1) * pscale_ref[...]

    qx = jnp.dot(hb, wqx_ref[...], preferred_element_type=F32)
    ymem = []
    for hd in range(XA_HEADS):
        km = kvm_ref[0, :, hd * XA_HEAD_DIM:(hd + 1) * XA_HEAD_DIM]
        vm = kvm_ref[0, :, XA_WIDTH + hd * XA_HEAD_DIM:XA_WIDTH + (hd + 1) * XA_HEAD_DIM]
        s = _dot_nt(qx[:, hd * XA_HEAD_DIM:(hd + 1) * XA_HEAD_DIM].astype(BF16), km) * (XA_HEAD_DIM ** -0.5)
        e = jnp.exp(s - jnp.max(s, axis=-1, keepdims=True))
        p = e / jnp.sum(e, axis=-1, keepdims=True)
        ymem.append(jnp.dot(p.astype(BF16), vm, preferred_element_type=F32))
    ymem = jnp.concatenate(ymem, axis=1)

    gbr = jax.nn.sigmoid(jnp.dot(hb, wgbr_ref[...], preferred_element_type=F32))
    y = (gbr[:, 0:D_MODEL] * jnp.dot(ypool.astype(BF16), wbp_ref[...], preferred_element_type=F32)
         + gbr[:, D_MODEL:2 * D_MODEL] * jnp.dot(yn_ref[...], wbn_ref[...], preferred_element_type=F32)
         + gbr[:, 2 * D_MODEL:3 * D_MODEL] * jnp.dot(ymem.astype(BF16), wbx_ref[...], preferred_element_type=F32))
    o = jnp.dot(y.astype(BF16), wout_ref[...], preferred_element_type=F32)
    o_ref[...] = x + _rms(o, gpost_ref[...])


def _merge(x2, ynsa, kvm, g, wpool, wqx, wgbr, poolw, pscale, wbp, wbn, wbx, wout, gpost, S, tm=512):
    T = x2.shape[0]
    M = kvm.shape[1]
    row = lambda w: pl.BlockSpec((tm, w), lambda i: (i, 0))
    halo = pl.BlockSpec((POOL_HALO, D_MODEL), lambda i: (jnp.maximum(i * (tm // POOL_HALO) - 1, 0), 0))
    return pl.pallas_call(
        functools.partial(_merge_kernel, S=S),
        grid=(T // tm,),
        in_specs=[row(D_MODEL), halo, row(NSA_WIDTH),
                  pl.BlockSpec((1, M, 2 * XA_WIDTH), lambda i: ((i * tm) // S, 0, 0)),
                  _resident((1, D_MODEL)), _resident(wpool.shape), _resident(wqx.shape), _resident(wgbr.shape),
                  _resident(poolw.shape), _resident(pscale.shape), _resident(wbp.shape), _resident(wbn.shape),
                  _resident(wbx.shape), _resident(wout.shape), _resident((1, D_MODEL))],
        out_specs=row(D_MODEL),
        out_shape=jax.ShapeDtypeStruct((T, D_MODEL), F32),
        compiler_params=pltpu.CompilerParams(dimension_semantics=("parallel",), vmem_limit_bytes=VMEM_LIMIT),
        name="merge",
    )(x2, x2, ynsa, kvm, g, wpool, wqx, wgbr, poolw, pscale, wbp, wbn, wbx, wout, gpost)


FFN_HALO = 8
FFN_CHUNK = 2816


def _ffn_kernel(x_ref, xh_ref, g_ref, wup_ref, cw_ref, cb_ref, wdn_ref, gpost_ref, o_ref, *, S):
    i = pl.program_id(0)
    tm = x_ref.shape[0]
    t0 = (i * tm) % S
    x = x_ref[...]
    g = g_ref[...]
    hh = jnp.where(t0 > 0, _rms(xh_ref[...], g), 0.0)
    he = jnp.concatenate([hh, _rms(x, g)], axis=0).astype(BF16)

    def conv(col0):
        u = jnp.dot(he, wup_ref[:, col0:col0 + FFN_CHUNK], preferred_element_type=F32)
        c = cb_ref[:, col0:col0 + FFN_CHUNK] + cw_ref[CONV_WIDTH - 1:CONV_WIDTH, col0:col0 + FFN_CHUNK] * u
        for k in range(1, CONV_WIDTH):
            tap = cw_ref[CONV_WIDTH - 1 - k:CONV_WIDTH - k, col0:col0 + FFN_CHUNK]
            c = c + tap * pltpu.roll(u, k, 0)
        return c[FFN_HALO:, :]

    f = jnp.zeros((tm, D_MODEL), F32)
    for j in range(D_FF // FFN_CHUNK):
        act = _gelu_tanh(conv(j * FFN_CHUNK)) * conv(D_FF + j * FFN_CHUNK)
        f = f + jnp.dot(act.astype(BF16), wdn_ref[j * FFN_CHUNK:(j + 1) * FFN_CHUNK, :],
                        preferred_element_type=F32)
    o_ref[...] = x + _rms(f, gpost_ref[...])


def _ffn(x2, g, wup, cw, cb, wdn, gpost, S, tm=512):
    T = x2.shape[0]
    row = pl.BlockSpec((tm, D_MODEL), lambda i: (i, 0))
    halo = pl.BlockSpec((FFN_HALO, D_MODEL), lambda i: (jnp.maximum(i * (tm // FFN_HALO) - 1, 0), 0))
    return pl.pallas_call(
        functools.partial(_ffn_kernel, S=S),
        grid=(T // tm,),
        in_specs=[row, halo, _resident((1, D_MODEL)), _resident(wup.shape), _resident(cw.shape),
                  _resident(cb.shape), _resident(wdn.shape), _resident((1, D_MODEL))],
        out_specs=row,
        out_shape=jax.ShapeDtypeStruct((T, D_MODEL), F32),
        compiler_params=pltpu.CompilerParams(dimension_semantics=("parallel",), vmem_limit_bytes=VMEM_LIMIT),
        name="ffn",
    )(x2, x2, g, wup, cw, cb, wdn, gpost)


def _layer(x2, mem, pos2, B, S, pre_mix_g, w_in, pool_w, pool_scale, cmp_pe, cmp_w1, cmp_w2, mem_norm_g,
           w_mem_kv, w_br_pool, w_br_nsa, w_br_xa, w_out, post_mix_g, pre_ffn_g, w_up, conv_w, conv_b,
           w_down, post_ffn_g):
    o = np.cumsum((0,) + IN_SIZES)
    w_pool, w_q, w_kv, w_gn, w_qx, w_gbr = (w_in[:, o[k]:o[k + 1]] for k in range(6))
    w_gn = jnp.pad(w_gn, ((0, 0), (0, LANES - w_gn.shape[1])))
    vec = lambda a: a.reshape(1, -1)
    q, kc, vc, ks, vst, kw, vwt, gn = _in_proj(x2, pos2, vec(pre_mix_g), w_q.astype(BF16), w_kv[:, :128].astype(BF16),
                                               w_kv[:, 128:].astype(BF16), w_gn.astype(BF16))
    b3 = lambda a: a.reshape(B, S, a.shape[-1])
    kcc = _compress(b3(kc), cmp_pe[0], cmp_w1[0], cmp_w2[0])
    vcc = _compress(b3(vc), cmp_pe[1], cmp_w1[1], cmp_w2[1])
    ynsa = _nsa(b3(q), kcc, vcc, b3(ks), vst, b3(kw), vwt, b3(gn))
    kvm = _mem_kv(mem, vec(mem_norm_g), w_mem_kv.astype(BF16))
    x2 = _merge(x2, ynsa.reshape(B * S, NSA_WIDTH), kvm, vec(pre_mix_g), w_pool.astype(BF16), w_qx.astype(BF16),
                w_gbr.astype(BF16), pool_w.astype(BF16), vec(pool_scale), w_br_pool.astype(BF16),
                w_br_nsa.astype(BF16), w_br_xa.astype(BF16), w_out.astype(BF16), vec(post_mix_g), S)
    return _ffn(x2, vec(pre_ffn_g), w_up.astype(BF16), conv_w, vec(conv_b), w_down.astype(BF16),
                vec(post_ffn_g), S)


def kernel(x, mem, positions, pre_mix_g, w_in, pool_w, pool_scale, cmp_pe, cmp_w1, cmp_w2, mem_norm_g, w_mem_kv,
           w_br_pool, w_br_nsa, w_br_xa, w_out, post_mix_g, pre_ffn_g, w_up, conv_w, conv_b, w_down, post_ffn_g):
    B, S, D = x.shape
    x2 = x.reshape(B * S, D)
    pos2 = positions.reshape(B * S, 1)
    for l in range(pre_mix_g.shape[0]):
        x2 = _layer(x2, mem, pos2, B, S, pre_mix_g[l], w_in[l], pool_w[l], pool_scale[l], cmp_pe[l], cmp_w1[l],
                    cmp_w2[l], mem_norm_g[l], w_mem_kv[l], w_br_pool[l], w_br_nsa[l], w_br_xa[l], w_out[l],
                    post_mix_g[l], pre_ffn_g[l], w_up[l], conv_w[l], conv_b[l], w_down[l], post_ffn_g[l])
    return x2.reshape(B, S, D)
```

```python
import functools

import numpy as np
import jax
import jax.numpy as jnp
from jax import lax
from jax.experimental import pallas as pl
from jax.experimental.pallas import tpu as pltpu

F32 = jnp.float32
BF16 = jnp.bfloat16

D_MODEL = 1024
EPS = 1e-6
POOL_WINDOWS = (2, 4, 8, 16)
POOL_GROUP = 128
POOL_WIDTH = POOL_GROUP * len(POOL_WINDOWS)
NSA_HEADS = 16
NSA_KV_GROUPS = 2
HEADS_PER_GROUP = NSA_HEADS // NSA_KV_GROUPS
HEAD_DIM = 64
NSA_WIDTH = NSA_HEADS * HEAD_DIM
NSA_KV_WIDTH = NSA_KV_GROUPS * HEAD_DIM
CMP_BLOCK = 32
CMP_STRIDE = 16
CMP_HIDDEN = 256
SEL_BLOCK = 64
SEL_TOPK = 16
WINDOW = 512
ROPE_THETA = 500000.0
ROT_DIM = HEAD_DIM // 4
XA_HEADS = 4
XA_HEAD_DIM = 128
XA_WIDTH = XA_HEADS * XA_HEAD_DIM
N_BRANCHES = 3
D_FF = 2816
CONV_WIDTH = 3
IN_SIZES = (POOL_WIDTH, NSA_WIDTH, 6 * NSA_KV_WIDTH, 3 * NSA_HEADS, XA_WIDTH, N_BRANCHES * D_MODEL)

LANES = 128
Q_TILE = 128
SEL_KEY_TILE = 512
MASK_NEG = -1e9
SCORE_FLOOR = -1e30
LAZY_MAX_HEADROOM = 64.0
VMEM_LIMIT = 56 * 1024 * 1024
Q_SCALE = HEAD_DIM ** -0.5 * 1.4426950408889634


def _rms(x, g):
    return x * lax.rsqrt(jnp.mean(x * x, axis=-1, keepdims=True) + EPS) * g


def _gelu_tanh(x):
    return 0.5 * x * (1.0 + jnp.tanh(0.7978845608028654 * (x + 0.044715 * (x * x * x))))


def _dot_bf16(x, w):
    return jnp.dot(x.astype(BF16), w, preferred_element_type=F32)


def _dot_nt(a, b, precision=None):
    return lax.dot_general(a, b, (((1,), (1,)), ((), ())), precision=precision,
                           preferred_element_type=F32)


def _resident(shape):
    return pl.BlockSpec(shape, lambda *_: (0,) * len(shape), pipeline_mode=pl.Buffered(1))


def _inproj_kernel(x_ref, pos_ref, g_ref, wq_ref, wkc_ref, wkv_ref, wgn_ref, frq_ref, sg1_ref, sg2_ref,
                   q_ref, kc_ref, vc_ref, ks_ref, vs_ref, kw_ref, vw_ref, gn_ref):
    h = _rms(x_ref[...], g_ref[...])
    hb = h.astype(BF16)
    ang = pos_ref[...].astype(F32) * frq_ref[...]
    c = jnp.cos(ang)
    s = jnp.sin(ang)
    s1 = s * sg1_ref[...]
    s2 = s * sg2_ref[...]

    def rope(v):
        return v * c + pltpu.roll(v, LANES - ROT_DIM // 2, 1) * s1 + pltpu.roll(v, ROT_DIM // 2, 1) * s2

    q = jnp.dot(hb, wq_ref[...], preferred_element_type=F32)
    for j in range(NSA_WIDTH // LANES):
        q_ref[:, j * LANES:(j + 1) * LANES] = rope(q[:, j * LANES:(j + 1) * LANES]) * Q_SCALE
    kc_ref[...] = rope(jnp.dot(hb, wkc_ref[...], preferred_element_type=F32))
    kv = jnp.dot(hb, wkv_ref[...], preferred_element_type=F32)
    vc_ref[...] = kv[:, 0:128]
    ks_ref[...] = rope(kv[:, 128:256]).astype(BF16)
    vs_ref[...] = kv[:, 256:384].T.astype(BF16)
    kw_ref[...] = rope(kv[:, 384:512]).astype(BF16)
    vw_ref[...] = kv[:, 512:640].T.astype(BF16)
    gn_ref[...] = jax.nn.sigmoid(jnp.dot(hb, wgn_ref[...], preferred_element_type=F32))


def _in_proj(x2, pos2, g, wq, wkc, wkv, wgn, tm=512):
    T = x2.shape[0]
    half = ROT_DIM // 2
    d = np.arange(LANES) % HEAD_DIM
    inv_freq = ROPE_THETA ** (-jnp.arange(half, dtype=F32) * (2.0 / ROT_DIM))
    frq = jnp.where(d < ROT_DIM, inv_freq[d % half], 0.0)[None, :]
    sg1 = np.where(d < half, -1.0, 0.0).astype(np.float32)[None, :]
    sg2 = np.where((d >= half) & (d < ROT_DIM), 1.0, 0.0).astype(np.float32)[None, :]
    row = lambda w: pl.BlockSpec((tm, w), lambda i: (i, 0))
    col = pl.BlockSpec((128, tm), lambda i: (0, i))
    out_shape = [jax.ShapeDtypeStruct((T, NSA_WIDTH), F32),
                 jax.ShapeDtypeStruct((T, 128), F32), jax.ShapeDtypeStruct((T, 128), F32),
                 jax.ShapeDtypeStruct((T, 128), BF16), jax.ShapeDtypeStruct((128, T), BF16),
                 jax.ShapeDtypeStruct((T, 128), BF16), jax.ShapeDtypeStruct((128, T), BF16),
                 jax.ShapeDtypeStruct((T, 128), F32)]
    return pl.pallas_call(
        _inproj_kernel,
        grid=(T // tm,),
        in_specs=[row(D_MODEL), row(1), _resident((1, D_MODEL)), _resident(wq.shape), _resident(wkc.shape),
                  _resident(wkv.shape), _resident(wgn.shape), _resident((1, LANES)), _resident((1, LANES)),
                  _resident((1, LANES))],
        out_specs=[row(NSA_WIDTH), row(128), row(128), row(128), col, row(128), col, row(128)],
        out_shape=out_shape,
        compiler_params=pltpu.CompilerParams(dimension_semantics=("parallel",), vmem_limit_bytes=VMEM_LIMIT),
        name="in_proj",
    )(x2, pos2, g, wq, wkc, wkv, wgn, frq, jnp.asarray(sg1), jnp.asarray(sg2))


def _compress_kernel(c_ref, pet_ref, peb_ref, wt_ref, wb_ref, w2_ref, o_ref):
    c = c_ref[0]
    a = _dot_bf16(c + pet_ref[...], wt_ref[...])
    b = _dot_bf16(c + peb_ref[...], wb_ref[...])
    n = c.shape[0]
    hid = a + pltpu.roll(b, n - 1, 0)
    o_ref[0] = _dot_bf16(_gelu_tanh(hid), w2_ref[...])


def _compress(kv, pe, w1, w2):
    B, S, _ = kv.shape
    n = S // CMP_STRIDE
    G = NSA_KV_GROUPS
    half = CMP_BLOCK // 2
    eye = jnp.eye(G, dtype=F32)

    def blockdiag_w1(w):
        w = w.reshape(half, 1, HEAD_DIM, 1, CMP_HIDDEN) * eye.reshape(1, G, 1, G, 1)
        return w.reshape(half * G * HEAD_DIM, G * CMP_HIDDEN)

    wt = blockdiag_w1(w1[:half * HEAD_DIM]).astype(BF16)
    wb = blockdiag_w1(w1[half * HEAD_DIM:]).astype(BF16)
    w2b = ((w2.reshape(1, CMP_HIDDEN, 1, HEAD_DIM) * eye.reshape(G, 1, G, 1))
           .reshape(G * CMP_HIDDEN, G * HEAD_DIM).astype(BF16))
    pet = jnp.broadcast_to(pe[:half, None, :], (half, G, HEAD_DIM)).reshape(1, half * G * HEAD_DIM)
    peb = jnp.broadcast_to(pe[half:, None, :], (half, G, HEAD_DIM)).reshape(1, half * G * HEAD_DIM)
    c = kv.reshape(B, n, CMP_STRIDE * 128)
    return pl.pallas_call(
        _compress_kernel,
        grid=(B,),
        in_specs=[pl.BlockSpec((1, n, CMP_STRIDE * 128), lambda b: (b, 0, 0)),
                  _resident(pet.shape), _resident(peb.shape), _resident(wt.shape), _resident(wb.shape),
                  _resident(w2b.shape)],
        out_specs=pl.BlockSpec((1, n, 128), lambda b: (b, 0, 0)),
        out_shape=jax.ShapeDtypeStruct((B, n, 128), F32),
        compiler_params=pltpu.CompilerParams(dimension_semantics=("parallel",), vmem_limit_bytes=VMEM_LIMIT),
        name="compress",
    )(c, pet, peb, wt, wb, w2b)


def _memkv_kernel(m_ref, g_ref, w_ref, o_ref):
    mn = _rms(m_ref[0], g_ref[...])
    o_ref[0] = jnp.dot(mn.astype(BF16), w_ref[...], preferred_element_type=F32).astype(BF16)


def _mem_kv(mem, g, w):
    B, M, _ = mem.shape
    return pl.pallas_call(
        _memkv_kernel,
        grid=(B,),
        in_specs=[pl.BlockSpec((1, M, D_MODEL), lambda b: (b, 0, 0)), _resident((1, D_MODEL)),
                  _resident(w.shape)],
        out_specs=pl.BlockSpec((1, M, 2 * XA_WIDTH), lambda b: (b, 0, 0)),
        out_shape=jax.ShapeDtypeStruct((B, M, 2 * XA_WIDTH), BF16),
        compiler_params=pltpu.CompilerParams(dimension_semantics=("parallel",), vmem_limit_bytes=VMEM_LIMIT),
        name="mem_kv",
    )(mem, g, w)


def _nsa_kernel(q_ref, kcc_ref, vcc_ref, ks_ref, vst_ref, kw_ref, vwt_ref, gn_ref, ovt_ref, blk_ref, o_ref,
                qat_ref, oct_ref, ost_ref, owt_ref, m_ref, l_ref, ap_ref, pvp_ref, over_ref, imp_ref):
    qt = pl.program_id(1)
    q0 = qt * Q_TILE
    n_cmp = kcc_ref.shape[1]
    head_cols = [slice(h * Q_TILE, (h + 1) * Q_TILE) for h in range(NSA_HEADS)]
    group_cols = [slice(g * HEADS_PER_GROUP * Q_TILE, (g + 1) * HEADS_PER_GROUP * Q_TILE)
                  for g in range(NSA_KV_GROUPS)]
    per_head = lambda a, n: jnp.concatenate([a] * n, axis=1)

    zeros = jnp.zeros((HEAD_DIM, Q_TILE), F32)
    for j in range(NSA_HEADS // 2):
        t = q_ref[0, :, j * LANES:(j + 1) * LANES].T
        for k in range(2):
            h = 2 * j + k
            part = t[k * HEAD_DIM:(k + 1) * HEAD_DIM, :]
            pair = [part, zeros] if h // HEADS_PER_GROUP == 0 else [zeros, part]
            qat_ref[0:LANES, head_cols[h]] = jnp.concatenate(pair, axis=0).astype(BF16)

    def compressed_branch(n_rows):
        crow = lax.broadcasted_iota(jnp.int32, (n_rows, Q_TILE), 0)
        cq = lax.broadcasted_iota(jnp.int32, (n_rows, Q_TILE), 1) + q0
        valid_c = crow * CMP_STRIDE + (CMP_BLOCK - 1) <= cq
        bias_c = per_head(jnp.where(valid_c, 0.0, SCORE_FLOOR), HEADS_PER_GROUP)
        seen_c = per_head(jnp.where(cq[0:1, :] >= CMP_BLOCK - 1, 1.0, 0.0), HEADS_PER_GROUP)
        kcc = kcc_ref[0, 0:n_rows, :].astype(BF16)
        vcc_ones_ov = jnp.concatenate([vcc_ref[0, 0:n_rows, :].T.astype(BF16), jnp.ones((16, n_rows), BF16),
                                       ovt_ref[:, 0:n_rows]], axis=0)
        for g in range(NSA_KV_GROUPS):
            cols = group_cols[g]
            s = jnp.dot(kcc, qat_ref[0:LANES, cols], preferred_element_type=F32) + bias_c
            p = jnp.exp2(s - jnp.max(s, axis=0, keepdims=True)).astype(BF16)
            prod = jnp.dot(vcc_ones_ov, p, preferred_element_type=F32)
            norm = seen_c / jnp.maximum(prod[LANES:LANES + 1], jnp.finfo(F32).tiny)
            oct_ref[:, cols] = prod[0:LANES] * norm
            imp_heads = prod[LANES + 16:2 * LANES + 16] * norm
            imp = imp_heads[:, 0:Q_TILE]
            for k in range(1, HEADS_PER_GROUP):
                imp = imp + imp_heads[:, k * Q_TILE:(k + 1) * Q_TILE]
            imp_ref[g] = imp

    row_steps = list(range(LANES, n_cmp + 1, LANES))
    for k, n_rows in enumerate(row_steps):
        in_step = (qt * (Q_TILE // CMP_STRIDE) + Q_TILE // CMP_STRIDE - 1) // LANES == k
        pl.when(in_step)(functools.partial(compressed_branch, n_rows))

    brow = lax.broadcasted_iota(jnp.int32, (LANES, Q_TILE), 0)
    cur = (lax.broadcasted_iota(jnp.int32, (LANES, Q_TILE), 1) + q0) // SEL_BLOCK
    forced = (brow == 0) | (brow == cur) | (brow == cur - 1)
    brow_f = brow.astype(F32)
    for g in range(NSA_KV_GROUPS):
        imp = imp_ref[g]
        v = jnp.where(brow > cur, -1.0, jnp.where(forced, -2.0, imp))
        for _ in range(SEL_TOPK - 3):
            mx = jnp.max(v, axis=0, keepdims=True)
            first = jnp.min(jnp.where(v == mx, brow_f, float(LANES)), axis=0, keepdims=True)
            v = jnp.where(brow_f == first, -2.0, v)
        neg = jnp.where((v == -2.0) & (brow < 2 * qt), 0.0, MASK_NEG).astype(BF16)
        for h in range(HEADS_PER_GROUP):
            qat_ref[LANES:2 * LANES, head_cols[g * HEADS_PER_GROUP + h]] = neg

    def scores(kt):
        k0 = pl.multiple_of(kt * SEL_KEY_TILE, SEL_KEY_TILE)
        ka = jnp.concatenate([ks_ref[0, pl.ds(k0, SEL_KEY_TILE), :], blk_ref[pl.ds(k0, SEL_KEY_TILE), :]],
                             axis=1)
        return jnp.dot(ka, qat_ref[...], preferred_element_type=F32)

    def values(start, n):
        return jnp.concatenate([vst_ref[:, pl.ds(start, n)], jnp.ones((16, n), BF16)], axis=0)

    def online_update(s, vt):
        m_old = m_ref[...]
        m_new = jnp.maximum(m_old, jnp.max(s, axis=0, keepdims=True))
        alpha = jnp.exp2(m_old - m_new)
        pv = jnp.dot(vt, jnp.exp2(s - m_new).astype(BF16), preferred_element_type=F32)
        ost_ref[...] = alpha * ost_ref[...] + pv[0:LANES]
        l_ref[...] = alpha * l_ref[...] + pv[LANES:LANES + 1]
        m_ref[...] = m_new

    qd = pl.multiple_of(q0, Q_TILE)
    drow = lax.broadcasted_iota(jnp.int32, (Q_TILE, Q_TILE), 0)
    dcol = lax.broadcasted_iota(jnp.int32, (Q_TILE, Q_TILE), 1)
    bias_d = per_head(jnp.where(drow <= dcol, 0.0, SCORE_FLOOR), NSA_HEADS)
    n_tiles = (q0 + SEL_KEY_TILE - 1) // SEL_KEY_TILE

    def start_with_diagonal():
        l_ref[...] = jnp.zeros(l_ref.shape, F32)
        ost_ref[...] = jnp.zeros(ost_ref.shape, F32)
        m_ref[...] = jnp.full(m_ref.shape, SCORE_FLOOR, F32)
        online_update(jnp.dot(ks_ref[0, pl.ds(qd, Q_TILE), :], qat_ref[0:LANES, :],
                              preferred_element_type=F32) + bias_d, values(qd, Q_TILE))

    start_with_diagonal()
    pvp_ref[...] = jnp.zeros(pvp_ref.shape, F32)
    ap_ref[...] = jnp.ones(ap_ref.shape, F32)
    over_ref[...] = jnp.full(over_ref.shape, SCORE_FLOOR, F32)

    def fold_parked():
        alpha = ap_ref[...]
        ost_ref[...] = alpha * (ost_ref[...] + pvp_ref[0:LANES, :])
        l_ref[...] = alpha * (l_ref[...] + pvp_ref[LANES:LANES + 1, :])

    def lazy_tile(kt, carry):
        k0 = pl.multiple_of(kt * SEL_KEY_TILE, SEL_KEY_TILE)
        fold_parked()
        s = scores(kt)
        m_old = m_ref[...]
        pvp_ref[...] = jnp.dot(values(k0, SEL_KEY_TILE), jnp.exp2(s - m_old).astype(BF16),
                               preferred_element_type=F32)
        t_max = jnp.max(s, axis=0, keepdims=True)
        m_new = jnp.maximum(m_old, t_max)
        over_ref[...] = jnp.maximum(over_ref[...], t_max - m_old)
        ap_ref[...] = jnp.exp2(m_old - m_new)
        m_ref[...] = m_new
        return carry

    def lazy_pair(j, carry):
        return lazy_tile(2 * j + 1, lazy_tile(2 * j, carry))

    lax.fori_loop(0, n_tiles // 2, lazy_pair, 0)

    @pl.when(n_tiles % 2 == 1)
    def _():
        lazy_tile(n_tiles - 1, 0)

    fold_parked()

    @pl.when(jnp.max(over_ref[...]) > LAZY_MAX_HEADROOM)
    def _():
        start_with_diagonal()

        def textbook_tile(kt, carry):
            online_update(scores(kt), values(pl.multiple_of(kt * SEL_KEY_TILE, SEL_KEY_TILE), SEL_KEY_TILE))
            return carry

        lax.fori_loop(0, n_tiles, textbook_tile, 0)

    wlen = WINDOW + Q_TILE
    w0 = pl.multiple_of(jnp.maximum(q0 - WINDOW, 0), Q_TILE)
    kwn = kw_ref[0, pl.ds(w0, wlen), :]
    vwnt = jnp.concatenate([vwt_ref[:, pl.ds(w0, wlen)], jnp.ones((16, wlen), BF16)], axis=0)
    kidx = w0 + lax.broadcasted_iota(jnp.int32, (wlen, Q_TILE), 0)
    tqw = q0 + lax.broadcasted_iota(jnp.int32, (wlen, Q_TILE), 1)
    bias_w = per_head(jnp.where((kidx <= tqw) & (tqw - kidx < WINDOW), 0.0, SCORE_FLOOR), NSA_HEADS)
    s = jnp.dot(kwn, qat_ref[0:LANES, :], preferred_element_type=F32) + bias_w
    p = jnp.exp2(s - jnp.max(s, axis=0, keepdims=True)).astype(BF16)
    o = jnp.dot(vwnt, p, preferred_element_type=F32)
    owt_ref[...] = o[0:LANES] / o[LANES:LANES + 1]

    gt = gn_ref[0].T
    for j in range(NSA_HEADS // 2):
        parts = []
        for h in (2 * j, 2 * j + 1):
            g = h // HEADS_PER_GROUP
            rows = slice(g * HEAD_DIM, (g + 1) * HEAD_DIM)
            cols = head_cols[h]
            o_sel = ost_ref[rows, cols] / l_ref[:, cols]
            parts.append(gt[3 * h:3 * h + 1, :] * oct_ref[rows, cols] + gt[3 * h + 1:3 * h + 2, :] * o_sel
                         + gt[3 * h + 2:3 * h + 3, :] * owt_ref[rows, cols])
        o_ref[0, :, j * LANES:(j + 1) * LANES] = jnp.concatenate(parts, axis=0).T.astype(o_ref.dtype)


def _nsa(q, kcc, vcc, ks, vst, kw, vwt, gn):
    B, S, _ = q.shape
    n_cmp = kcc.shape[1]
    n_sel = S // SEL_BLOCK
    assert n_sel <= LANES and S % (2 * SEL_KEY_TILE) == 0 and S >= WINDOW + Q_TILE
    ci = np.arange(n_cmp)[None, :] * CMP_STRIDE
    sj = np.arange(LANES)[:, None] * SEL_BLOCK
    overlap_t = ((ci < sj + SEL_BLOCK) & (ci + CMP_BLOCK > sj)).astype(np.float32)
    block_onehot = (np.arange(S)[:, None] // SEL_BLOCK == np.arange(LANES)[None, :])
    block_onehot = jnp.asarray(block_onehot.astype(np.float32), dtype=BF16)
    cols = NSA_HEADS * Q_TILE
    full = lambda n, w: pl.BlockSpec((1, n, w), lambda b, i: (b, 0, 0))
    full_t = pl.BlockSpec((128, S), lambda b, i: (0, b))
    tile = lambda w: pl.BlockSpec((1, Q_TILE, w), lambda b, i: (b, i, 0))
    return pl.pallas_call(
        _nsa_kernel,
        grid=(B, S // Q_TILE),
        in_specs=[tile(NSA_WIDTH), full(n_cmp, 128), full(n_cmp, 128), full(S, 128), full_t,
                  full(S, 128), full_t, tile(128), _resident(overlap_t.shape), _resident((S, LANES))],
        out_specs=tile(NSA_WIDTH),
        out_shape=jax.ShapeDtypeStruct((B, S, NSA_WIDTH), BF16),
        scratch_shapes=[pltpu.VMEM((2 * LANES, cols), BF16),
                        pltpu.VMEM((LANES, cols), F32), pltpu.VMEM((LANES, cols), F32),
                        pltpu.VMEM((LANES, cols), F32), pltpu.VMEM((1, cols), F32),
                        pltpu.VMEM((1, cols), F32), pltpu.VMEM((1, cols), F32),
                        pltpu.VMEM((LANES + 16, cols), F32), pltpu.VMEM((1, cols), F32),
                        pltpu.VMEM((NSA_KV_GROUPS, LANES, Q_TILE), F32)],
        compiler_params=pltpu.CompilerParams(dimension_semantics=("parallel", "parallel"),
                                             vmem_limit_bytes=VMEM_LIMIT),
        name="nsa",
    )(q, kcc, vcc, ks, vst, kw, vwt, gn, jnp.asarray(overlap_t, dtype=BF16), block_onehot)


POOL_HALO = 16


def _merge_kernel(x_ref, xh_ref, yn_ref, kvm_ref, g_ref, wpool_ref, wqx_ref, wgbr_ref, poolw_ref, pscale_ref,
                  wbp_ref, wbn_ref, wbx_ref, wout_ref, gpost_ref, o_ref, *, S):
    i = pl.program_id(0)
    tm = x_ref.shape[0]
    t0 = (i * tm) % S
    x = x_ref[...]
    g = g_ref[...]
    h = _rms(x, g)
    hb = h.astype(BF16)

    hh = _rms(xh_ref[...], g).astype(BF16)
    u_halo = jnp.dot(hh, wpool_ref[...], preferred_element_type=F32)
    u_halo = jnp.where(t0 > 0, u_halo, 0.0)
    u = jnp.dot(hb, wpool_ref[...], preferred_element_type=F32)
    ue = jnp.concatenate([u_halo, u], axis=0)
    trow = t0 + lax.broadcasted_iota(jnp.int32, (tm, 1), 0) + 1
    ypool = []
    for gi, w in enumerate(POOL_WINDOWS):
        acc = ue[:, gi * POOL_GROUP:(gi + 1) * POOL_GROUP]
        step = 1
        while step < w:
            acc = acc + pltpu.roll(acc, step, 0)
            step *= 2
        cnt = jnp.minimum(trow, w).astype(F32)
        p = acc[POOL_HALO:, :] / cnt - u[:, gi * POOL_GROUP:(gi + 1) * POOL_GROUP]
        ypool.append(jnp.dot(p.astype(BF16), poolw_ref[gi], preferred_element_type=F32))
    ypool = jnp.concatenate(ypool, axis=1) * pscale_ref[...]

    qx = jnp.dot(hb, wqx_ref[...], preferred_element_type=F32)
    ymem = []
    for hd in range(XA_HEADS):
        km = kvm_ref[0, :, hd * XA_HEAD_DIM:(hd + 1) * XA_HEAD_DIM]
        vm = kvm_ref[0, :, XA_WIDTH + hd * XA_HEAD_DIM:XA_WIDTH + (hd + 1) * XA_HEAD_DIM]
        s = _dot_nt(qx[:, hd * XA_HEAD_DIM:(hd + 1) * XA_HEAD_DIM].astype(BF16), km) * (XA_HEAD_DIM ** -0.5)
        e = jnp.exp(s - jnp.max(s, axis=-1, keepdims=True))
        p = e / jnp.sum(e, axis=-1, keepdims=True)
        ymem.append(jnp.dot(p.astype(BF16), vm, preferred_element_type=F32))
    ymem = jnp.concatenate(ymem, axis=1)

    gbr = jax.nn.sigmoid(jnp.dot(hb, wgbr_ref[...], preferred_element_type=F32))
    y = (gbr[:, 0:D_MODEL] * jnp.dot(ypool.astype(BF16), wbp_ref[...], preferred_element_type=F32)
         + gbr[:, D_MODEL:2 * D_MODEL] * jnp.dot(yn_ref[...], wbn_ref[...], preferred_element_type=F32)
         + gbr[:, 2 * D_MODEL:3 * D_MODEL] * jnp.dot(ymem.astype(BF16), wbx_ref[...], preferred_element_type=F32))
    o = jnp.dot(y.astype(BF16), wout_ref[...], preferred_element_type=F32)
    o_ref[...] = x + _rms(o, gpost_ref[...])


def _merge(x2, ynsa, kvm, g, wpool, wqx, wgbr, poolw, pscale, wbp, wbn, wbx, wout, gpost, S, tm=512):
    T = x2.shape[0]
    M = kvm.shape[1]
    row = lambda w: pl.BlockSpec((tm, w), lambda i: (i, 0))
    halo = pl.BlockSpec((POOL_HALO, D_MODEL), lambda i: (jnp.maximum(i * (tm // POOL_HALO) - 1, 0), 0))
    return pl.pallas_call(
        functools.partial(_merge_kernel, S=S),
        grid=(T // tm,),
        in_specs=[row(D_MODEL), halo, row(NSA_WIDTH),
                  pl.BlockSpec((1, M, 2 * XA_WIDTH), lambda i: ((i * tm) // S, 0, 0)),
                  _resident((1, D_MODEL)), _resident(wpool.shape), _resident(wqx.shape), _resident(wgbr.shape),
                  _resident(poolw.shape), _resident(pscale.shape), _resident(wbp.shape), _resident(wbn.shape),
                  _resident(wbx.shape), _resident(wout.shape), _resident((1, D_MODEL))],
        out_specs=row(D_MODEL),
        out_shape=jax.ShapeDtypeStruct((T, D_MODEL), F32),
        compiler_params=pltpu.CompilerParams(dimension_semantics=("parallel",), vmem_limit_bytes=VMEM_LIMIT),
        name="merge",
    )(x2, x2, ynsa, kvm, g, wpool, wqx, wgbr, poolw, pscale, wbp, wbn, wbx, wout, gpost)


FFN_HALO = 8
FFN_CHUNK = 2816


def _ffn_kernel(x_ref, xh_ref, g_ref, wup_ref, cw_ref, cb_ref, wdn_ref, gpost_ref, o_ref, *, S):
    i = pl.program_id(0)
    tm = x_ref.shape[0]
    t0 = (i * tm) % S
    x = x_ref[...]
    g = g_ref[...]
    hh = jnp.where(t0 > 0, _rms(xh_ref[...], g), 0.0)
    he = jnp.concatenate([hh, _rms(x, g)], axis=0).astype(BF16)

    def conv(col0):
        u = jnp.dot(he, wup_ref[:, col0:col0 + FFN_CHUNK], preferred_element_type=F32)
        c = cb_ref[:, col0:col0 + FFN_CHUNK] + cw_ref[CONV_WIDTH - 1:CONV_WIDTH, col0:col0 + FFN_CHUNK] * u
        for k in range(1, CONV_WIDTH):
            tap = cw_ref[CONV_WIDTH - 1 - k:CONV_WIDTH - k, col0:col0 + FFN_CHUNK]
            c = c + tap * pltpu.roll(u, k, 0)
        return c[FFN_HALO:, :]

    f = jnp.zeros((tm, D_MODEL), F32)
    for j in range(D_FF // FFN_CHUNK):
        act = _gelu_tanh(conv(j * FFN_CHUNK)) * conv(D_FF + j * FFN_CHUNK)
        f = f + jnp.dot(act.astype(BF16), wdn_ref[j * FFN_CHUNK:(j + 1) * FFN_CHUNK, :],
                        preferred_element_type=F32)
    o_ref[...] = x + _rms(f, gpost_ref[...])


def _ffn(x2, g, wup, cw, cb, wdn, gpost, S, tm=512):
    T = x2.shape[0]
    row = pl.BlockSpec((tm, D_MODEL), lambda i: (i, 0))
    halo = pl.BlockSpec((FFN_HALO, D_MODEL), lambda i: (jnp.maximum(i * (tm // FFN_HALO) - 1, 0), 0))
    return pl.pallas_call(
        functools.partial(_ffn_kernel, S=S),
        grid=(T // tm,),
        in_specs=[row, halo, _resident((1, D_MODEL)), _resident(wup.shape), _resident(cw.shape),
                  _resident(cb.shape), _resident(wdn.shape), _resident((1, D_MODEL))],
        out_specs=row,
        out_shape=jax.ShapeDtypeStruct((T, D_MODEL), F32),
        compiler_params=pltpu.CompilerParams(dimension_semantics=("parallel",), vmem_limit_bytes=VMEM_LIMIT),
        name="ffn",
    )(x2, x2, g, wup, cw, cb, wdn, gpost)


def _layer(x2, mem, pos2, B, S, pre_mix_g, w_in, pool_w, pool_scale, cmp_pe, cmp_w1, cmp_w2, mem_norm_g,
           w_mem_kv, w_br_pool, w_br_nsa, w_br_xa, w_out, post_mix_g, pre_ffn_g, w_up, conv_w, conv_b,
           w_down, post_ffn_g):
    o = np.cumsum((0,) + IN_SIZES)
    w_pool, w_q, w_kv, w_gn, w_qx, w_gbr = (w_in[:, o[k]:o[k + 1]] for k in range(6))
    w_gn = jnp.pad(w_gn, ((0, 0), (0, LANES - w_gn.shape[1])))
    vec = lambda a: a.reshape(1, -1)
    q, kc, vc, ks, vst, kw, vwt, gn = _in_proj(x2, pos2, vec(pre_mix_g), w_q.astype(BF16), w_kv[:, :128].astype(BF16),
                                               w_kv[:, 128:].astype(BF16), w_gn.astype(BF16))
    b3 = lambda a: a.reshape(B, S, a.shape[-1])
    kcc = _compress(b3(kc), cmp_pe[0], cmp_w1[0], cmp_w2[0])
    vcc = _compress(b3(vc), cmp_pe[1], cmp_w1[1], cmp_w2[1])
    ynsa = _nsa(b3(q), kcc, vcc, b3(ks), vst, b3(kw), vwt, b3(gn))
    kvm = _mem_kv(mem, vec(mem_norm_g), w_mem_kv.astype(BF16))
    x2 = _merge(x2, ynsa.reshape(B * S, NSA_WIDTH), kvm, vec(pre_mix_g), w_pool.astype(BF16), w_qx.astype(BF16),
                w_gbr.astype(BF16), pool_w.astype(BF16), vec(pool_scale), w_br_pool.astype(BF16),
                w_br_nsa.astype(BF16), w_br_xa.astype(BF16), w_out.astype(BF16), vec(post_mix_g), S)
    return _ffn(x2, vec(pre_ffn_g), w_up.astype(BF16), conv_w, vec(conv_b), w_down.astype(BF16),
                vec(post_ffn_g), S)


def kernel(x, mem, positions, pre_mix_g, w_in, pool_w, pool_scale, cmp_pe, cmp_w1, cmp_w2, mem_norm_g, w_mem_kv,
           w_br_pool, w_br_nsa, w_br_xa, w_out, post_mix_g, pre_ffn_g, w_up, conv_w, conv_b, w_down, post_ffn_g):
    B, S, D = x.shape
    x2 = x.reshape(B * S, D)
    pos2 = positions.reshape(B * S, 1)
    for l in range(pre_mix_g.shape[0]):
        x2 = _layer(x2, mem, pos2, B, S, pre_mix_g[l], w_in[l], pool_w[l], pool_scale[l], cmp_pe[l], cmp_w1[l],
                    cmp_w2[l], mem_norm_g[l], w_mem_kv[l], w_br_pool[l], w_br_nsa[l], w_br_xa[l], w_out[l],
                    post_mix_g[l], pre_ffn_g[l], w_up[l], conv_w[l], conv_b[l], w_down[l], post_ffn_g[l])
    return x2.reshape(B, S, D)
```

```python
import functools

import numpy as np
import jax
import jax.numpy as jnp
from jax import lax
from jax.experimental import pallas as pl
from jax.experimental.pallas import tpu as pltpu

F32 = jnp.float32
BF16 = jnp.bfloat16

D_MODEL = 1024
EPS = 1e-6
POOL_WINDOWS = (2, 4, 8, 16)
POOL_GROUP = 128
POOL_WIDTH = POOL_GROUP * len(POOL_WINDOWS)
NSA_HEADS = 16
NSA_KV_GROUPS = 2
HEADS_PER_GROUP = NSA_HEADS // NSA_KV_GROUPS
HEAD_DIM = 64
NSA_WIDTH = NSA_HEADS * HEAD_DIM
NSA_KV_WIDTH = NSA_KV_GROUPS * HEAD_DIM
CMP_BLOCK = 32
CMP_STRIDE = 16
CMP_HIDDEN = 256
SEL_BLOCK = 64
SEL_TOPK = 16
WINDOW = 512
ROPE_THETA = 500000.0
ROT_DIM = HEAD_DIM // 4
XA_HEADS = 4
XA_HEAD_DIM = 128
XA_WIDTH = XA_HEADS * XA_HEAD_DIM
N_BRANCHES = 3
D_FF = 2816
CONV_WIDTH = 3
IN_SIZES = (POOL_WIDTH, NSA_WIDTH, 6 * NSA_KV_WIDTH, 3 * NSA_HEADS, XA_WIDTH, N_BRANCHES * D_MODEL)

LANES = 128
Q_TILE = 128
SEL_KEY_TILE = 512
MASK_NEG = -1e9
SCORE_FLOOR = -1e30
LAZY_MAX_HEADROOM = 64.0
VMEM_LIMIT = 56 * 1024 * 1024
Q_SCALE = HEAD_DIM ** -0.5 * 1.4426950408889634


def _rms(x, g):
    return x * lax.rsqrt(jnp.mean(x * x, axis=-1, keepdims=True) + EPS) * g


def _gelu_tanh(x):
    return 0.5 * x * (1.0 + jnp.tanh(0.7978845608028654 * (x + 0.044715 * (x * x * x))))


def _dot_bf16(x, w):
    return jnp.dot(x.astype(BF16), w, preferred_element_type=F32)


def _dot_nt(a, b, precision=None):
    return lax.dot_general(a, b, (((1,), (1,)), ((), ())), precision=precision,
                           preferred_element_type=F32)


def _resident(shape):
    return pl.BlockSpec(shape, lambda *_: (0,) * len(shape), pipeline_mode=pl.Buffered(1))


def _inproj_kernel(x_ref, pos_ref, g_ref, wq_ref, wkc_ref, wkv_ref, wgn_ref, frq_ref, sg1_ref, sg2_ref,
                   q_ref, kc_ref, vc_ref, ks_ref, vs_ref, kw_ref, vw_ref, gn_ref):
    h = _rms(x_ref[...], g_ref[...])
    hb = h.astype(BF16)
    ang = pos_ref[...].astype(F32) * frq_ref[...]
    c = jnp.cos(ang)
    s = jnp.sin(ang)
    s1 = s * sg1_ref[...]
    s2 = s * sg2_ref[...]

    def rope(v):
        return v * c + pltpu.roll(v, LANES - ROT_DIM // 2, 1) * s1 + pltpu.roll(v, ROT_DIM // 2, 1) * s2

    q = jnp.dot(hb, wq_ref[...], preferred_element_type=F32)
    for j in range(NSA_WIDTH // LANES):
        q_ref[:, j * LANES:(j + 1) * LANES] = rope(q[:, j * LANES:(j + 1) * LANES]) * Q_SCALE
    kc_ref[...] = rope(jnp.dot(hb, wkc_ref[...], preferred_element_type=F32))
    kv = jnp.dot(hb, wkv_ref[...], preferred_element_type=F32)
    vc_ref[...] = kv[:, 0:128]
    ks_ref[...] = rope(kv[:, 128:256]).astype(BF16)
    vs_ref[...] = kv[:, 256:384].T.astype(BF16)
    kw_ref[...] = rope(kv[:, 384:512]).astype(BF16)
    vw_ref[...] = kv[:, 512:640].T.astype(BF16)
    gn_ref[...] = jax.nn.sigmoid(jnp.dot(hb, wgn_ref[...], preferred_element_type=F32))


def _in_proj(x2, pos2, g, wq, wkc, wkv, wgn, tm=512):
    T = x2.shape[0]
    half = ROT_DIM // 2
    d = np.arange(LANES) % HEAD_DIM
    inv_freq = ROPE_THETA ** (-jnp.arange(half, dtype=F32) * (2.0 / ROT_DIM))
    frq = jnp.where(d < ROT_DIM, inv_freq[d % half], 0.0)[None, :]
    sg1 = np.where(d < half, -1.0, 0.0).astype(np.float32)[None, :]
    sg2 = np.where((d >= half) & (d < ROT_DIM), 1.0, 0.0).astype(np.float32)[None, :]
    row = lambda w: pl.BlockSpec((tm, w), lambda i: (i, 0))
    col = pl.BlockSpec((128, tm), lambda i: (0, i))
    out_shape = [jax.ShapeDtypeStruct((T, NSA_WIDTH), F32),
                 jax.ShapeDtypeStruct((T, 128), F32), jax.ShapeDtypeStruct((T, 128), F32),
                 jax.ShapeDtypeStruct((T, 128), BF16), jax.ShapeDtypeStruct((128, T), BF16),
                 jax.ShapeDtypeStruct((T, 128), BF16), jax.ShapeDtypeStruct((128, T), BF16),
                 jax.ShapeDtypeStruct((T, 128), F32)]
    return pl.pallas_call(
        _inproj_kernel,
        grid=(T // tm,),
        in_specs=[row(D_MODEL), row(1), _resident((1, D_MODEL)), _resident(wq.shape), _resident(wkc.shape),
                  _resident(wkv.shape), _resident(wgn.shape), _resident((1, LANES)), _resident((1, LANES)),
                  _resident((1, LANES))],
        out_specs=[row(NSA_WIDTH), row(128), row(128), row(128), col, row(128), col, row(128)],
        out_shape=out_shape,
        compiler_params=pltpu.CompilerParams(dimension_semantics=("parallel",), vmem_limit_bytes=VMEM_LIMIT),
        name="in_proj",
    )(x2, pos2, g, wq, wkc, wkv, wgn, frq, jnp.asarray(sg1), jnp.asarray(sg2))


def _compress_kernel(c_ref, pet_ref, peb_ref, wt_ref, wb_ref, w2_ref, o_ref):
    c = c_ref[0]
    a = _dot_bf16(c + pet_ref[...], wt_ref[...])
    b = _dot_bf16(c + peb_ref[...], wb_ref[...])
    n = c.shape[0]
    hid = a + pltpu.roll(b, n - 1, 0)
    o_ref[0] = _dot_bf16(_gelu_tanh(hid), w2_ref[...])


def _compress(kv, pe, w1, w2):
    B, S, _ = kv.shape
    n = S // CMP_STRIDE
    G = NSA_KV_GROUPS
    half = CMP_BLOCK // 2
    eye = jnp.eye(G, dtype=F32)

    def blockdiag_w1(w):
        w = w.reshape(half, 1, HEAD_DIM, 1, CMP_HIDDEN) * eye.reshape(1, G, 1, G, 1)
        return w.reshape(half * G * HEAD_DIM, G * CMP_HIDDEN)

    wt = blockdiag_w1(w1[:half * HEAD_DIM]).astype(BF16)
    wb = blockdiag_w1(w1[half * HEAD_DIM:]).astype(BF16)
    w2b = ((w2.reshape(1, CMP_HIDDEN, 1, HEAD_DIM) * eye.reshape(G, 1, G, 1))
           .reshape(G * CMP_HIDDEN, G * HEAD_DIM).astype(BF16))
    pet = jnp.broadcast_to(pe[:half, None, :], (half, G, HEAD_DIM)).reshape(1, half * G * HEAD_DIM)
    peb = jnp.broadcast_to(pe[half:, None, :], (half, G, HEAD_DIM)).reshape(1, half * G * HEAD_DIM)
    c = kv.reshape(B, n, CMP_STRIDE * 128)
    return pl.pallas_call(
        _compress_kernel,
        grid=(B,),
        in_specs=[pl.BlockSpec((1, n, CMP_STRIDE * 128), lambda b: (b, 0, 0)),
                  _resident(pet.shape), _resident(peb.shape), _resident(wt.shape), _resident(wb.shape),
                  _resident(w2b.shape)],
        out_specs=pl.BlockSpec((1, n, 128), lambda b: (b, 0, 0)),
        out_shape=jax.ShapeDtypeStruct((B, n, 128), F32),
        compiler_params=pltpu.CompilerParams(dimension_semantics=("parallel",), vmem_limit_bytes=VMEM_LIMIT),
        name="compress",
    )(c, pet, peb, wt, wb, w2b)


def _memkv_kernel(m_ref, g_ref, w_ref, o_ref):
    mn = _rms(m_ref[0], g_ref[...])
    o_ref[0] = jnp.dot(mn.astype(BF16), w_ref[...], preferred_element_type=F32).astype(BF16)


def _mem_kv(mem, g, w):
    B, M, _ = mem.shape
    return pl.pallas_call(
        _memkv_kernel,
        grid=(B,),
        in_specs=[pl.BlockSpec((1, M, D_MODEL), lambda b: (b, 0, 0)), _resident((1, D_MODEL)),
                  _resident(w.shape)],
        out_specs=pl.BlockSpec((1, M, 2 * XA_WIDTH), lambda b: (b, 0, 0)),
        out_shape=jax.ShapeDtypeStruct((B, M, 2 * XA_WIDTH), BF16),
        compiler_params=pltpu.CompilerParams(dimension_semantics=("parallel",), vmem_limit_bytes=VMEM_LIMIT),
        name="mem_kv",
    )(mem, g, w)


def _nsa_kernel(q_ref, kcc_ref, vcc_ref, ks_ref, vst_ref, kw_ref, vwt_ref, gn_ref, ovt_ref, blk_ref, o_ref,
                qat_ref, oct_ref, ost_ref, owt_ref, m_ref, l_ref, ap_ref, pvp_ref, over_ref, imp_ref):
    qt = pl.program_id(1)
    q0 = qt * Q_TILE
    n_cmp = kcc_ref.shape[1]
    head_cols = [slice(h * Q_TILE, (h + 1) * Q_TILE) for h in range(NSA_HEADS)]
    group_cols = [slice(g * HEADS_PER_GROUP * Q_TILE, (g + 1) * HEADS_PER_GROUP * Q_TILE)
                  for g in range(NSA_KV_GROUPS)]
    per_head = lambda a, n: jnp.concatenate([a] * n, axis=1)

    def value_product(vt_ref, start, n, p, extra_rows=None):
        out = []
        for g in range(NSA_KV_GROUPS):
            lhs = [vt_ref[g * HEAD_DIM:(g + 1) * HEAD_DIM, pl.ds(start, n)], jnp.ones((16, n), BF16)]
            if extra_rows is not None:
                lhs.append(extra_rows)
            out.append(jnp.dot(jnp.concatenate(lhs, axis=0), p[:, group_cols[g]], preferred_element_type=F32))
        return jnp.concatenate(out, axis=1)

    zeros = jnp.zeros((HEAD_DIM, Q_TILE), F32)
    for j in range(NSA_HEADS // 2):
        t = q_ref[0, :, j * LANES:(j + 1) * LANES].T
        for k in range(2):
            h = 2 * j + k
            part = t[k * HEAD_DIM:(k + 1) * HEAD_DIM, :]
            pair = [part, zeros] if h // HEADS_PER_GROUP == 0 else [zeros, part]
            qat_ref[0:LANES, head_cols[h]] = jnp.concatenate(pair, axis=0).astype(BF16)

    def compressed_branch(n_rows):
        crow = lax.broadcasted_iota(jnp.int32, (n_rows, Q_TILE), 0)
        cq = lax.broadcasted_iota(jnp.int32, (n_rows, Q_TILE), 1) + q0
        valid_c = crow * CMP_STRIDE + (CMP_BLOCK - 1) <= cq
        bias_c = per_head(jnp.where(valid_c, 0.0, SCORE_FLOOR), HEADS_PER_GROUP)
        seen_c = per_head(jnp.where(cq[0:1, :] >= CMP_BLOCK - 1, 1.0, 0.0), HEADS_PER_GROUP)
        kcc = kcc_ref[0, 0:n_rows, :].astype(BF16)
        vcct = vcc_ref[0, 0:n_rows, :].T.astype(BF16)
        ones_ov = jnp.concatenate([jnp.ones((16, n_rows), BF16), ovt_ref[:, 0:n_rows]], axis=0)
        for g in range(NSA_KV_GROUPS):
            cols = group_cols[g]
            s = jnp.dot(kcc, qat_ref[0:LANES, cols], preferred_element_type=F32) + bias_c
            p = jnp.exp2(s - jnp.max(s, axis=0, keepdims=True)).astype(BF16)
            lhs = jnp.concatenate([vcct[g * HEAD_DIM:(g + 1) * HEAD_DIM], ones_ov], axis=0)
            prod = jnp.dot(lhs, p, preferred_element_type=F32)
            norm = seen_c / jnp.maximum(prod[HEAD_DIM:HEAD_DIM + 1], jnp.finfo(F32).tiny)
            oct_ref[:, cols] = prod[0:HEAD_DIM] * norm
            imp_heads = prod[HEAD_DIM + 16:HEAD_DIM + 16 + LANES] * norm
            imp = imp_heads[:, 0:Q_TILE]
            for k in range(1, HEADS_PER_GROUP):
                imp = imp + imp_heads[:, k * Q_TILE:(k + 1) * Q_TILE]
            imp_ref[g] = imp

    row_steps = list(range(LANES, n_cmp + 1, LANES))
    for k, n_rows in enumerate(row_steps):
        in_step = (qt * (Q_TILE // CMP_STRIDE) + Q_TILE // CMP_STRIDE - 1) // LANES == k
        pl.when(in_step)(functools.partial(compressed_branch, n_rows))

    brow = lax.broadcasted_iota(jnp.int32, (LANES, Q_TILE), 0)
    cur = (lax.broadcasted_iota(jnp.int32, (LANES, Q_TILE), 1) + q0) // SEL_BLOCK
    forced = (brow == 0) | (brow == cur) | (brow == cur - 1)
    brow_f = brow.astype(F32)
    for g in range(NSA_KV_GROUPS):
        imp = imp_ref[g]
        v = jnp.where(brow > cur, -1.0, jnp.where(forced, -2.0, imp))
        for _ in range(SEL_TOPK - 3):
            mx = jnp.max(v, axis=0, keepdims=True)
            first = jnp.min(jnp.where(v == mx, brow_f, float(LANES)), axis=0, keepdims=True)
            v = jnp.where(brow_f == first, -2.0, v)
        neg = jnp.where((v == -2.0) & (brow < 2 * qt), 0.0, MASK_NEG).astype(BF16)
        for h in range(HEADS_PER_GROUP):
            qat_ref[LANES:2 * LANES, head_cols[g * HEADS_PER_GROUP + h]] = neg

    def scores(kt):
        k0 = pl.multiple_of(kt * SEL_KEY_TILE, SEL_KEY_TILE)
        ka = jnp.concatenate([ks_ref[0, pl.ds(k0, SEL_KEY_TILE), :], blk_ref[pl.ds(k0, SEL_KEY_TILE), :]],
                             axis=1)
        return jnp.dot(ka, qat_ref[...], preferred_element_type=F32)

    def online_update(s, start, n):
        m_old = m_ref[...]
        m_new = jnp.maximum(m_old, jnp.max(s, axis=0, keepdims=True))
        alpha = jnp.exp2(m_old - m_new)
        pv = value_product(vst_ref, start, n, jnp.exp2(s - m_new).astype(BF16))
        ost_ref[...] = alpha * ost_ref[...] + pv[0:HEAD_DIM]
        l_ref[...] = alpha * l_ref[...] + pv[HEAD_DIM:HEAD_DIM + 1]
        m_ref[...] = m_new

    qd = pl.multiple_of(q0, Q_TILE)
    drow = lax.broadcasted_iota(jnp.int32, (Q_TILE, Q_TILE), 0)
    dcol = lax.broadcasted_iota(jnp.int32, (Q_TILE, Q_TILE), 1)
    bias_d = per_head(jnp.where(drow <= dcol, 0.0, SCORE_FLOOR), NSA_HEADS)
    n_tiles = (q0 + SEL_KEY_TILE - 1) // SEL_KEY_TILE

    def start_with_diagonal():
        l_ref[...] = jnp.zeros(l_ref.shape, F32)
        ost_ref[...] = jnp.zeros(ost_ref.shape, F32)
        m_ref[...] = jnp.full(m_ref.shape, SCORE_FLOOR, F32)
        online_update(jnp.dot(ks_ref[0, pl.ds(qd, Q_TILE), :], qat_ref[0:LANES, :],
                              preferred_element_type=F32) + bias_d, qd, Q_TILE)

    start_with_diagonal()
    pvp_ref[...] = jnp.zeros(pvp_ref.shape, F32)
    ap_ref[...] = jnp.ones(ap_ref.shape, F32)
    over_ref[...] = jnp.full(over_ref.shape, SCORE_FLOOR, F32)

    def fold_parked():
        alpha = ap_ref[...]
        ost_ref[...] = alpha * (ost_ref[...] + pvp_ref[0:HEAD_DIM, :])
        l_ref[...] = alpha * (l_ref[...] + pvp_ref[HEAD_DIM:HEAD_DIM + 1, :])

    def lazy_tile(kt, carry):
        k0 = pl.multiple_of(kt * SEL_KEY_TILE, SEL_KEY_TILE)
        fold_parked()
        s = scores(kt)
        m_old = m_ref[...]
        pvp_ref[...] = value_product(vst_ref, k0, SEL_KEY_TILE, jnp.exp2(s - m_old).astype(BF16))
        t_max = jnp.max(s, axis=0, keepdims=True)
        m_new = jnp.maximum(m_old, t_max)
        over_ref[...] = jnp.maximum(over_ref[...], t_max - m_old)
        ap_ref[...] = jnp.exp2(m_old - m_new)
        m_ref[...] = m_new
        return carry

    def lazy_pair(j, carry):
        return lazy_tile(2 * j + 1, lazy_tile(2 * j, carry))

    lax.fori_loop(0, n_tiles // 2, lazy_pair, 0)

    @pl.when(n_tiles % 2 == 1)
    def _():
        lazy_tile(n_tiles - 1, 0)

    fold_parked()

    @pl.when(jnp.max(over_ref[...]) > LAZY_MAX_HEADROOM)
    def _():
        start_with_diagonal()

        def textbook_tile(kt, carry):
            online_update(scores(kt), pl.multiple_of(kt * SEL_KEY_TILE, SEL_KEY_TILE), SEL_KEY_TILE)
            return carry

        lax.fori_loop(0, n_tiles, textbook_tile, 0)

    wlen = WINDOW + Q_TILE
    w0 = pl.multiple_of(jnp.maximum(q0 - WINDOW, 0), Q_TILE)
    kwn = kw_ref[0, pl.ds(w0, wlen), :]
    kidx = w0 + lax.broadcasted_iota(jnp.int32, (wlen, Q_TILE), 0)
    tqw = q0 + lax.broadcasted_iota(jnp.int32, (wlen, Q_TILE), 1)
    bias_w = per_head(jnp.where((kidx <= tqw) & (tqw - kidx < WINDOW), 0.0, SCORE_FLOOR), NSA_HEADS)
    s = jnp.dot(kwn, qat_ref[0:LANES, :], preferred_element_type=F32) + bias_w
    p = jnp.exp2(s - jnp.max(s, axis=0, keepdims=True)).astype(BF16)
    o = value_product(vwt_ref, w0, wlen, p)
    owt_ref[...] = o[0:HEAD_DIM] / o[HEAD_DIM:HEAD_DIM + 1]

    gt = gn_ref[0].T
    for j in range(NSA_HEADS // 2):
        parts = []
        for h in (2 * j, 2 * j + 1):
            cols = head_cols[h]
            o_sel = ost_ref[:, cols] / l_ref[:, cols]
            parts.append(gt[3 * h:3 * h + 1, :] * oct_ref[:, cols] + gt[3 * h + 1:3 * h + 2, :] * o_sel
                         + gt[3 * h + 2:3 * h + 3, :] * owt_ref[:, cols])
        o_ref[0, :, j * LANES:(j + 1) * LANES] = jnp.concatenate(parts, axis=0).T.astype(o_ref.dtype)


def _nsa(q, kcc, vcc, ks, vst, kw, vwt, gn):
    B, S, _ = q.shape
    n_cmp = kcc.shape[1]
    n_sel = S // SEL_BLOCK
    assert n_sel <= LANES and S % (2 * SEL_KEY_TILE) == 0 and S >= WINDOW + Q_TILE
    ci = np.arange(n_cmp)[None, :] * CMP_STRIDE
    sj = np.arange(LANES)[:, None] * SEL_BLOCK
    overlap_t = ((ci < sj + SEL_BLOCK) & (ci + CMP_BLOCK > sj)).astype(np.float32)
    block_onehot = (np.arange(S)[:, None] // SEL_BLOCK == np.arange(LANES)[None, :])
    block_onehot = jnp.asarray(block_onehot.astype(np.float32), dtype=BF16)
    cols = NSA_HEADS * Q_TILE
    full = lambda n, w: pl.BlockSpec((1, n, w), lambda b, i: (b, 0, 0))
    full_t = pl.BlockSpec((128, S), lambda b, i: (0, b))
    tile = lambda w: pl.BlockSpec((1, Q_TILE, w), lambda b, i: (b, i, 0))
    return pl.pallas_call(
        _nsa_kernel,
        grid=(B, S // Q_TILE),
        in_specs=[tile(NSA_WIDTH), full(n_cmp, 128), full(n_cmp, 128), full(S, 128), full_t,
                  full(S, 128), full_t, tile(128), _resident(overlap_t.shape), _resident((S, LANES))],
        out_specs=tile(NSA_WIDTH),
        out_shape=jax.ShapeDtypeStruct((B, S, NSA_WIDTH), BF16),
        scratch_shapes=[pltpu.VMEM((2 * LANES, cols), BF16),
                        pltpu.VMEM((HEAD_DIM, cols), F32), pltpu.VMEM((HEAD_DIM, cols), F32),
                        pltpu.VMEM((HEAD_DIM, cols), F32), pltpu.VMEM((1, cols), F32),
                        pltpu.VMEM((1, cols), F32), pltpu.VMEM((1, cols), F32),
                        pltpu.VMEM((HEAD_DIM + 16, cols), F32), pltpu.VMEM((1, cols), F32),
                        pltpu.VMEM((NSA_KV_GROUPS, LANES, Q_TILE), F32)],
        compiler_params=pltpu.CompilerParams(dimension_semantics=("parallel", "parallel"),
                                             vmem_limit_bytes=VMEM_LIMIT),
        name="nsa",
    )(q, kcc, vcc, ks, vst, kw, vwt, gn, jnp.asarray(overlap_t, dtype=BF16), block_onehot)


POOL_HALO = 16


def _merge_kernel(x_ref, xh_ref, yn_ref, kvm_ref, g_ref, wpool_ref, wqx_ref, wgbr_ref, poolw_ref, pscale_ref,
                  wbp_ref, wbn_ref, wbx_ref, wout_ref, gpost_ref, o_ref, *, S):
    i = pl.program_id(0)
    tm = x_ref.shape[0]
    t0 = (i * tm) % S
    x = x_ref[...]
    g = g_ref[...]
    h = _rms(x, g)
    hb = h.astype(BF16)

    hh = _rms(xh_ref[...], g).astype(BF16)
    u_halo = jnp.dot(hh, wpool_ref[...], preferred_element_type=F32)
    u_halo = jnp.where(t0 > 0, u_halo, 0.0)
    u = jnp.dot(hb, wpool_ref[...], preferred_element_type=F32)
    ue = jnp.concatenate([u_halo, u], axis=0)
    trow = t0 + lax.broadcasted_iota(jnp.int32, (tm, 1), 0) + 1
    ypool = []
    for gi, w in enumerate(POOL_WINDOWS):
        acc = ue[:, gi * POOL_GROUP:(gi + 1) * POOL_GROUP]
        step = 1
        while step < w:
            acc = acc + pltpu.roll(acc, step, 0)
            step *= 2
        cnt = jnp.minimum(trow, w).astype(F32)
        p = acc[POOL_HALO:, :] / cnt - u[:, gi * POOL_GROUP:(gi + 1) * POOL_GROUP]
        ypool.append(jnp.dot(p.astype(BF16), poolw_ref[gi], preferred_element_type=F32))
    ypool = jnp.concatenate(ypool, axis=1) * pscale_ref[...]

    qx = jnp.dot(hb, wqx_ref[...], preferred_element_type=F32)
    ymem = []
    for hd in range(XA_HEADS):
        km = kvm_ref[0, :, hd * XA_HEAD_DIM:(hd + 1) * XA_HEAD_DIM]
        vm = kvm_ref[0, :, XA_WIDTH + hd * XA_HEAD_DIM:XA_WIDTH + (hd + 1) * XA_HEAD_DIM]
        s = _dot_nt(qx[:, hd * XA_HEAD_DIM:(hd + 1) * XA_HEAD_DIM].astype(BF16), km) * (XA_HEAD_DIM ** -0.5)
        e = jnp.exp(s - jnp.max(s, axis=-1, keepdims=True))
        p = e / jnp.sum(e, axis=-1, keepdims=True)
        ymem.append(jnp.dot(p.astype(BF16), vm, preferred_element_type=F32))
    ymem = jnp.concatenate(ymem, axis=1)

    gbr = jax.nn.sigmoid(jnp.dot(hb, wgbr_ref[...], preferred_element_type=F32))
    y = (gbr[:, 0:D_MODEL] * jnp.dot(ypool.astype(BF16), wbp_ref[...], preferred_element_type=F32)
         + gbr[:, D_MODEL:2 * D_MODEL] * jnp.dot(yn_ref[...], wbn_ref[...], preferred_element_type=F32)
         + gbr[:, 2 * D_MODEL:3 * D_MODEL] * jnp.dot(ymem.astype(BF16), wbx_ref[...], preferred_element_type=F32))
    o = jnp.dot(y.astype(BF16), wout_ref[...], preferred_element_type=F32)
    o_ref[...] = x + _rms(o, gpost_ref[...])


def _merge(x2, ynsa, kvm, g, wpool, wqx, wgbr, poolw, pscale, wbp, wbn, wbx, wout, gpost, S, tm=512):
    T = x2.shape[0]
    M = kvm.shape[1]
    row = lambda w: pl.BlockSpec((tm, w), lambda i: (i, 0))
    halo = pl.BlockSpec((POOL_HALO, D_MODEL), lambda i: (jnp.maximum(i * (tm // POOL_HALO) - 1, 0), 0))
    return pl.pallas_call(
        functools.partial(_merge_kernel, S=S),
        grid=(T // tm,),
        in_specs=[row(D_MODEL), halo, row(NSA_WIDTH),
                  pl.BlockSpec((1, M, 2 * XA_WIDTH), lambda i: ((i * tm) // S, 0, 0)),
                  _resident((1, D_MODEL)), _resident(wpool.shape), _resident(wqx.shape), _resident(wgbr.shape),
                  _resident(poolw.shape), _resident(pscale.shape), _resident(wbp.shape), _resident(wbn.shape),
                  _resident(wbx.shape), _resident(wout.shape), _resident((1, D_MODEL))],
        out_specs=row(D_MODEL),
        out_shape=jax.ShapeDtypeStruct((T, D_MODEL), F32),
        compiler_params=pltpu.CompilerParams(dimension_semantics=("parallel",), vmem_limit_bytes=VMEM_LIMIT),
        name="merge",
    )(x2, x2, ynsa, kvm, g, wpool, wqx, wgbr, poolw, pscale, wbp, wbn, wbx, wout, gpost)


FFN_HALO = 8
FFN_CHUNK = 2816


def _ffn_kernel(x_ref, xh_ref, g_ref, wup_ref, cw_ref, cb_ref, wdn_ref, gpost_ref, o_ref, *, S):
    i = pl.program_id(0)
    tm = x_ref.shape[0]
    t0 = (i * tm) % S
    x = x_ref[...]
    g = g_ref[...]
    hh = jnp.where(t0 > 0, _rms(xh_ref[...], g), 0.0)
    he = jnp.concatenate([hh, _rms(x, g)], axis=0).astype(BF16)

    def conv(col0):
        u = jnp.dot(he, wup_ref[:, col0:col0 + FFN_CHUNK], preferred_element_type=F32)
        c = cb_ref[:, col0:col0 + FFN_CHUNK] + cw_ref[CONV_WIDTH - 1:CONV_WIDTH, col0:col0 + FFN_CHUNK] * u
        for k in range(1, CONV_WIDTH):
            tap = cw_ref[CONV_WIDTH - 1 - k:CONV_WIDTH - k, col0:col0 + FFN_CHUNK]
            c = c + tap * pltpu.roll(u, k, 0)
        return c[FFN_HALO:, :]

    f = jnp.zeros((tm, D_MODEL), F32)
    for j in range(D_FF // FFN_CHUNK):
        act = _gelu_tanh(conv(j * FFN_CHUNK)) * conv(D_FF + j * FFN_CHUNK)
        f = f + jnp.dot(act.astype(BF16), wdn_ref[j * FFN_CHUNK:(j + 1) * FFN_CHUNK, :],
                        preferred_element_type=F32)
    o_ref[...] = x + _rms(f, gpost_ref[...])


def _ffn(x2, g, wup, cw, cb, wdn, gpost, S, tm=512):
    T = x2.shape[0]
    row = pl.BlockSpec((tm, D_MODEL), lambda i: (i, 0))
    halo = pl.BlockSpec((FFN_HALO, D_MODEL), lambda i: (jnp.maximum(i * (tm // FFN_HALO) - 1, 0), 0))
    return pl.pallas_call(
        functools.partial(_ffn_kernel, S=S),
        grid=(T // tm,),
        in_specs=[row, halo, _resident((1, D_MODEL)), _resident(wup.shape), _resident(cw.shape),
                  _resident(cb.shape), _resident(wdn.shape), _resident((1, D_MODEL))],
        out_specs=row,
        out_shape=jax.ShapeDtypeStruct((T, D_MODEL), F32),
        compiler_params=pltpu.CompilerParams(dimension_semantics=("parallel",), vmem_limit_bytes=VMEM_LIMIT),
        name="ffn",
    )(x2, x2, g, wup, cw, cb, wdn, gpost)


def _layer(x2, mem, pos2, B, S, pre_mix_g, w_in, pool_w, pool_scale, cmp_pe, cmp_w1, cmp_w2, mem_norm_g,
           w_mem_kv, w_br_pool, w_br_nsa, w_br_xa, w_out, post_mix_g, pre_ffn_g, w_up, conv_w, conv_b,
           w_down, post_ffn_g):
    o = np.cumsum((0,) + IN_SIZES)
    w_pool, w_q, w_kv, w_gn, w_qx, w_gbr = (w_in[:, o[k]:o[k + 1]] for k in range(6))
    w_gn = jnp.pad(w_gn, ((0, 0), (0, LANES - w_gn.shape[1])))
    vec = lambda a: a.reshape(1, -1)
    q, kc, vc, ks, vst, kw, vwt, gn = _in_proj(x2, pos2, vec(pre_mix_g), w_q.astype(BF16), w_kv[:, :128].astype(BF16),
                                               w_kv[:, 128:].astype(BF16), w_gn.astype(BF16))
    b3 = lambda a: a.reshape(B, S, a.shape[-1])
    kcc = _compress(b3(kc), cmp_pe[0], cmp_w1[0], cmp_w2[0])
    vcc = _compress(b3(vc), cmp_pe[1], cmp_w1[1], cmp_w2[1])
    ynsa = _nsa(b3(q), kcc, vcc, b3(ks), vst, b3(kw), vwt, b3(gn))
    kvm = _mem_kv(mem, vec(mem_norm_g), w_mem_kv.astype(BF16))
    x2 = _merge(x2, ynsa.reshape(B * S, NSA_WIDTH), kvm, vec(pre_mix_g), w_pool.astype(BF16), w_qx.astype(BF16),
                w_gbr.astype(BF16), pool_w.astype(BF16), vec(pool_scale), w_br_pool.astype(BF16),
                w_br_nsa.astype(BF16), w_br_xa.astype(BF16), w_out.astype(BF16), vec(post_mix_g), S)
    return _ffn(x2, vec(pre_ffn_g), w_up.astype(BF16), conv_w, vec(conv_b), w_down.astype(BF16),
                vec(post_ffn_g), S)


def kernel(x, mem, positions, pre_mix_g, w_in, pool_w, pool_scale, cmp_pe, cmp_w1, cmp_w2, mem_norm_g, w_mem_kv,
           w_br_pool, w_br_nsa, w_br_xa, w_out, post_mix_g, pre_ffn_g, w_up, conv_w, conv_b, w_down, post_ffn_g):
    B, S, D = x.shape
    x2 = x.reshape(B * S, D)
    pos2 = positions.reshape(B * S, 1)
    for l in range(pre_mix_g.shape[0]):
        x2 = _layer(x2, mem, pos2, B, S, pre_mix_g[l], w_in[l], pool_w[l], pool_scale[l], cmp_pe[l], cmp_w1[l],
                    cmp_w2[l], mem_norm_g[l], w_mem_kv[l], w_br_pool[l], w_br_nsa[l], w_br_xa[l], w_out[l],
                    post_mix_g[l], pre_ffn_g[l], w_up[l], conv_w[l], conv_b[l], w_down[l], post_ffn_g[l])
    return x2.reshape(B, S, D)
```

```python
import functools

import numpy as np
import jax
import jax.numpy as jnp
from jax import lax
from jax.experimental import pallas as pl
from jax.experimental.pallas import tpu as pltpu

F32 = jnp.float32
BF16 = jnp.bfloat16

D_MODEL = 1024
EPS = 1e-6
POOL_WINDOWS = (2, 4, 8, 16)
POOL_GROUP = 128
POOL_WIDTH = POOL_GROUP * len(POOL_WINDOWS)
NSA_HEADS = 16
NSA_KV_GROUPS = 2
HEADS_PER_GROUP = NSA_HEADS // NSA_KV_GROUPS
HEAD_DIM = 64
NSA_WIDTH = NSA_HEADS * HEAD_DIM
NSA_KV_WIDTH = NSA_KV_GROUPS * HEAD_DIM
CMP_BLOCK = 32
CMP_STRIDE = 16
CMP_HIDDEN = 256
SEL_BLOCK = 64
SEL_TOPK = 16
WINDOW = 512
ROPE_THETA = 500000.0
ROT_DIM = HEAD_DIM // 4
XA_HEADS = 4
XA_HEAD_DIM = 128
XA_WIDTH = XA_HEADS * XA_HEAD_DIM
N_BRANCHES = 3
D_FF = 2816
CONV_WIDTH = 3
IN_SIZES = (POOL_WIDTH, NSA_WIDTH, 6 * NSA_KV_WIDTH, 3 * NSA_HEADS, XA_WIDTH, N_BRANCHES * D_MODEL)

LANES = 128
Q_TILE = 128
SEL_KEY_TILE = 512
MASK_NEG = -1e9
SCORE_FLOOR = -1e30
LAZY_MAX_HEADROOM = 64.0
VMEM_LIMIT = 56 * 1024 * 1024
Q_SCALE = HEAD_DIM ** -0.5 * 1.4426950408889634


def _rms(x, g):
    return x * lax.rsqrt(jnp.mean(x * x, axis=-1, keepdims=True) + EPS) * g


def _gelu_tanh(x):
    return 0.5 * x * (1.0 + jnp.tanh(0.7978845608028654 * (x + 0.044715 * (x * x * x))))


def _dot_bf16(x, w):
    return jnp.dot(x.astype(BF16), w, preferred_element_type=F32)


def _dot_nt(a, b):
    return lax.dot_general(a, b, (((1,), (1,)), ((), ())), preferred_element_type=F32)


def _resident(shape):
    return pl.BlockSpec(shape, lambda *_: (0,) * len(shape), pipeline_mode=pl.Buffered(1))


def _inproj_kernel(x_ref, pos_ref, g_ref, wq_ref, wkc_ref, wkv_ref, wgn_ref, frq_ref, sg1_ref, sg2_ref,
                   q_ref, kc_ref, vc_ref, ks_ref, vs_ref, kw_ref, vw_ref, gn_ref):
    h = _rms(x_ref[...], g_ref[...])
    hb = h.astype(BF16)
    ang = pos_ref[...].astype(F32) * frq_ref[...]
    c = jnp.cos(ang)
    s = jnp.sin(ang)
    s1 = s * sg1_ref[...]
    s2 = s * sg2_ref[...]

    def rope(v):
        return v * c + pltpu.roll(v, LANES - ROT_DIM // 2, 1) * s1 + pltpu.roll(v, ROT_DIM // 2, 1) * s2

    q = jnp.dot(hb, wq_ref[...], preferred_element_type=F32)
    for j in range(NSA_WIDTH // LANES):
        q_ref[:, j * LANES:(j + 1) * LANES] = rope(q[:, j * LANES:(j + 1) * LANES]) * Q_SCALE
    kc_ref[...] = rope(jnp.dot(hb, wkc_ref[...], preferred_element_type=F32)).astype(BF16)
    kv = jnp.dot(hb, wkv_ref[...], preferred_element_type=F32)
    vc_ref[...] = kv[:, 0:128].astype(BF16)
    ks_ref[...] = rope(kv[:, 128:256]).astype(BF16)
    vs_ref[...] = kv[:, 256:384].T.astype(BF16)
    kw_ref[...] = rope(kv[:, 384:512]).astype(BF16)
    vw_ref[...] = kv[:, 512:640].T.astype(BF16)
    gn_ref[...] = jax.nn.sigmoid(jnp.dot(hb, wgn_ref[...], preferred_element_type=F32))


def _in_proj(x2, pos2, g, wq, wkc, wkv, wgn, tm=512):
    T = x2.shape[0]
    half = ROT_DIM // 2
    d = np.arange(LANES) % HEAD_DIM
    inv_freq = ROPE_THETA ** (-jnp.arange(half, dtype=F32) * (2.0 / ROT_DIM))
    frq = jnp.where(d < ROT_DIM, inv_freq[d % half], 0.0)[None, :]
    sg1 = np.where(d < half, -1.0, 0.0).astype(np.float32)[None, :]
    sg2 = np.where((d >= half) & (d < ROT_DIM), 1.0, 0.0).astype(np.float32)[None, :]
    row = lambda w: pl.BlockSpec((tm, w), lambda i: (i, 0))
    col = pl.BlockSpec((128, tm), lambda i: (0, i))
    out_shape = [jax.ShapeDtypeStruct((T, NSA_WIDTH), F32),
                 jax.ShapeDtypeStruct((T, 128), BF16), jax.ShapeDtypeStruct((T, 128), BF16),
                 jax.ShapeDtypeStruct((T, 128), BF16), jax.ShapeDtypeStruct((128, T), BF16),
                 jax.ShapeDtypeStruct((T, 128), BF16), jax.ShapeDtypeStruct((128, T), BF16),
                 jax.ShapeDtypeStruct((T, 128), F32)]
    return pl.pallas_call(
        _inproj_kernel,
        grid=(T // tm,),
        in_specs=[row(D_MODEL), row(LANES), _resident((1, D_MODEL)), _resident(wq.shape), _resident(wkc.shape),
                  _resident(wkv.shape), _resident(wgn.shape), _resident((1, LANES)), _resident((1, LANES)),
                  _resident((1, LANES))],
        out_specs=[row(NSA_WIDTH), row(128), row(128), row(128), col, row(128), col, row(128)],
        out_shape=out_shape,
        compiler_params=pltpu.CompilerParams(dimension_semantics=("parallel",), vmem_limit_bytes=VMEM_LIMIT),
        name="in_proj",
    )(x2, pos2, g, wq, wkc, wkv, wgn, frq, jnp.asarray(sg1), jnp.asarray(sg2))


def _compress_kernel(c_ref, pet_ref, peb_ref, wt_ref, wb_ref, w2_ref, o_ref):
    c = c_ref[0].astype(F32)
    a = _dot_bf16(c + pet_ref[...], wt_ref[...])
    b = _dot_bf16(c + peb_ref[...], wb_ref[...])
    n = c.shape[0]
    hid = a + pltpu.roll(b, n - 1, 0)
    o_ref[0] = _dot_bf16(_gelu_tanh(hid), w2_ref[...])


def _compress(kv, pe, w1, w2):
    B, S, _ = kv.shape
    n = S // CMP_STRIDE
    G = NSA_KV_GROUPS
    half = CMP_BLOCK // 2
    eye = jnp.eye(G, dtype=F32)

    def blockdiag_w1(w):
        w = w.reshape(half, 1, HEAD_DIM, 1, CMP_HIDDEN) * eye.reshape(1, G, 1, G, 1)
        return w.reshape(half * G * HEAD_DIM, G * CMP_HIDDEN)

    wt = blockdiag_w1(w1[:half * HEAD_DIM]).astype(BF16)
    wb = blockdiag_w1(w1[half * HEAD_DIM:]).astype(BF16)
    w2b = ((w2.reshape(1, CMP_HIDDEN, 1, HEAD_DIM) * eye.reshape(G, 1, G, 1))
           .reshape(G * CMP_HIDDEN, G * HEAD_DIM).astype(BF16))
    pet = jnp.broadcast_to(pe[:half, None, :], (half, G, HEAD_DIM)).reshape(1, half * G * HEAD_DIM)
    peb = jnp.broadcast_to(pe[half:, None, :], (half, G, HEAD_DIM)).reshape(1, half * G * HEAD_DIM)
    c = kv.reshape(B, n, CMP_STRIDE * 128)
    return pl.pallas_call(
        _compress_kernel,
        grid=(B,),
        in_specs=[pl.BlockSpec((1, n, CMP_STRIDE * 128), lambda b: (b, 0, 0)),
                  _resident(pet.shape), _resident(peb.shape), _resident(wt.shape), _resident(wb.shape),
                  _resident(w2b.shape)],
        out_specs=pl.BlockSpec((1, n, 128), lambda b: (b, 0, 0)),
        out_shape=jax.ShapeDtypeStruct((B, n, 128), F32),
        compiler_params=pltpu.CompilerParams(dimension_semantics=("parallel",), vmem_limit_bytes=VMEM_LIMIT),
        name="compress",
    )(c, pet, peb, wt, wb, w2b)


def _memkv_kernel(m_ref, g_ref, w_ref, o_ref):
    mn = _rms(m_ref[0], g_ref[...])
    o_ref[0] = jnp.dot(mn.astype(BF16), w_ref[...], preferred_element_type=F32).astype(BF16)


def _mem_kv(mem, g, w):
    B, M, _ = mem.shape
    return pl.pallas_call(
        _memkv_kernel,
        grid=(B,),
        in_specs=[pl.BlockSpec((1, M, D_MODEL), lambda b: (b, 0, 0)), _resident((1, D_MODEL)),
                  _resident(w.shape)],
        out_specs=pl.BlockSpec((1, M, 2 * XA_WIDTH), lambda b: (b, 0, 0)),
        out_shape=jax.ShapeDtypeStruct((B, M, 2 * XA_WIDTH), BF16),
        compiler_params=pltpu.CompilerParams(dimension_semantics=("parallel",), vmem_limit_bytes=VMEM_LIMIT),
        name="mem_kv",
    )(mem, g, w)


def _nsa_kernel(q_ref, kcc_ref, vcc_ref, ks_ref, vst_ref, kw_ref, vwt_ref, gn_ref, ovt_ref, blk_ref, o_ref,
                qat_ref, oct_ref, ost_ref, owt_ref, m_ref, l_ref, ap_ref, pvp_ref, over_ref, imp_ref):
    qt = pl.program_id(1)
    q0 = qt * Q_TILE
    n_cmp = kcc_ref.shape[1]
    head_cols = [slice(h * Q_TILE, (h + 1) * Q_TILE) for h in range(NSA_HEADS)]
    group_cols = [slice(g * HEADS_PER_GROUP * Q_TILE, (g + 1) * HEADS_PER_GROUP * Q_TILE)
                  for g in range(NSA_KV_GROUPS)]
    per_head = lambda a, n: jnp.concatenate([a] * n, axis=1)

    def masked(s, bias):
        return jnp.concatenate([s[:, c:c + Q_TILE] + bias for c in range(0, s.shape[1], Q_TILE)], axis=1)

    def value_product(vt_ref, start, n, p, extra_rows=None):
        out = []
        for g in range(NSA_KV_GROUPS):
            lhs = [vt_ref[g * HEAD_DIM:(g + 1) * HEAD_DIM, pl.ds(start, n)], jnp.ones((16, n), BF16)]
            if extra_rows is not None:
                lhs.append(extra_rows)
            out.append(jnp.dot(jnp.concatenate(lhs, axis=0), p[:, group_cols[g]], preferred_element_type=F32))
        return jnp.concatenate(out, axis=1)

    zeros = jnp.zeros((HEAD_DIM, Q_TILE), F32)
    for j in range(NSA_HEADS // 2):
        t = q_ref[0, :, j * LANES:(j + 1) * LANES].T
        for k in range(2):
            h = 2 * j + k
            part = t[k * HEAD_DIM:(k + 1) * HEAD_DIM, :]
            pair = [part, zeros] if h // HEADS_PER_GROUP == 0 else [zeros, part]
            qat_ref[0:LANES, head_cols[h]] = jnp.concatenate(pair, axis=0).astype(BF16)

    def compressed_branch(n_rows):
        crow = lax.broadcasted_iota(jnp.int32, (n_rows, Q_TILE), 0)
        cq = lax.broadcasted_iota(jnp.int32, (n_rows, Q_TILE), 1) + q0
        valid_c = crow * CMP_STRIDE + (CMP_BLOCK - 1) <= cq
        bias_c = jnp.where(valid_c, 0.0, SCORE_FLOOR)
        seen_c = per_head(jnp.where(cq[0:1, :] >= CMP_BLOCK - 1, 1.0, 0.0), HEADS_PER_GROUP)
        kcc = kcc_ref[0, 0:n_rows, :].astype(BF16)
        vcct = vcc_ref[0, 0:n_rows, :].T.astype(BF16)
        ones_ov = jnp.concatenate([jnp.ones((16, n_rows), BF16), ovt_ref[:, 0:n_rows]], axis=0)
        for g in range(NSA_KV_GROUPS):
            cols = group_cols[g]
            s = masked(jnp.dot(kcc, qat_ref[0:LANES, cols], preferred_element_type=F32), bias_c)
            p = jnp.exp2(s - jnp.max(s, axis=0, keepdims=True)).astype(BF16)
            lhs = jnp.concatenate([vcct[g * HEAD_DIM:(g + 1) * HEAD_DIM], ones_ov], axis=0)
            prod = jnp.dot(lhs, p, preferred_element_type=F32)
            norm = seen_c / jnp.maximum(prod[HEAD_DIM:HEAD_DIM + 1], jnp.finfo(F32).tiny)
            oct_ref[:, cols] = prod[0:HEAD_DIM] * norm
            imp_heads = prod[HEAD_DIM + 16:HEAD_DIM + 16 + LANES] * norm
            imp = imp_heads[:, 0:Q_TILE]
            for k in range(1, HEADS_PER_GROUP):
                imp = imp + imp_heads[:, k * Q_TILE:(k + 1) * Q_TILE]
            imp_ref[g] = imp

    row_steps = list(range(LANES, n_cmp + 1, LANES))
    for k, n_rows in enumerate(row_steps):
        in_step = (qt * (Q_TILE // CMP_STRIDE) + Q_TILE // CMP_STRIDE - 1) // LANES == k
        pl.when(in_step)(functools.partial(compressed_branch, n_rows))

    brow = lax.broadcasted_iota(jnp.int32, (LANES, Q_TILE), 0)
    cur = (lax.broadcasted_iota(jnp.int32, (LANES, Q_TILE), 1) + q0) // SEL_BLOCK
    forced = (brow == 0) | (brow == cur) | (brow == cur - 1)
    brow_f = brow.astype(F32)
    for g in range(NSA_KV_GROUPS):
        imp = imp_ref[g]
        v = jnp.where(brow > cur, -1.0, jnp.where(forced, -2.0, imp))
        for _ in range(SEL_TOPK - 3):
            mx = jnp.max(v, axis=0, keepdims=True)
            first = jnp.min(jnp.where(v == mx, brow_f, float(LANES)), axis=0, keepdims=True)
            v = jnp.where(brow_f == first, -2.0, v)
        neg = jnp.where(v == -2.0, 0.0, MASK_NEG).astype(BF16)
        for h in range(HEADS_PER_GROUP):
            qat_ref[LANES:2 * LANES, head_cols[g * HEADS_PER_GROUP + h]] = neg

    def scores(start, n):
        ka = jnp.concatenate([ks_ref[0, pl.ds(start, n), :], blk_ref[pl.ds(start, n), :]], axis=1)
        return jnp.dot(ka, qat_ref[...], preferred_element_type=F32)

    def online_update(s, start, n):
        m_old = m_ref[...]
        m_new = jnp.maximum(m_old, jnp.max(s, axis=0, keepdims=True))
        alpha = jnp.exp2(m_old - m_new)
        pv = value_product(vst_ref, start, n, jnp.exp2(s - m_new).astype(BF16))
        ost_ref[...] = alpha * ost_ref[...] + pv[0:HEAD_DIM]
        l_ref[...] = alpha * l_ref[...] + pv[HEAD_DIM:HEAD_DIM + 1]
        m_ref[...] = m_new

    qd = pl.multiple_of(q0, Q_TILE)
    drow = lax.broadcasted_iota(jnp.int32, (Q_TILE, Q_TILE), 0)
    dcol = lax.broadcasted_iota(jnp.int32, (Q_TILE, Q_TILE), 1)
    bias_d = jnp.where(drow <= dcol, 0.0, SCORE_FLOOR)
    head = q0 % SEL_KEY_TILE
    n_full = q0 // SEL_KEY_TILE
    head_sizes = range(Q_TILE, SEL_KEY_TILE, Q_TILE)
    tile_start = lambda j: pl.multiple_of(head + j * SEL_KEY_TILE, Q_TILE)

    def start_with_diagonal():
        l_ref[...] = jnp.zeros(l_ref.shape, F32)
        ost_ref[...] = jnp.zeros(ost_ref.shape, F32)
        m_ref[...] = jnp.full(m_ref.shape, SCORE_FLOOR, F32)
        online_update(masked(jnp.dot(ks_ref[0, pl.ds(qd, Q_TILE), :], qat_ref[0:LANES, :],
                                     preferred_element_type=F32), bias_d), qd, Q_TILE)

    start_with_diagonal()
    pvp_ref[...] = jnp.zeros(pvp_ref.shape, F32)
    ap_ref[...] = jnp.ones(ap_ref.shape, F32)
    over_ref[...] = jnp.full(over_ref.shape, SCORE_FLOOR, F32)

    def fold_parked():
        alpha = ap_ref[...]
        ost_ref[...] = alpha * (ost_ref[...] + pvp_ref[0:HEAD_DIM, :])
        l_ref[...] = alpha * (l_ref[...] + pvp_ref[HEAD_DIM:HEAD_DIM + 1, :])

    def lazy_step(start, n):
        fold_parked()
        s = scores(start, n)
        m_old = m_ref[...]
        pvp_ref[...] = value_product(vst_ref, start, n, jnp.exp2(s - m_old).astype(BF16))
        t_max = jnp.max(s, axis=0, keepdims=True)
        m_new = jnp.maximum(m_old, t_max)
        over_ref[...] = jnp.maximum(over_ref[...], t_max - m_old)
        ap_ref[...] = jnp.exp2(m_old - m_new)
        m_ref[...] = m_new

    def lazy_pair(j, carry):
        lazy_step(tile_start(2 * j), SEL_KEY_TILE)
        lazy_step(tile_start(2 * j + 1), SEL_KEY_TILE)
        return carry

    lax.fori_loop(0, n_full // 2, lazy_pair, 0)
    pl.when(n_full % 2 == 1)(lambda: lazy_step(tile_start(n_full - 1), SEL_KEY_TILE))
    for n in head_sizes:
        pl.when(head == n)(functools.partial(lazy_step, 0, n))
    fold_parked()

    @pl.when(jnp.max(over_ref[...]) > LAZY_MAX_HEADROOM)
    def _():
        start_with_diagonal()

        def textbook_tile(j, carry):
            online_update(scores(tile_start(j), SEL_KEY_TILE), tile_start(j), SEL_KEY_TILE)
            return carry

        lax.fori_loop(0, n_full, textbook_tile, 0)
        for n in head_sizes:
            pl.when(head == n)(lambda n=n: online_update(scores(0, n), 0, n))

    wlen = WINDOW + Q_TILE
    w0 = pl.multiple_of(jnp.maximum(q0 - WINDOW, 0), Q_TILE)
    kidx = w0 + lax.broadcasted_iota(jnp.int32, (wlen, Q_TILE), 0)
    tqw = q0 + lax.broadcasted_iota(jnp.int32, (wlen, Q_TILE), 1)
    bias_w = jnp.where((kidx <= tqw) & (tqw - kidx < WINDOW), 0.0, SCORE_FLOOR)
    s = masked(jnp.dot(kw_ref[0, pl.ds(w0, wlen), :], qat_ref[0:LANES, :], preferred_element_type=F32), bias_w)
    o = value_product(vwt_ref, w0, wlen, jnp.exp2(s - jnp.max(s, axis=0, keepdims=True)).astype(BF16))
    owt_ref[...] = o[0:HEAD_DIM] / o[HEAD_DIM:HEAD_DIM + 1]

    gt = gn_ref[0].T
    for j in range(NSA_HEADS // 2):
        parts = []
        for h in (2 * j, 2 * j + 1):
            cols = head_cols[h]
            o_sel = ost_ref[:, cols] / l_ref[:, cols]
            parts.append(gt[3 * h:3 * h + 1, :] * oct_ref[:, cols] + gt[3 * h + 1:3 * h + 2, :] * o_sel
                         + gt[3 * h + 2:3 * h + 3, :] * owt_ref[:, cols])
        o_ref[0, :, j * LANES:(j + 1) * LANES] = jnp.concatenate(parts, axis=0).T.astype(o_ref.dtype)


def _nsa(q, kcc, vcc, ks, vst, kw, vwt, gn):
    B, S, _ = q.shape
    n_cmp = kcc.shape[1]
    n_sel = S // SEL_BLOCK
    assert n_sel <= LANES and n_cmp % LANES == 0 and S % SEL_KEY_TILE == 0 and S >= WINDOW + Q_TILE
    ci = np.arange(n_cmp)[None, :] * CMP_STRIDE
    sj = np.arange(LANES)[:, None] * SEL_BLOCK
    overlap_t = ((ci < sj + SEL_BLOCK) & (ci + CMP_BLOCK > sj)).astype(np.float32)
    block_onehot = (np.arange(S)[:, None] // SEL_BLOCK == np.arange(LANES)[None, :])
    block_onehot = jnp.asarray(block_onehot.astype(np.float32), dtype=BF16)
    cols = NSA_HEADS * Q_TILE
    full = lambda n, w: pl.BlockSpec((1, n, w), lambda b, i: (b, 0, 0))
    full_t = pl.BlockSpec((128, S), lambda b, i: (0, b))
    tile = lambda w: pl.BlockSpec((1, Q_TILE, w), lambda b, i: (b, i, 0))
    return pl.pallas_call(
        _nsa_kernel,
        grid=(B, S // Q_TILE),
        in_specs=[tile(NSA_WIDTH), full(n_cmp, 128), full(n_cmp, 128), full(S, 128), full_t,
                  full(S, 128), full_t, tile(128), _resident(overlap_t.shape), _resident((S, LANES))],
        out_specs=tile(NSA_WIDTH),
        out_shape=jax.ShapeDtypeStruct((B, S, NSA_WIDTH), BF16),
        scratch_shapes=[pltpu.VMEM((2 * LANES, cols), BF16),
                        pltpu.VMEM((HEAD_DIM, cols), F32), pltpu.VMEM((HEAD_DIM, cols), F32),
                        pltpu.VMEM((HEAD_DIM, cols), F32), pltpu.VMEM((1, cols), F32),
                        pltpu.VMEM((1, cols), F32), pltpu.VMEM((1, cols), F32),
                        pltpu.VMEM((HEAD_DIM + 16, cols), F32), pltpu.VMEM((1, cols), F32),
                        pltpu.VMEM((NSA_KV_GROUPS, LANES, Q_TILE), F32)],
        compiler_params=pltpu.CompilerParams(dimension_semantics=("parallel", "parallel"),
                                             vmem_limit_bytes=VMEM_LIMIT),
        name="nsa",
    )(q, kcc, vcc, ks, vst, kw, vwt, gn, jnp.asarray(overlap_t, dtype=BF16), block_onehot)


POOL_HALO = 16


def _merge_kernel(x_ref, xh_ref, yn_ref, kvm_ref, g_ref, wpool_ref, wqx_ref, wgbr_ref, poolw_ref, pscale_ref,
                  wbp_ref, wbn_ref, wbx_ref, wout_ref, gpost_ref, o_ref, *, S):
    i = pl.program_id(0)
    tm = x_ref.shape[0]
    t0 = (i * tm) % S
    x = x_ref[...]
    g = g_ref[...]
    h = _rms(x, g)
    hb = h.astype(BF16)

    hh = _rms(xh_ref[...], g).astype(BF16)
    u_halo = jnp.dot(hh, wpool_ref[...], preferred_element_type=F32)
    u_halo = jnp.where(t0 > 0, u_halo, 0.0)
    u = jnp.dot(hb, wpool_ref[...], preferred_element_type=F32)
    ue = jnp.concatenate([u_halo, u], axis=0)
    trow = t0 + lax.broadcasted_iota(jnp.int32, (tm, 1), 0) + 1
    ypool = []
    for gi, w in enumerate(POOL_WINDOWS):
        acc = ue[:, gi * POOL_GROUP:(gi + 1) * POOL_GROUP]
        step = 1
        while step < w:
            acc = acc + pltpu.roll(acc, step, 0)
            step *= 2
        cnt = jnp.minimum(trow, w).astype(F32)
        p = acc[POOL_HALO:, :] / cnt - u[:, gi * POOL_GROUP:(gi + 1) * POOL_GROUP]
        ypool.append(jnp.dot(p.astype(BF16), poolw_ref[gi], preferred_element_type=F32))
    ypool = jnp.concatenate(ypool, axis=1) * pscale_ref[...]

    qx = jnp.dot(hb, wqx_ref[...], preferred_element_type=F32)
    ymem = []
    for hd in range(XA_HEADS):
        km = kvm_ref[0, :, hd * XA_HEAD_DIM:(hd + 1) * XA_HEAD_DIM]
        vm = kvm_ref[0, :, XA_WIDTH + hd * XA_HEAD_DIM:XA_WIDTH + (hd + 1) * XA_HEAD_DIM]
        s = _dot_nt(qx[:, hd * XA_HEAD_DIM:(hd + 1) * XA_HEAD_DIM].astype(BF16), km) * (XA_HEAD_DIM ** -0.5)
        e = jnp.exp(s - jnp.max(s, axis=-1, keepdims=True))
        p = e / jnp.sum(e, axis=-1, keepdims=True)
        ymem.append(jnp.dot(p.astype(BF16), vm, preferred_element_type=F32))
    ymem = jnp.concatenate(ymem, axis=1)

    gbr = jax.nn.sigmoid(jnp.dot(hb, wgbr_ref[...], preferred_element_type=F32))
    y = (gbr[:, 0:D_MODEL] * jnp.dot(ypool.astype(BF16), wbp_ref[...], preferred_element_type=F32)
         + gbr[:, D_MODEL:2 * D_MODEL] * jnp.dot(yn_ref[...], wbn_ref[...], preferred_element_type=F32)
         + gbr[:, 2 * D_MODEL:3 * D_MODEL] * jnp.dot(ymem.astype(BF16), wbx_ref[...], preferred_element_type=F32))
    o = jnp.dot(y.astype(BF16), wout_ref[...], preferred_element_type=F32)
    o_ref[...] = x + _rms(o, gpost_ref[...])


def _merge(x2, ynsa, kvm, g, wpool, wqx, wgbr, poolw, pscale, wbp, wbn, wbx, wout, gpost, S, tm=512):
    T = x2.shape[0]
    M = kvm.shape[1]
    row = lambda w: pl.BlockSpec((tm, w), lambda i: (i, 0))
    halo = pl.BlockSpec((POOL_HALO, D_MODEL), lambda i: (jnp.maximum(i * (tm // POOL_HALO) - 1, 0), 0))
    return pl.pallas_call(
        functools.partial(_merge_kernel, S=S),
        grid=(T // tm,),
        in_specs=[row(D_MODEL), halo, row(NSA_WIDTH),
                  pl.BlockSpec((1, M, 2 * XA_WIDTH), lambda i: ((i * tm) // S, 0, 0)),
                  _resident((1, D_MODEL)), _resident(wpool.shape), _resident(wqx.shape), _resident(wgbr.shape),
                  _resident(poolw.shape), _resident(pscale.shape), _resident(wbp.shape), _resident(wbn.shape),
                  _resident(wbx.shape), _resident(wout.shape), _resident((1, D_MODEL))],
        out_specs=row(D_MODEL),
        out_shape=jax.ShapeDtypeStruct((T, D_MODEL), F32),
        compiler_params=pltpu.CompilerParams(dimension_semantics=("parallel",), vmem_limit_bytes=VMEM_LIMIT),
        name="merge",
    )(x2, x2, ynsa, kvm, g, wpool, wqx, wgbr, poolw, pscale, wbp, wbn, wbx, wout, gpost)


FFN_HALO = 8
FFN_CHUNK = 2816


def _ffn_kernel(x_ref, xh_ref, g_ref, wup_ref, cw_ref, cb_ref, wdn_ref, gpost_ref, o_ref, *, S):
    i = pl.program_id(0)
    tm = x_ref.shape[0]
    t0 = (i * tm) % S
    x = x_ref[...]
    g = g_ref[...]
    hh = jnp.where(t0 > 0, _rms(xh_ref[...], g), 0.0)
    he = jnp.concatenate([hh, _rms(x, g)], axis=0).astype(BF16)

    def conv(col0):
        u = jnp.dot(he, wup_ref[:, col0:col0 + FFN_CHUNK], preferred_element_type=F32)
        c = cb_ref[:, col0:col0 + FFN_CHUNK] + cw_ref[CONV_WIDTH - 1:CONV_WIDTH, col0:col0 + FFN_CHUNK] * u
        for k in range(1, CONV_WIDTH):
            tap = cw_ref[CONV_WIDTH - 1 - k:CONV_WIDTH - k, col0:col0 + FFN_CHUNK]
            c = c + tap * pltpu.roll(u, k, 0)
        return c[FFN_HALO:, :]

    f = jnp.zeros((tm, D_MODEL), F32)
    for j in range(D_FF // FFN_CHUNK):
        act = _gelu_tanh(conv(j * FFN_CHUNK)) * conv(D_FF + j * FFN_CHUNK)
        f = f + jnp.dot(act.astype(BF16), wdn_ref[j * FFN_CHUNK:(j + 1) * FFN_CHUNK, :],
                        preferred_element_type=F32)
    o_ref[...] = x + _rms(f, gpost_ref[...])


def _ffn(x2, g, wup, cw, cb, wdn, gpost, S, tm=512):
    T = x2.shape[0]
    row = pl.BlockSpec((tm, D_MODEL), lambda i: (i, 0))
    halo = pl.BlockSpec((FFN_HALO, D_MODEL), lambda i: (jnp.maximum(i * (tm // FFN_HALO) - 1, 0), 0))
    return pl.pallas_call(
        functools.partial(_ffn_kernel, S=S),
        grid=(T // tm,),
        in_specs=[row, halo, _resident((1, D_MODEL)), _resident(wup.shape), _resident(cw.shape),
                  _resident(cb.shape), _resident(wdn.shape), _resident((1, D_MODEL))],
        out_specs=row,
        out_shape=jax.ShapeDtypeStruct((T, D_MODEL), F32),
        compiler_params=pltpu.CompilerParams(dimension_semantics=("parallel",), vmem_limit_bytes=VMEM_LIMIT),
        name="ffn",
    )(x2, x2, g, wup, cw, cb, wdn, gpost)


def _layer(x2, mem, pos2, B, S, pre_mix_g, w_in, pool_w, pool_scale, cmp_pe, cmp_w1, cmp_w2, mem_norm_g,
           w_mem_kv, w_br_pool, w_br_nsa, w_br_xa, w_out, post_mix_g, pre_ffn_g, w_up, conv_w, conv_b,
           w_down, post_ffn_g):
    o = np.cumsum((0,) + IN_SIZES)
    w_pool, w_q, w_kv, w_gn, w_qx, w_gbr = (w_in[:, o[k]:o[k + 1]] for k in range(6))
    w_gn = jnp.pad(w_gn, ((0, 0), (0, LANES - w_gn.shape[1])))
    vec = lambda a: a.reshape(1, -1)
    q, kc, vc, ks, vst, kw, vwt, gn = _in_proj(x2, pos2, vec(pre_mix_g), w_q.astype(BF16), w_kv[:, :128].astype(BF16),
                                               w_kv[:, 128:].astype(BF16), w_gn.astype(BF16))
    b3 = lambda a: a.reshape(B, S, a.shape[-1])
    kcc = _compress(b3(kc), cmp_pe[0], cmp_w1[0], cmp_w2[0])
    vcc = _compress(b3(vc), cmp_pe[1], cmp_w1[1], cmp_w2[1])
    ynsa = _nsa(b3(q), kcc, vcc, b3(ks), vst, b3(kw), vwt, b3(gn))
    kvm = _mem_kv(mem, vec(mem_norm_g), w_mem_kv.astype(BF16))
    x2 = _merge(x2, ynsa.reshape(B * S, NSA_WIDTH), kvm, vec(pre_mix_g), w_pool.astype(BF16), w_qx.astype(BF16),
                w_gbr.astype(BF16), pool_w.astype(BF16), vec(pool_scale), w_br_pool.astype(BF16),
                w_br_nsa.astype(BF16), w_br_xa.astype(BF16), w_out.astype(BF16), vec(post_mix_g), S)
    return _ffn(x2, vec(pre_ffn_g), w_up.astype(BF16), conv_w, vec(conv_b), w_down.astype(BF16),
                vec(post_ffn_g), S)


def kernel(x, mem, positions, pre_mix_g, w_in, pool_w, pool_scale, cmp_pe, cmp_w1, cmp_w2, mem_norm_g, w_mem_kv,
           w_br_pool, w_br_nsa, w_br_xa, w_out, post_mix_g, pre_ffn_g, w_up, conv_w, conv_b, w_down, post_ffn_g):
    B, S, D = x.shape
    x2 = x.reshape(B * S, D)
    pos2 = jnp.broadcast_to(positions.reshape(B * S, 1), (B * S, LANES))
    for l in range(pre_mix_g.shape[0]):
        x2 = _layer(x2, mem, pos2, B, S, pre_mix_g[l], w_in[l], pool_w[l], pool_scale[l], cmp_pe[l], cmp_w1[l],
                    cmp_w2[l], mem_norm_g[l], w_mem_kv[l], w_br_pool[l], w_br_nsa[l], w_br_xa[l], w_out[l],
                    post_mix_g[l], pre_ffn_g[l], w_up[l], conv_w[l], conv_b[l], w_down[l], post_ffn_g[l])
    return x2.reshape(B, S, D)
```

```python
import functools

import numpy as np
import jax
import jax.numpy as jnp
from jax import lax
from jax.experimental import pallas as pl
from jax.experimental.pallas import tpu as pltpu

F32 = jnp.float32
BF16 = jnp.bfloat16

D_MODEL = 1024
EPS = 1e-6
POOL_WINDOWS = (2, 4, 8, 16)
POOL_GROUP = 128
POOL_WIDTH = POOL_GROUP * len(POOL_WINDOWS)
NSA_HEADS = 16
NSA_KV_GROUPS = 2
HEADS_PER_GROUP = NSA_HEADS // NSA_KV_GROUPS
HEAD_DIM = 64
NSA_WIDTH = NSA_HEADS * HEAD_DIM
NSA_KV_WIDTH = NSA_KV_GROUPS * HEAD_DIM
CMP_BLOCK = 32
CMP_STRIDE = 16
CMP_HIDDEN = 256
SEL_BLOCK = 64
SEL_TOPK = 16
WINDOW = 512
ROPE_THETA = 500000.0
ROT_DIM = HEAD_DIM // 4
XA_HEADS = 4
XA_HEAD_DIM = 128
XA_WIDTH = XA_HEADS * XA_HEAD_DIM
N_BRANCHES = 3
D_FF = 2816
CONV_WIDTH = 3
IN_SIZES = (POOL_WIDTH, NSA_WIDTH, 6 * NSA_KV_WIDTH, 3 * NSA_HEADS, XA_WIDTH, N_BRANCHES * D_MODEL)

LANES = 128
Q_TILE = 128
SEL_KEY_TILE = 512
MASK_NEG = -1e9
SCORE_FLOOR = -1e30
LAZY_MAX_HEADROOM = 64.0
VMEM_LIMIT = 56 * 1024 * 1024
Q_SCALE = HEAD_DIM ** -0.5 * 1.4426950408889634


def _rms(x, g):
    return x * lax.rsqrt(jnp.mean(x * x, axis=-1, keepdims=True) + EPS) * g


def _gelu_tanh(x):
    return 0.5 * x * (1.0 + jnp.tanh(0.7978845608028654 * (x + 0.044715 * (x * x * x))))


def _dot_bf16(x, w):
    return jnp.dot(x.astype(BF16), w, preferred_element_type=F32)


def _dot_nt(a, b):
    return lax.dot_general(a, b, (((1,), (1,)), ((), ())), preferred_element_type=F32)


def _resident(shape):
    return pl.BlockSpec(shape, lambda *_: (0,) * len(shape), pipeline_mode=pl.Buffered(1))


def _inproj_kernel(x_ref, pos_ref, g_ref, wq_ref, wkc_ref, wkv_ref, wgn_ref, frq_ref, sg1_ref, sg2_ref,
                   q_ref, kc_ref, vc_ref, ks_ref, vs_ref, kw_ref, vw_ref, gn_ref):
    h = _rms(x_ref[...], g_ref[...])
    hb = h.astype(BF16)
    ang = pos_ref[...].astype(F32) * frq_ref[...]
    c = jnp.cos(ang)
    s = jnp.sin(ang)
    s1 = s * sg1_ref[...]
    s2 = s * sg2_ref[...]

    def rope(v):
        return v * c + pltpu.roll(v, LANES - ROT_DIM // 2, 1) * s1 + pltpu.roll(v, ROT_DIM // 2, 1) * s2

    q = jnp.dot(hb, wq_ref[...], preferred_element_type=F32)
    for j in range(NSA_WIDTH // LANES):
        q_ref[:, j * LANES:(j + 1) * LANES] = rope(q[:, j * LANES:(j + 1) * LANES]) * Q_SCALE
    kc_ref[...] = rope(jnp.dot(hb, wkc_ref[...], preferred_element_type=F32)).astype(BF16)
    kv = jnp.dot(hb, wkv_ref[...], preferred_element_type=F32)
    vc_ref[...] = kv[:, 0:128].astype(BF16)
    ks_ref[...] = rope(kv[:, 128:256]).astype(BF16)
    vs_ref[...] = kv[:, 256:384].T.astype(BF16)
    kw_ref[...] = rope(kv[:, 384:512]).astype(BF16)
    vw_ref[...] = kv[:, 512:640].T.astype(BF16)
    gn_ref[...] = jax.nn.sigmoid(jnp.dot(hb, wgn_ref[...], preferred_element_type=F32))


def _in_proj(x2, pos2, g, wq, wkc, wkv, wgn, tm=512):
    T = x2.shape[0]
    half = ROT_DIM // 2
    d = np.arange(LANES) % HEAD_DIM
    inv_freq = ROPE_THETA ** (-jnp.arange(half, dtype=F32) * (2.0 / ROT_DIM))
    frq = jnp.where(d < ROT_DIM, inv_freq[d % half], 0.0)[None, :]
    sg1 = np.where(d < half, -1.0, 0.0).astype(np.float32)[None, :]
    sg2 = np.where((d >= half) & (d < ROT_DIM), 1.0, 0.0).astype(np.float32)[None, :]
    row = lambda w: pl.BlockSpec((tm, w), lambda i: (i, 0))
    col = pl.BlockSpec((128, tm), lambda i: (0, i))
    out_shape = [jax.ShapeDtypeStruct((T, NSA_WIDTH), F32),
                 jax.ShapeDtypeStruct((T, 128), BF16), jax.ShapeDtypeStruct((T, 128), BF16),
                 jax.ShapeDtypeStruct((T, 128), BF16), jax.ShapeDtypeStruct((128, T), BF16),
                 jax.ShapeDtypeStruct((T, 128), BF16), jax.ShapeDtypeStruct((128, T), BF16),
                 jax.ShapeDtypeStruct((T, 128), F32)]
    return pl.pallas_call(
        _inproj_kernel,
        grid=(T // tm,),
        in_specs=[row(D_MODEL), row(1), _resident((1, D_MODEL)), _resident(wq.shape), _resident(wkc.shape),
                  _resident(wkv.shape), _resident(wgn.shape), _resident((1, LANES)), _resident((1, LANES)),
                  _resident((1, LANES))],
        out_specs=[row(NSA_WIDTH), row(128), row(128), row(128), col, row(128), col, row(128)],
        out_shape=out_shape,
        compiler_params=pltpu.CompilerParams(dimension_semantics=("parallel",), vmem_limit_bytes=VMEM_LIMIT),
        name="in_proj",
    )(x2, pos2, g, wq, wkc, wkv, wgn, frq, jnp.asarray(sg1), jnp.asarray(sg2))


def _compress_kernel(c_ref, pet_ref, peb_ref, wt_ref, wb_ref, w2_ref, o_ref):
    c = c_ref[0].astype(F32)
    a = _dot_bf16(c + pet_ref[...], wt_ref[...])
    b = _dot_bf16(c + peb_ref[...], wb_ref[...])
    n = c.shape[0]
    hid = a + pltpu.roll(b, n - 1, 0)
    o_ref[0] = _dot_bf16(_gelu_tanh(hid), w2_ref[...])


def _compress(kv, pe, w1, w2):
    B, S, _ = kv.shape
    n = S // CMP_STRIDE
    G = NSA_KV_GROUPS
    half = CMP_BLOCK // 2
    eye = jnp.eye(G, dtype=F32)

    def blockdiag_w1(w):
        w = w.reshape(half, 1, HEAD_DIM, 1, CMP_HIDDEN) * eye.reshape(1, G, 1, G, 1)
        return w.reshape(half * G * HEAD_DIM, G * CMP_HIDDEN)

    wt = blockdiag_w1(w1[:half * HEAD_DIM]).astype(BF16)
    wb = blockdiag_w1(w1[half * HEAD_DIM:]).astype(BF16)
    w2b = ((w2.reshape(1, CMP_HIDDEN, 1, HEAD_DIM) * eye.reshape(G, 1, G, 1))
           .reshape(G * CMP_HIDDEN, G * HEAD_DIM).astype(BF16))
    pet = jnp.broadcast_to(pe[:half, None, :], (half, G, HEAD_DIM)).reshape(1, half * G * HEAD_DIM)
    peb = jnp.broadcast_to(pe[half:, None, :], (half, G, HEAD_DIM)).reshape(1, half * G * HEAD_DIM)
    c = kv.reshape(B, n, CMP_STRIDE * 128)
    return pl.pallas_call(
        _compress_kernel,
        grid=(B,),
        in_specs=[pl.BlockSpec((1, n, CMP_STRIDE * 128), lambda b: (b, 0, 0)),
                  _resident(pet.shape), _resident(peb.shape), _resident(wt.shape), _resident(wb.shape),
                  _resident(w2b.shape)],
        out_specs=pl.BlockSpec((1, n, 128), lambda b: (b, 0, 0)),
        out_shape=jax.ShapeDtypeStruct((B, n, 128), F32),
        compiler_params=pltpu.CompilerParams(dimension_semantics=("parallel",), vmem_limit_bytes=VMEM_LIMIT),
        name="compress",
    )(c, pet, peb, wt, wb, w2b)


def _memkv_kernel(m_ref, g_ref, w_ref, o_ref):
    mn = _rms(m_ref[0], g_ref[...])
    o_ref[0] = jnp.dot(mn.astype(BF16), w_ref[...], preferred_element_type=F32).astype(BF16)


def _mem_kv(mem, g, w):
    B, M, _ = mem.shape
    return pl.pallas_call(
        _memkv_kernel,
        grid=(B,),
        in_specs=[pl.BlockSpec((1, M, D_MODEL), lambda b: (b, 0, 0)), _resident((1, D_MODEL)),
                  _resident(w.shape)],
        out_specs=pl.BlockSpec((1, M, 2 * XA_WIDTH), lambda b: (b, 0, 0)),
        out_shape=jax.ShapeDtypeStruct((B, M, 2 * XA_WIDTH), BF16),
        compiler_params=pltpu.CompilerParams(dimension_semantics=("parallel",), vmem_limit_bytes=VMEM_LIMIT),
        name="mem_kv",
    )(mem, g, w)


def _nsa_kernel(q_ref, kcc_ref, vcc_ref, ks_ref, vst_ref, kw_ref, vwt_ref, gn_ref, ovt_ref, blk_ref, o_ref,
                qat_ref, oct_ref, ost_ref, owt_ref, m_ref, l_ref, ap_ref, pvp_ref, over_ref, imp_ref):
    qt = pl.program_id(1)
    q0 = qt * Q_TILE
    n_cmp = kcc_ref.shape[1]
    head_cols = [slice(h * Q_TILE, (h + 1) * Q_TILE) for h in range(NSA_HEADS)]
    group_cols = [slice(g * HEADS_PER_GROUP * Q_TILE, (g + 1) * HEADS_PER_GROUP * Q_TILE)
                  for g in range(NSA_KV_GROUPS)]
    per_head = lambda a, n: jnp.concatenate([a] * n, axis=1)

    def masked(s, bias):
        return jnp.concatenate([s[:, c:c + Q_TILE] + bias for c in range(0, s.shape[1], Q_TILE)], axis=1)

    def value_product(vt_ref, start, n, p, extra_rows=None):
        out = []
        for g in range(NSA_KV_GROUPS):
            lhs = [vt_ref[g * HEAD_DIM:(g + 1) * HEAD_DIM, pl.ds(start, n)], jnp.ones((16, n), BF16)]
            if extra_rows is not None:
                lhs.append(extra_rows)
            out.append(jnp.dot(jnp.concatenate(lhs, axis=0), p[:, group_cols[g]], preferred_element_type=F32))
        return jnp.concatenate(out, axis=1)

    zeros = jnp.zeros((HEAD_DIM, Q_TILE), F32)
    for j in range(NSA_HEADS // 2):
        t = q_ref[0, :, j * LANES:(j + 1) * LANES].T
        for k in range(2):
            h = 2 * j + k
            part = t[k * HEAD_DIM:(k + 1) * HEAD_DIM, :]
            pair = [part, zeros] if h // HEADS_PER_GROUP == 0 else [zeros, part]
            qat_ref[0:LANES, head_cols[h]] = jnp.concatenate(pair, axis=0).astype(BF16)

    def compressed_branch(n_rows):
        crow = lax.broadcasted_iota(jnp.int32, (n_rows, Q_TILE), 0)
        cq = lax.broadcasted_iota(jnp.int32, (n_rows, Q_TILE), 1) + q0
        valid_c = crow * CMP_STRIDE + (CMP_BLOCK - 1) <= cq
        bias_c = jnp.where(valid_c, 0.0, SCORE_FLOOR)
        seen_c = per_head(jnp.where(cq[0:1, :] >= CMP_BLOCK - 1, 1.0, 0.0), HEADS_PER_GROUP)
        kcc = kcc_ref[0, 0:n_rows, :].astype(BF16)
        vcct = vcc_ref[0, 0:n_rows, :].T.astype(BF16)
        ones_ov = jnp.concatenate([jnp.ones((16, n_rows), BF16), ovt_ref[:, 0:n_rows]], axis=0)
        for g in range(NSA_KV_GROUPS):
            cols = group_cols[g]
            s = masked(jnp.dot(kcc, qat_ref[0:LANES, cols], preferred_element_type=F32), bias_c)
            p = jnp.exp2(s - jnp.max(s, axis=0, keepdims=True)).astype(BF16)
            lhs = jnp.concatenate([vcct[g * HEAD_DIM:(g + 1) * HEAD_DIM], ones_ov], axis=0)
            prod = jnp.dot(lhs, p, preferred_element_type=F32)
            norm = seen_c / jnp.maximum(prod[HEAD_DIM:HEAD_DIM + 1], jnp.finfo(F32).tiny)
            oct_ref[:, cols] = prod[0:HEAD_DIM] * norm
            imp_heads = prod[HEAD_DIM + 16:HEAD_DIM + 16 + LANES] * norm
            imp = imp_heads[:, 0:Q_TILE]
            for k in range(1, HEADS_PER_GROUP):
                imp = imp + imp_heads[:, k * Q_TILE:(k + 1) * Q_TILE]
            imp_ref[g] = imp

    row_steps = list(range(LANES, n_cmp + 1, LANES))
    for k, n_rows in enumerate(row_steps):
        in_step = (qt * (Q_TILE // CMP_STRIDE) + Q_TILE // CMP_STRIDE - 1) // LANES == k
        pl.when(in_step)(functools.partial(compressed_branch, n_rows))

    brow = lax.broadcasted_iota(jnp.int32, (LANES, Q_TILE), 0)
    cur = (lax.broadcasted_iota(jnp.int32, (LANES, Q_TILE), 1) + q0) // SEL_BLOCK
    forced = (brow == 0) | (brow == cur) | (brow == cur - 1)
    brow_f = brow.astype(F32)
    for g in range(NSA_KV_GROUPS):
        imp = imp_ref[g]
        v = jnp.where(brow > cur, -1.0, jnp.where(forced, -2.0, imp))
        for _ in range(SEL_TOPK - 3):
            mx = jnp.max(v, axis=0, keepdims=True)
            first = jnp.min(jnp.where(v == mx, brow_f, float(LANES)), axis=0, keepdims=True)
            v = jnp.where(brow_f == first, -2.0, v)
        neg = jnp.where(v == -2.0, 0.0, MASK_NEG).astype(BF16)
        for h in range(HEADS_PER_GROUP):
            qat_ref[LANES:2 * LANES, head_cols[g * HEADS_PER_GROUP + h]] = neg

    def scores(start, n):
        ka = jnp.concatenate([ks_ref[0, pl.ds(start, n), :], blk_ref[pl.ds(start, n), :]], axis=1)
        return jnp.dot(ka, qat_ref[...], preferred_element_type=F32)

    def online_update(s, start, n):
        m_old = m_ref[...]
        m_new = jnp.maximum(m_old, jnp.max(s, axis=0, keepdims=True))
        alpha = jnp.exp2(m_old - m_new)
        pv = value_product(vst_ref, start, n, jnp.exp2(s - m_new).astype(BF16))
        ost_ref[...] = alpha * ost_ref[...] + pv[0:HEAD_DIM]
        l_ref[...] = alpha * l_ref[...] + pv[HEAD_DIM:HEAD_DIM + 1]
        m_ref[...] = m_new

    qd = pl.multiple_of(q0, Q_TILE)
    drow = lax.broadcasted_iota(jnp.int32, (Q_TILE, Q_TILE), 0)
    dcol = lax.broadcasted_iota(jnp.int32, (Q_TILE, Q_TILE), 1)
    bias_d = jnp.where(drow <= dcol, 0.0, SCORE_FLOOR)
    head = q0 % SEL_KEY_TILE
    n_full = q0 // SEL_KEY_TILE
    head_sizes = range(Q_TILE, SEL_KEY_TILE, Q_TILE)
    tile_start = lambda j: pl.multiple_of(head + j * SEL_KEY_TILE, Q_TILE)

    def start_with_diagonal():
        l_ref[...] = jnp.zeros(l_ref.shape, F32)
        ost_ref[...] = jnp.zeros(ost_ref.shape, F32)
        m_ref[...] = jnp.full(m_ref.shape, SCORE_FLOOR, F32)
        online_update(masked(jnp.dot(ks_ref[0, pl.ds(qd, Q_TILE), :], qat_ref[0:LANES, :],
                                     preferred_element_type=F32), bias_d), qd, Q_TILE)

    start_with_diagonal()
    pvp_ref[...] = jnp.zeros(pvp_ref.shape, F32)
    ap_ref[...] = jnp.ones(ap_ref.shape, F32)
    over_ref[...] = jnp.full(over_ref.shape, SCORE_FLOOR, F32)

    def fold_parked():
        alpha = ap_ref[...]
        ost_ref[...] = alpha * (ost_ref[...] + pvp_ref[0:HEAD_DIM, :])
        l_ref[...] = alpha * (l_ref[...] + pvp_ref[HEAD_DIM:HEAD_DIM + 1, :])

    def lazy_step(start, n):
        fold_parked()
        s = scores(start, n)
        m_old = m_ref[...]
        pvp_ref[...] = value_product(vst_ref, start, n, jnp.exp2(s - m_old).astype(BF16))
        t_max = jnp.max(s, axis=0, keepdims=True)
        m_new = jnp.maximum(m_old, t_max)
        over_ref[...] = jnp.maximum(over_ref[...], t_max - m_old)
        ap_ref[...] = jnp.exp2(m_old - m_new)
        m_ref[...] = m_new

    def lazy_pair(j, carry):
        lazy_step(tile_start(2 * j), SEL_KEY_TILE)
        lazy_step(tile_start(2 * j + 1), SEL_KEY_TILE)
        return carry

    lax.fori_loop(0, n_full // 2, lazy_pair, 0)
    pl.when(n_full % 2 == 1)(lambda: lazy_step(tile_start(n_full - 1), SEL_KEY_TILE))
    for n in head_sizes:
        pl.when(head == n)(functools.partial(lazy_step, 0, n))
    fold_parked()

    @pl.when(jnp.max(over_ref[...]) > LAZY_MAX_HEADROOM)
    def _():
        start_with_diagonal()

        def textbook_tile(j, carry):
            online_update(scores(tile_start(j), SEL_KEY_TILE), tile_start(j), SEL_KEY_TILE)
            return carry

        lax.fori_loop(0, n_full, textbook_tile, 0)
        for n in head_sizes:
            pl.when(head == n)(lambda n=n: online_update(scores(0, n), 0, n))

    wlen = WINDOW + Q_TILE
    w0 = pl.multiple_of(jnp.maximum(q0 - WINDOW, 0), Q_TILE)
    kidx = w0 + lax.broadcasted_iota(jnp.int32, (wlen, Q_TILE), 0)
    tqw = q0 + lax.broadcasted_iota(jnp.int32, (wlen, Q_TILE), 1)
    bias_w = jnp.where((kidx <= tqw) & (tqw - kidx < WINDOW), 0.0, SCORE_FLOOR)
    s = masked(jnp.dot(kw_ref[0, pl.ds(w0, wlen), :], qat_ref[0:LANES, :], preferred_element_type=F32), bias_w)
    o = value_product(vwt_ref, w0, wlen, jnp.exp2(s - jnp.max(s, axis=0, keepdims=True)).astype(BF16))
    owt_ref[...] = o[0:HEAD_DIM] / o[HEAD_DIM:HEAD_DIM + 1]

    gt = gn_ref[0].T
    for j in range(NSA_HEADS // 2):
        parts = []
        for h in (2 * j, 2 * j + 1):
            cols = head_cols[h]
            o_sel = ost_ref[:, cols] / l_ref[:, cols]
            parts.append(gt[3 * h:3 * h + 1, :] * oct_ref[:, cols] + gt[3 * h + 1:3 * h + 2, :] * o_sel
                         + gt[3 * h + 2:3 * h + 3, :] * owt_ref[:, cols])
        o_ref[0, :, j * LANES:(j + 1) * LANES] = jnp.concatenate(parts, axis=0).T.astype(o_ref.dtype)


def _nsa(q, kcc, vcc, ks, vst, kw, vwt, gn):
    B, S, _ = q.shape
    n_cmp = kcc.shape[1]
    n_sel = S // SEL_BLOCK
    assert n_sel <= LANES and n_cmp % LANES == 0 and S % SEL_KEY_TILE == 0 and S >= WINDOW + Q_TILE
    ci = np.arange(n_cmp)[None, :] * CMP_STRIDE
    sj = np.arange(LANES)[:, None] * SEL_BLOCK
    overlap_t = ((ci < sj + SEL_BLOCK) & (ci + CMP_BLOCK > sj)).astype(np.float32)
    block_onehot = (np.arange(S)[:, None] // SEL_BLOCK == np.arange(LANES)[None, :])
    block_onehot = jnp.asarray(block_onehot.astype(np.float32), dtype=BF16)
    cols = NSA_HEADS * Q_TILE
    full = lambda n, w: pl.BlockSpec((1, n, w), lambda b, i: (b, 0, 0))
    full_t = pl.BlockSpec((128, S), lambda b, i: (0, b))
    tile = lambda w: pl.BlockSpec((1, Q_TILE, w), lambda b, i: (b, i, 0))
    return pl.pallas_call(
        _nsa_kernel,
        grid=(B, S // Q_TILE),
        in_specs=[tile(NSA_WIDTH), full(n_cmp, 128), full(n_cmp, 128), full(S, 128), full_t,
                  full(S, 128), full_t, tile(128), _resident(overlap_t.shape), _resident((S, LANES))],
        out_specs=tile(NSA_WIDTH),
        out_shape=jax.ShapeDtypeStruct((B, S, NSA_WIDTH), BF16),
        scratch_shapes=[pltpu.VMEM((2 * LANES, cols), BF16),
                        pltpu.VMEM((HEAD_DIM, cols), F32), pltpu.VMEM((HEAD_DIM, cols), F32),
                        pltpu.VMEM((HEAD_DIM, cols), F32), pltpu.VMEM((1, cols), F32),
                        pltpu.VMEM((1, cols), F32), pltpu.VMEM((1, cols), F32),
                        pltpu.VMEM((HEAD_DIM + 16, cols), F32), pltpu.VMEM((1, cols), F32),
                        pltpu.VMEM((NSA_KV_GROUPS, LANES, Q_TILE), F32)],
        compiler_params=pltpu.CompilerParams(dimension_semantics=("parallel", "parallel"),
                                             vmem_limit_bytes=VMEM_LIMIT),
        name="nsa",
    )(q, kcc, vcc, ks, vst, kw, vwt, gn, jnp.asarray(overlap_t, dtype=BF16), block_onehot)


POOL_HALO = 16


def _merge_kernel(x_ref, xh_ref, yn_ref, kvm_ref, g_ref, wpool_ref, wqx_ref, wgbr_ref, poolw_ref, pscale_ref,
                  wbp_ref, wbn_ref, wbx_ref, wout_ref, gpost_ref, o_ref, *, S):
    i = pl.program_id(0)
    tm = x_ref.shape[0]
    t0 = (i * tm) % S
    x = x_ref[...]
    g = g_ref[...]
    h = _rms(x, g)
    hb = h.astype(BF16)

    hh = _rms(xh_ref[...], g).astype(BF16)
    u_halo = jnp.dot(hh, wpool_ref[...], preferred_element_type=F32)
    u_halo = jnp.where(t0 > 0, u_halo, 0.0)
    u = jnp.dot(hb, wpool_ref[...], preferred_element_type=F32)
    ue = jnp.concatenate([u_halo, u], axis=0)
    trow = t0 + lax.broadcasted_iota(jnp.int32, (tm, 1), 0) + 1
    ypool = []
    for gi, w in enumerate(POOL_WINDOWS):
        acc = ue[:, gi * POOL_GROUP:(gi + 1) * POOL_GROUP]
        step = 1
        while step < w:
            acc = acc + pltpu.roll(acc, step, 0)
            step *= 2
        cnt = jnp.minimum(trow, w).astype(F32)
        p = acc[POOL_HALO:, :] / cnt - u[:, gi * POOL_GROUP:(gi + 1) * POOL_GROUP]
        ypool.append(jnp.dot(p.astype(BF16), poolw_ref[gi], preferred_element_type=F32))
    ypool = jnp.concatenate(ypool, axis=1) * pscale_ref[...]

    qx = jnp.dot(hb, wqx_ref[...], preferred_element_type=F32)
    ymem = []
    for hd in range(XA_HEADS):
        km = kvm_ref[0, :, hd * XA_HEAD_DIM:(hd + 1) * XA_HEAD_DIM]
        vm = kvm_ref[0, :, XA_WIDTH + hd * XA_HEAD_DIM:XA_WIDTH + (hd + 1) * XA_HEAD_DIM]
        s = _dot_nt(qx[:, hd * XA_HEAD_DIM:(hd + 1) * XA_HEAD_DIM].astype(BF16), km) * (XA_HEAD_DIM ** -0.5)
        e = jnp.exp(s - jnp.max(s, axis=-1, keepdims=True))
        p = e / jnp.sum(e, axis=-1, keepdims=True)
        ymem.append(jnp.dot(p.astype(BF16), vm, preferred_element_type=F32))
    ymem = jnp.concatenate(ymem, axis=1)

    gbr = jax.nn.sigmoid(jnp.dot(hb, wgbr_ref[...], preferred_element_type=F32))
    y = (gbr[:, 0:D_MODEL] * jnp.dot(ypool.astype(BF16), wbp_ref[...], preferred_element_type=F32)
         + gbr[:, D_MODEL:2 * D_MODEL] * jnp.dot(yn_ref[...], wbn_ref[...], preferred_element_type=F32)
         + gbr[:, 2 * D_MODEL:3 * D_MODEL] * jnp.dot(ymem.astype(BF16), wbx_ref[...], preferred_element_type=F32))
    o = jnp.dot(y.astype(BF16), wout_ref[...], preferred_element_type=F32)
    o_ref[...] = x + _rms(o, gpost_ref[...])


def _merge(x2, ynsa, kvm, g, wpool, wqx, wgbr, poolw, pscale, wbp, wbn, wbx, wout, gpost, S, tm=512):
    T = x2.shape[0]
    M = kvm.shape[1]
    row = lambda w: pl.BlockSpec((tm, w), lambda i: (i, 0))
    halo = pl.BlockSpec((POOL_HALO, D_MODEL), lambda i: (jnp.maximum(i * (tm // POOL_HALO) - 1, 0), 0))
    return pl.pallas_call(
        functools.partial(_merge_kernel, S=S),
        grid=(T // tm,),
        in_specs=[row(D_MODEL), halo, row(NSA_WIDTH),
                  pl.BlockSpec((1, M, 2 * XA_WIDTH), lambda i: ((i * tm) // S, 0, 0)),
                  _resident((1, D_MODEL)), _resident(wpool.shape), _resident(wqx.shape), _resident(wgbr.shape),
                  _resident(poolw.shape), _resident(pscale.shape), _resident(wbp.shape), _resident(wbn.shape),
                  _resident(wbx.shape), _resident(wout.shape), _resident((1, D_MODEL))],
        out_specs=row(D_MODEL),
        out_shape=jax.ShapeDtypeStruct((T, D_MODEL), F32),
        compiler_params=pltpu.CompilerParams(dimension_semantics=("parallel",), vmem_limit_bytes=VMEM_LIMIT),
        name="merge",
    )(x2, x2, ynsa, kvm, g, wpool, wqx, wgbr, poolw, pscale, wbp, wbn, wbx, wout, gpost)


FFN_HALO = 8
FFN_CHUNK = 2816


def _ffn_kernel(x_ref, xh_ref, g_ref, wup_ref, cw_ref, cb_ref, wdn_ref, gpost_ref, o_ref, *, S):
    i = pl.program_id(0)
    tm = x_ref.shape[0]
    t0 = (i * tm) % S
    x = x_ref[...]
    g = g_ref[...]
    hh = jnp.where(t0 > 0, _rms(xh_ref[...], g), 0.0)
    he = jnp.concatenate([hh, _rms(x, g)], axis=0).astype(BF16)

    def conv(col0):
        u = jnp.dot(he, wup_ref[:, col0:col0 + FFN_CHUNK], preferred_element_type=F32)
        c = cb_ref[:, col0:col0 + FFN_CHUNK] + cw_ref[CONV_WIDTH - 1:CONV_WIDTH, col0:col0 + FFN_CHUNK] * u
        for k in range(1, CONV_WIDTH):
            tap = cw_ref[CONV_WIDTH - 1 - k:CONV_WIDTH - k, col0:col0 + FFN_CHUNK]
            c = c + tap * pltpu.roll(u, k, 0)
        return c[FFN_HALO:, :]

    f = jnp.zeros((tm, D_MODEL), F32)
    for j in range(D_FF // FFN_CHUNK):
        act = _gelu_tanh(conv(j * FFN_CHUNK)) * conv(D_FF + j * FFN_CHUNK)
        f = f + jnp.dot(act.astype(BF16), wdn_ref[j * FFN_CHUNK:(j + 1) * FFN_CHUNK, :],
                        preferred_element_type=F32)
    o_ref[...] = x + _rms(f, gpost_ref[...])


def _ffn(x2, g, wup, cw, cb, wdn, gpost, S, tm=512):
    T = x2.shape[0]
    row = pl.BlockSpec((tm, D_MODEL), lambda i: (i, 0))
    halo = pl.BlockSpec((FFN_HALO, D_MODEL), lambda i: (jnp.maximum(i * (tm // FFN_HALO) - 1, 0), 0))
    return pl.pallas_call(
        functools.partial(_ffn_kernel, S=S),
        grid=(T // tm,),
        in_specs=[row, halo, _resident((1, D_MODEL)), _resident(wup.shape), _resident(cw.shape),
                  _resident(cb.shape), _resident(wdn.shape), _resident((1, D_MODEL))],
        out_specs=row,
        out_shape=jax.ShapeDtypeStruct((T, D_MODEL), F32),
        compiler_params=pltpu.CompilerParams(dimension_semantics=("parallel",), vmem_limit_bytes=VMEM_LIMIT),
        name="ffn",
    )(x2, x2, g, wup, cw, cb, wdn, gpost)


def _layer(x2, mem, pos2, B, S, pre_mix_g, w_in, pool_w, pool_scale, cmp_pe, cmp_w1, cmp_w2, mem_norm_g,
           w_mem_kv, w_br_pool, w_br_nsa, w_br_xa, w_out, post_mix_g, pre_ffn_g, w_up, conv_w, conv_b,
           w_down, post_ffn_g):
    o = np.cumsum((0,) + IN_SIZES)
    w_pool, w_q, w_kv, w_gn, w_qx, w_gbr = (w_in[:, o[k]:o[k + 1]] for k in range(6))
    w_gn = jnp.pad(w_gn, ((0, 0), (0, LANES - w_gn.shape[1])))
    vec = lambda a: a.reshape(1, -1)
    q, kc, vc, ks, vst, kw, vwt, gn = _in_proj(x2, pos2, vec(pre_mix_g), w_q.astype(BF16), w_kv[:, :128].astype(BF16),
                                               w_kv[:, 128:].astype(BF16), w_gn.astype(BF16))
    b3 = lambda a: a.reshape(B, S, a.shape[-1])
    kcc = _compress(b3(kc), cmp_pe[0], cmp_w1[0], cmp_w2[0])
    vcc = _compress(b3(vc), cmp_pe[1], cmp_w1[1], cmp_w2[1])
    ynsa = _nsa(b3(q), kcc, vcc, b3(ks), vst, b3(kw), vwt, b3(gn))
    kvm = _mem_kv(mem, vec(mem_norm_g), w_mem_kv.astype(BF16))
    x2 = _merge(x2, ynsa.reshape(B * S, NSA_WIDTH), kvm, vec(pre_mix_g), w_pool.astype(BF16), w_qx.astype(BF16),
                w_gbr.astype(BF16), pool_w.astype(BF16), vec(pool_scale), w_br_pool.astype(BF16),
                w_br_nsa.astype(BF16), w_br_xa.astype(BF16), w_out.astype(BF16), vec(post_mix_g), S)
    return _ffn(x2, vec(pre_ffn_g), w_up.astype(BF16), conv_w, vec(conv_b), w_down.astype(BF16),
                vec(post_ffn_g), S)


def kernel(x, mem, positions, pre_mix_g, w_in, pool_w, pool_scale, cmp_pe, cmp_w1, cmp_w2, mem_norm_g, w_mem_kv,
           w_br_pool, w_br_nsa, w_br_xa, w_out, post_mix_g, pre_ffn_g, w_up, conv_w, conv_b, w_down, post_ffn_g):
    B, S, D = x.shape
    x2 = x.reshape(B * S, D)
    pos2 = positions.reshape(B * S, 1)
    for l in range(pre_mix_g.shape[0]):
        x2 = _layer(x2, mem, pos2, B, S, pre_mix_g[l], w_in[l], pool_w[l], pool_scale[l], cmp_pe[l], cmp_w1[l],
                    cmp_w2[l], mem_norm_g[l], w_mem_kv[l], w_br_pool[l], w_br_nsa[l], w_br_xa[l], w_out[l],
                    post_mix_g[l], pre_ffn_g[l], w_up[l], conv_w[l], conv_b[l], w_down[l], post_ffn_g[l])
    return x2.reshape(B, S, D)
```

```python
import functools

import numpy as np
import jax
import jax.numpy as jnp
from jax import lax
from jax.experimental import pallas as pl
from jax.experimental.pallas import tpu as pltpu

F32 = jnp.float32
BF16 = jnp.bfloat16

D_MODEL = 1024
EPS = 1e-6
POOL_WINDOWS = (2, 4, 8, 16)
POOL_GROUP = 128
POOL_WIDTH = POOL_GROUP * len(POOL_WINDOWS)
NSA_HEADS = 16
NSA_KV_GROUPS = 2
HEADS_PER_GROUP = NSA_HEADS // NSA_KV_GROUPS
HEAD_DIM = 64
NSA_WIDTH = NSA_HEADS * HEAD_DIM
NSA_KV_WIDTH = NSA_KV_GROUPS * HEAD_DIM
CMP_BLOCK = 32
CMP_STRIDE = 16
CMP_HIDDEN = 256
SEL_BLOCK = 64
SEL_TOPK = 16
WINDOW = 512
ROPE_THETA = 500000.0
ROT_DIM = HEAD_DIM // 4
XA_HEADS = 4
XA_HEAD_DIM = 128
XA_WIDTH = XA_HEADS * XA_HEAD_DIM
N_BRANCHES = 3
D_FF = 2816
CONV_WIDTH = 3
IN_SIZES = (POOL_WIDTH, NSA_WIDTH, 6 * NSA_KV_WIDTH, 3 * NSA_HEADS, XA_WIDTH, N_BRANCHES * D_MODEL)

LANES = 128
Q_TILE = 128
SEL_KEY_TILE = 512
MASK_NEG = -1e9
SCORE_FLOOR = -1e30
LAZY_MAX_HEADROOM = 64.0
VMEM_LIMIT = 56 * 1024 * 1024
Q_SCALE = HEAD_DIM ** -0.5 * 1.4426950408889634


def _rms(x, g):
    return x * lax.rsqrt(jnp.mean(x * x, axis=-1, keepdims=True) + EPS) * g


def _gelu_tanh(x):
    return 0.5 * x * (1.0 + jnp.tanh(0.7978845608028654 * (x + 0.044715 * (x * x * x))))


def _dot_bf16(x, w):
    return jnp.dot(x.astype(BF16), w, preferred_element_type=F32)


def _dot_nt(a, b):
    return lax.dot_general(a, b, (((1,), (1,)), ((), ())), preferred_element_type=F32)


def _resident(shape):
    return pl.BlockSpec(shape, lambda *_: (0,) * len(shape), pipeline_mode=pl.Buffered(1))


def _inproj_kernel(x_ref, pos_ref, g_ref, wq_ref, wkc_ref, wkv_ref, wgn_ref, frq_ref, sg0_ref, sg1_ref, sg2_ref, one_ref,
                   q_ref, kc_ref, vc_ref, ks_ref, vs_ref, kw_ref, vw_ref, gn_ref):
    h = _rms(x_ref[...], g_ref[...])
    hb = h.astype(BF16)
    ang_t = frq_ref[...] * pos_ref[0].astype(F32)

    def spread(table_t, e_ref):
        hi = table_t.astype(BF16)
        rest = table_t - hi.astype(F32)
        mid = rest.astype(BF16)
        lo = (rest - mid.astype(F32)).astype(BF16)
        return lax.dot_general(jnp.concatenate([hi, mid, lo], axis=0), e_ref[...], (((0,), (0,)), ((), ())),
                               preferred_element_type=F32)

    cos_t = jnp.cos(ang_t)
    sin_t = jnp.sin(ang_t)
    c = spread(cos_t, sg0_ref) + one_ref[...]
    s1 = spread(sin_t, sg1_ref)
    s2 = spread(sin_t, sg2_ref)

    def rope(v):
        return v * c + pltpu.roll(v, LANES - ROT_DIM // 2, 1) * s1 + pltpu.roll(v, ROT_DIM // 2, 1) * s2

    q = jnp.dot(hb, wq_ref[...], preferred_element_type=F32)
    for j in range(NSA_WIDTH // LANES):
        q_ref[:, j * LANES:(j + 1) * LANES] = rope(q[:, j * LANES:(j + 1) * LANES]) * Q_SCALE
    kc_ref[...] = rope(jnp.dot(hb, wkc_ref[...], preferred_element_type=F32)).astype(BF16)
    kv = jnp.dot(hb, wkv_ref[...], preferred_element_type=F32)
    vc_ref[...] = kv[:, 0:128].astype(BF16)
    ks_ref[...] = rope(kv[:, 128:256]).astype(BF16)
    vs_ref[...] = kv[:, 256:384].T.astype(BF16)
    kw_ref[...] = rope(kv[:, 384:512]).astype(BF16)
    vw_ref[...] = kv[:, 512:640].T.astype(BF16)
    gn_ref[...] = jax.nn.sigmoid(jnp.dot(hb, wgn_ref[...], preferred_element_type=F32))


def _in_proj(x2, pos, g, wq, wkc, wkv, wgn, tm=512):
    T = x2.shape[0]
    half = ROT_DIM // 2
    d = np.arange(LANES) % HEAD_DIM
    inv_freq = ROPE_THETA ** (-jnp.arange(half, dtype=F32) * (2.0 / ROT_DIM))
    frq = inv_freq[:, None]
    j = np.arange(half)[:, None]
    stack3 = lambda m: jnp.asarray(np.tile(m.astype(np.float32), (3, 1)), dtype=BF16)
    sg0 = stack3((d[None, :] < ROT_DIM) & (d[None, :] % half == j))
    sg1 = stack3(-((d[None, :] < half) & (d[None, :] == j)).astype(np.float32))
    sg2 = stack3((d[None, :] >= half) & (d[None, :] < ROT_DIM) & (d[None, :] - half == j))
    one = np.where(d >= ROT_DIM, 1.0, 0.0).astype(np.float32)[None, :]
    row = lambda w: pl.BlockSpec((tm, w), lambda i: (i, 0))
    col = pl.BlockSpec((128, tm), lambda i: (0, i))
    out_shape = [jax.ShapeDtypeStruct((T, NSA_WIDTH), F32),
                 jax.ShapeDtypeStruct((T, 128), BF16), jax.ShapeDtypeStruct((T, 128), BF16),
                 jax.ShapeDtypeStruct((T, 128), BF16), jax.ShapeDtypeStruct((128, T), BF16),
                 jax.ShapeDtypeStruct((T, 128), BF16), jax.ShapeDtypeStruct((128, T), BF16),
                 jax.ShapeDtypeStruct((T, 128), F32)]
    return pl.pallas_call(
        _inproj_kernel,
        grid=(T // tm,),
        in_specs=[row(D_MODEL), pl.BlockSpec((1, 1, tm), lambda i: (i, 0, 0)), _resident((1, D_MODEL)),
                  _resident(wq.shape), _resident(wkc.shape), _resident(wkv.shape), _resident(wgn.shape),
                  _resident((half, 1)), _resident((3 * half, LANES)), _resident((3 * half, LANES)),
                  _resident((3 * half, LANES)), _resident((1, LANES))],
        out_specs=[row(NSA_WIDTH), row(128), row(128), row(128), col, row(128), col, row(128)],
        out_shape=out_shape,
        compiler_params=pltpu.CompilerParams(dimension_semantics=("parallel",), vmem_limit_bytes=VMEM_LIMIT),
        name="in_proj",
    )(x2, pos.reshape(T // tm, 1, tm), g, wq, wkc, wkv, wgn, frq, sg0, sg1, sg2, jnp.asarray(one))


def _compress_kernel(c_ref, pet_ref, peb_ref, wt_ref, wb_ref, w2_ref, o_ref):
    c = c_ref[0].astype(F32)
    a = _dot_bf16(c + pet_ref[...], wt_ref[...])
    b = _dot_bf16(c + peb_ref[...], wb_ref[...])
    n = c.shape[0]
    hid = a + pltpu.roll(b, n - 1, 0)
    o_ref[0] = _dot_bf16(_gelu_tanh(hid), w2_ref[...])


def _compress(kv, pe, w1, w2):
    B, S, _ = kv.shape
    n = S // CMP_STRIDE
    G = NSA_KV_GROUPS
    half = CMP_BLOCK // 2
    eye = jnp.eye(G, dtype=F32)

    def blockdiag_w1(w):
        w = w.reshape(half, 1, HEAD_DIM, 1, CMP_HIDDEN) * eye.reshape(1, G, 1, G, 1)
        return w.reshape(half * G * HEAD_DIM, G * CMP_HIDDEN)

    wt = blockdiag_w1(w1[:half * HEAD_DIM]).astype(BF16)
    wb = blockdiag_w1(w1[half * HEAD_DIM:]).astype(BF16)
    w2b = ((w2.reshape(1, CMP_HIDDEN, 1, HEAD_DIM) * eye.reshape(G, 1, G, 1))
           .reshape(G * CMP_HIDDEN, G * HEAD_DIM).astype(BF16))
    pet = jnp.broadcast_to(pe[:half, None, :], (half, G, HEAD_DIM)).reshape(1, half * G * HEAD_DIM)
    peb = jnp.broadcast_to(pe[half:, None, :], (half, G, HEAD_DIM)).reshape(1, half * G * HEAD_DIM)
    c = kv.reshape(B, n, CMP_STRIDE * 128)
    return pl.pallas_call(
        _compress_kernel,
        grid=(B,),
        in_specs=[pl.BlockSpec((1, n, CMP_STRIDE * 128), lambda b: (b, 0, 0)),
                  _resident(pet.shape), _resident(peb.shape), _resident(wt.shape), _resident(wb.shape),
                  _resident(w2b.shape)],
        out_specs=pl.BlockSpec((1, n, 128), lambda b: (b, 0, 0)),
        out_shape=jax.ShapeDtypeStruct((B, n, 128), F32),
        compiler_params=pltpu.CompilerParams(dimension_semantics=("parallel",), vmem_limit_bytes=VMEM_LIMIT),
        name="compress",
    )(c, pet, peb, wt, wb, w2b)


def _memkv_kernel(m_ref, g_ref, w_ref, o_ref):
    mn = _rms(m_ref[0], g_ref[...])
    o_ref[0] = jnp.dot(mn.astype(BF16), w_ref[...], preferred_element_type=F32).astype(BF16)


def _mem_kv(mem, g, w):
    B, M, _ = mem.shape
    return pl.pallas_call(
        _memkv_kernel,
        grid=(B,),
        in_specs=[pl.BlockSpec((1, M, D_MODEL), lambda b: (b, 0, 0)), _resident((1, D_MODEL)),
                  _resident(w.shape)],
        out_specs=pl.BlockSpec((1, M, 2 * XA_WIDTH), lambda b: (b, 0, 0)),
        out_shape=jax.ShapeDtypeStruct((B, M, 2 * XA_WIDTH), BF16),
        compiler_params=pltpu.CompilerParams(dimension_semantics=("parallel",), vmem_limit_bytes=VMEM_LIMIT),
        name="mem_kv",
    )(mem, g, w)


def _nsa_kernel(q_ref, kcc_ref, vcc_ref, ks_ref, vst_ref, kw_ref, vwt_ref, gn_ref, ovt_ref, blk_ref, o_ref,
                qat_ref, oct_ref, ost_ref, owt_ref, m_ref, l_ref, ap_ref, pvp_ref, over_ref, imp_ref):
    qt = pl.program_id(1)
    q0 = qt * Q_TILE
    n_cmp = kcc_ref.shape[1]
    head_cols = [slice(h * Q_TILE, (h + 1) * Q_TILE) for h in range(NSA_HEADS)]
    group_cols = [slice(g * HEADS_PER_GROUP * Q_TILE, (g + 1) * HEADS_PER_GROUP * Q_TILE)
                  for g in range(NSA_KV_GROUPS)]
    per_head = lambda a, n: jnp.concatenate([a] * n, axis=1)

    def masked(s, bias):
        return jnp.concatenate([s[:, c:c + Q_TILE] + bias for c in range(0, s.shape[1], Q_TILE)], axis=1)

    def value_product(vt_ref, start, n, p, extra_rows=None):
        out = []
        for g in range(NSA_KV_GROUPS):
            lhs = [vt_ref[g * HEAD_DIM:(g + 1) * HEAD_DIM, pl.ds(start, n)], jnp.ones((16, n), BF16)]
            if extra_rows is not None:
                lhs.append(extra_rows)
            out.append(jnp.dot(jnp.concatenate(lhs, axis=0), p[:, group_cols[g]], preferred_element_type=F32))
        return jnp.concatenate(out, axis=1)

    zeros = jnp.zeros((HEAD_DIM, Q_TILE), F32)
    for j in range(NSA_HEADS // 2):
        t = q_ref[0, :, j * LANES:(j + 1) * LANES].T
        for k in range(2):
            h = 2 * j + k
            part = t[k * HEAD_DIM:(k + 1) * HEAD_DIM, :]
            pair = [part, zeros] if h // HEADS_PER_GROUP == 0 else [zeros, part]
            qat_ref[0:LANES, head_cols[h]] = jnp.concatenate(pair, axis=0).astype(BF16)

    def compressed_branch(n_rows):
        crow = lax.broadcasted_iota(jnp.int32, (n_rows, Q_TILE), 0)
        cq = lax.broadcasted_iota(jnp.int32, (n_rows, Q_TILE), 1) + q0
        valid_c = crow * CMP_STRIDE + (CMP_BLOCK - 1) <= cq
        bias_c = jnp.where(valid_c, 0.0, SCORE_FLOOR)
        seen_c = per_head(jnp.where(cq[0:1, :] >= CMP_BLOCK - 1, 1.0, 0.0), HEADS_PER_GROUP)
        kcc = kcc_ref[0, 0:n_rows, :].astype(BF16)
        vcct = vcc_ref[0, 0:n_rows, :].T.astype(BF16)
        ones_ov = jnp.concatenate([jnp.ones((16, n_rows), BF16), ovt_ref[:, 0:n_rows]], axis=0)
        for g in range(NSA_KV_GROUPS):
            cols = group_cols[g]
            s = masked(jnp.dot(kcc, qat_ref[0:LANES, cols], preferred_element_type=F32), bias_c)
            p = jnp.exp2(s - jnp.max(s, axis=0, keepdims=True)).astype(BF16)
            lhs = jnp.concatenate([vcct[g * HEAD_DIM:(g + 1) * HEAD_DIM], ones_ov], axis=0)
            prod = jnp.dot(lhs, p, preferred_element_type=F32)
            norm = seen_c / jnp.maximum(prod[HEAD_DIM:HEAD_DIM + 1], jnp.finfo(F32).tiny)
            oct_ref[:, cols] = prod[0:HEAD_DIM] * norm
            imp_heads = prod[HEAD_DIM + 16:HEAD_DIM + 16 + LANES] * norm
            imp = imp_heads[:, 0:Q_TILE]
            for k in range(1, HEADS_PER_GROUP):
                imp = imp + imp_heads[:, k * Q_TILE:(k + 1) * Q_TILE]
            imp_ref[g] = imp

    row_steps = list(range(LANES, n_cmp + 1, LANES))
    for k, n_rows in enumerate(row_steps):
        in_step = (qt * (Q_TILE // CMP_STRIDE) + Q_TILE // CMP_STRIDE - 1) // LANES == k
        pl.when(in_step)(functools.partial(compressed_branch, n_rows))

    brow = lax.broadcasted_iota(jnp.int32, (LANES, Q_TILE), 0)
    cur = (lax.broadcasted_iota(jnp.int32, (LANES, Q_TILE), 1) + q0) // SEL_BLOCK
    forced = (brow == 0) | (brow == cur) | (brow == cur - 1)
    brow_f = brow.astype(F32)
    for g in range(NSA_KV_GROUPS):
        imp = imp_ref[g]
        v = jnp.where(brow > cur, -1.0, jnp.where(forced, -2.0, imp))
        for _ in range(SEL_TOPK - 3):
            mx = jnp.max(v, axis=0, keepdims=True)
            first = jnp.min(jnp.where(v == mx, brow_f, float(LANES)), axis=0, keepdims=True)
            v = jnp.where(brow_f == first, -2.0, v)
        neg = jnp.where(v == -2.0, 0.0, MASK_NEG).astype(BF16)
        for h in range(HEADS_PER_GROUP):
            qat_ref[LANES:2 * LANES, head_cols[g * HEADS_PER_GROUP + h]] = neg

    def scores(start, n):
        ka = jnp.concatenate([ks_ref[0, pl.ds(start, n), :], blk_ref[pl.ds(start, n), :]], axis=1)
        return jnp.dot(ka, qat_ref[...], preferred_element_type=F32)

    def online_update(s, start, n):
        m_old = m_ref[...]
        m_new = jnp.maximum(m_old, jnp.max(s, axis=0, keepdims=True))
        alpha = jnp.exp2(m_old - m_new)
        pv = value_product(vst_ref, start, n, jnp.exp2(s - m_new).astype(BF16))
        ost_ref[...] = alpha * ost_ref[...] + pv[0:HEAD_DIM]
        l_ref[...] = alpha * l_ref[...] + pv[HEAD_DIM:HEAD_DIM + 1]
        m_ref[...] = m_new

    qd = pl.multiple_of(q0, Q_TILE)
    drow = lax.broadcasted_iota(jnp.int32, (Q_TILE, Q_TILE), 0)
    dcol = lax.broadcasted_iota(jnp.int32, (Q_TILE, Q_TILE), 1)
    bias_d = jnp.where(drow <= dcol, 0.0, SCORE_FLOOR)
    head = q0 % SEL_KEY_TILE
    n_full = q0 // SEL_KEY_TILE
    head_sizes = range(Q_TILE, SEL_KEY_TILE, Q_TILE)
    tile_start = lambda j: pl.multiple_of(head + j * SEL_KEY_TILE, Q_TILE)

    def start_with_diagonal():
        l_ref[...] = jnp.zeros(l_ref.shape, F32)
        ost_ref[...] = jnp.zeros(ost_ref.shape, F32)
        m_ref[...] = jnp.full(m_ref.shape, SCORE_FLOOR, F32)
        online_update(masked(jnp.dot(ks_ref[0, pl.ds(qd, Q_TILE), :], qat_ref[0:LANES, :],
                                     preferred_element_type=F32), bias_d), qd, Q_TILE)

    start_with_diagonal()
    pvp_ref[...] = jnp.zeros(pvp_ref.shape, F32)
    ap_ref[...] = jnp.ones(ap_ref.shape, F32)
    over_ref[...] = jnp.full(over_ref.shape, SCORE_FLOOR, F32)

    def fold_parked():
        alpha = ap_ref[...]
        ost_ref[...] = alpha * (ost_ref[...] + pvp_ref[0:HEAD_DIM, :])
        l_ref[...] = alpha * (l_ref[...] + pvp_ref[HEAD_DIM:HEAD_DIM + 1, :])

    def lazy_step(start, n):
        fold_parked()
        s = scores(start, n)
        m_old = m_ref[...]
        pvp_ref[...] = value_product(vst_ref, start, n, jnp.exp2(s - m_old).astype(BF16))
        t_max = jnp.max(s, axis=0, keepdims=True)
        m_new = jnp.maximum(m_old, t_max)
        over_ref[...] = jnp.maximum(over_ref[...], t_max - m_old)
        ap_ref[...] = jnp.exp2(m_old - m_new)
        m_ref[...] = m_new

    def lazy_pair(j, carry):
        lazy_step(tile_start(2 * j), SEL_KEY_TILE)
        lazy_step(tile_start(2 * j + 1), SEL_KEY_TILE)
        return carry

    lax.fori_loop(0, n_full // 2, lazy_pair, 0)
    pl.when(n_full % 2 == 1)(lambda: lazy_step(tile_start(n_full - 1), SEL_KEY_TILE))
    for n in head_sizes:
        pl.when(head == n)(functools.partial(lazy_step, 0, n))
    fold_parked()

    @pl.when(jnp.max(over_ref[...]) > LAZY_MAX_HEADROOM)
    def _():
        start_with_diagonal()

        def textbook_tile(j, carry):
            online_update(scores(tile_start(j), SEL_KEY_TILE), tile_start(j), SEL_KEY_TILE)
            return carry

        lax.fori_loop(0, n_full, textbook_tile, 0)
        for n in head_sizes:
            pl.when(head == n)(lambda n=n: online_update(scores(0, n), 0, n))

    wlen = WINDOW + Q_TILE
    w0 = pl.multiple_of(jnp.maximum(q0 - WINDOW, 0), Q_TILE)
    kidx = w0 + lax.broadcasted_iota(jnp.int32, (wlen, Q_TILE), 0)
    tqw = q0 + lax.broadcasted_iota(jnp.int32, (wlen, Q_TILE), 1)
    bias_w = jnp.where((kidx <= tqw) & (tqw - kidx < WINDOW), 0.0, SCORE_FLOOR)
    s = masked(jnp.dot(kw_ref[0, pl.ds(w0, wlen), :], qat_ref[0:LANES, :], preferred_element_type=F32), bias_w)
    o = value_product(vwt_ref, w0, wlen, jnp.exp2(s - jnp.max(s, axis=0, keepdims=True)).astype(BF16))
    owt_ref[...] = o[0:HEAD_DIM] / o[HEAD_DIM:HEAD_DIM + 1]

    gt = gn_ref[0].T
    for j in range(NSA_HEADS // 2):
        parts = []
        for h in (2 * j, 2 * j + 1):
            cols = head_cols[h]
            o_sel = ost_ref[:, cols] / l_ref[:, cols]
            parts.append(gt[3 * h:3 * h + 1, :] * oct_ref[:, cols] + gt[3 * h + 1:3 * h + 2, :] * o_sel
                         + gt[3 * h + 2:3 * h + 3, :] * owt_ref[:, cols])
        o_ref[0, :, j * LANES:(j + 1) * LANES] = jnp.concatenate(parts, axis=0).T.astype(o_ref.dtype)


def _nsa(q, kcc, vcc, ks, vst, kw, vwt, gn):
    B, S, _ = q.shape
    n_cmp = kcc.shape[1]
    n_sel = S // SEL_BLOCK
    assert n_sel <= LANES and n_cmp % LANES == 0 and S % SEL_KEY_TILE == 0 and S >= WINDOW + Q_TILE
    ci = np.arange(n_cmp)[None, :] * CMP_STRIDE
    sj = np.arange(LANES)[:, None] * SEL_BLOCK
    overlap_t = ((ci < sj + SEL_BLOCK) & (ci + CMP_BLOCK > sj)).astype(np.float32)
    block_onehot = (np.arange(S)[:, None] // SEL_BLOCK == np.arange(LANES)[None, :])
    block_onehot = jnp.asarray(block_onehot.astype(np.float32), dtype=BF16)
    cols = NSA_HEADS * Q_TILE
    full = lambda n, w: pl.BlockSpec((1, n, w), lambda b, i: (b, 0, 0))
    full_t = pl.BlockSpec((128, S), lambda b, i: (0, b))
    tile = lambda w: pl.BlockSpec((1, Q_TILE, w), lambda b, i: (b, i, 0))
    return pl.pallas_call(
        _nsa_kernel,
        grid=(B, S // Q_TILE),
        in_specs=[tile(NSA_WIDTH), full(n_cmp, 128), full(n_cmp, 128), full(S, 128), full_t,
                  full(S, 128), full_t, tile(128), _resident(overlap_t.shape), _resident((S, LANES))],
        out_specs=tile(NSA_WIDTH),
        out_shape=jax.ShapeDtypeStruct((B, S, NSA_WIDTH), BF16),
        scratch_shapes=[pltpu.VMEM((2 * LANES, cols), BF16),
                        pltpu.VMEM((HEAD_DIM, cols), F32), pltpu.VMEM((HEAD_DIM, cols), F32),
                        pltpu.VMEM((HEAD_DIM, cols), F32), pltpu.VMEM((1, cols), F32),
                        pltpu.VMEM((1, cols), F32), pltpu.VMEM((1, cols), F32),
                        pltpu.VMEM((HEAD_DIM + 16, cols), F32), pltpu.VMEM((1, cols), F32),
                        pltpu.VMEM((NSA_KV_GROUPS, LANES, Q_TILE), F32)],
        compiler_params=pltpu.CompilerParams(dimension_semantics=("parallel", "parallel"),
                                             vmem_limit_bytes=VMEM_LIMIT),
        name="nsa",
    )(q, kcc, vcc, ks, vst, kw, vwt, gn, jnp.asarray(overlap_t, dtype=BF16), block_onehot)


POOL_HALO = 16


def _merge_kernel(x_ref, xh_ref, yn_ref, kvm_ref, g_ref, wpool_ref, wqx_ref, wgbr_ref, poolw_ref, pscale_ref,
                  wbp_ref, wbn_ref, wbx_ref, wout_ref, gpost_ref, o_ref, *, S):
    i = pl.program_id(0)
    tm = x_ref.shape[0]
    t0 = (i * tm) % S
    x = x_ref[...]
    g = g_ref[...]
    h = _rms(x, g)
    hb = h.astype(BF16)

    hh = _rms(xh_ref[...], g).astype(BF16)
    u_halo = jnp.dot(hh, wpool_ref[...], preferred_element_type=F32)
    u_halo = jnp.where(t0 > 0, u_halo, 0.0)
    u = jnp.dot(hb, wpool_ref[...], preferred_element_type=F32)
    ue = jnp.concatenate([u_halo, u], axis=0)
    trow = t0 + lax.broadcasted_iota(jnp.int32, (tm, 1), 0) + 1
    ypool = []
    for gi, w in enumerate(POOL_WINDOWS):
        acc = ue[:, gi * POOL_GROUP:(gi + 1) * POOL_GROUP]
        step = 1
        while step < w:
            acc = acc + pltpu.roll(acc, step, 0)
            step *= 2
        cnt = jnp.minimum(trow, w).astype(F32)
        p = acc[POOL_HALO:, :] / cnt - u[:, gi * POOL_GROUP:(gi + 1) * POOL_GROUP]
        ypool.append(jnp.dot(p.astype(BF16), poolw_ref[gi], preferred_element_type=F32))
    ypool = jnp.concatenate(ypool, axis=1) * pscale_ref[...]

    qx = jnp.dot(hb, wqx_ref[...], preferred_element_type=F32)
    ymem = []
    for hd in range(XA_HEADS):
        km = kvm_ref[0, :, hd * XA_HEAD_DIM:(hd + 1) * XA_HEAD_DIM]
        vm = kvm_ref[0, :, XA_WIDTH + hd * XA_HEAD_DIM:XA_WIDTH + (hd + 1) * XA_HEAD_DIM]
        s = _dot_nt(qx[:, hd * XA_HEAD_DIM:(hd + 1) * XA_HEAD_DIM].astype(BF16), km) * (XA_HEAD_DIM ** -0.5)
        e = jnp.exp(s - jnp.max(s, axis=-1, keepdims=True))
        p = e / jnp.sum(e, axis=-1, keepdims=True)
        ymem.append(jnp.dot(p.astype(BF16), vm, preferred_element_type=F32))
    ymem = jnp.concatenate(ymem, axis=1)

    gbr = jax.nn.sigmoid(jnp.dot(hb, wgbr_ref[...], preferred_element_type=F32))
    y = (gbr[:, 0:D_MODEL] * jnp.dot(ypool.astype(BF16), wbp_ref[...], preferred_element_type=F32)
         + gbr[:, D_MODEL:2 * D_MODEL] * jnp.dot(yn_ref[...], wbn_ref[...], preferred_element_type=F32)
         + gbr[:, 2 * D_MODEL:3 * D_MODEL] * jnp.dot(ymem.astype(BF16), wbx_ref[...], preferred_element_type=F32))
    o = jnp.dot(y.astype(BF16), wout_ref[...], preferred_element_type=F32)
    o_ref[...] = x + _rms(o, gpost_ref[...])


def _merge(x2, ynsa, kvm, g, wpool, wqx, wgbr, poolw, pscale, wbp, wbn, wbx, wout, gpost, S, tm=512):
    T = x2.shape[0]
    M = kvm.shape[1]
    row = lambda w: pl.BlockSpec((tm, w), lambda i: (i, 0))
    halo = pl.BlockSpec((POOL_HALO, D_MODEL), lambda i: (jnp.maximum(i * (tm // POOL_HALO) - 1, 0), 0))
    return pl.pallas_call(
        functools.partial(_merge_kernel, S=S),
        grid=(T // tm,),
        in_specs=[row(D_MODEL), halo, row(NSA_WIDTH),
                  pl.BlockSpec((1, M, 2 * XA_WIDTH), lambda i: ((i * tm) // S, 0, 0)),
                  _resident((1, D_MODEL)), _resident(wpool.shape), _resident(wqx.shape), _resident(wgbr.shape),
                  _resident(poolw.shape), _resident(pscale.shape), _resident(wbp.shape), _resident(wbn.shape),
                  _resident(wbx.shape), _resident(wout.shape), _resident((1, D_MODEL))],
        out_specs=row(D_MODEL),
        out_shape=jax.ShapeDtypeStruct((T, D_MODEL), F32),
        compiler_params=pltpu.CompilerParams(dimension_semantics=("parallel",), vmem_limit_bytes=VMEM_LIMIT),
        name="merge",
    )(x2, x2, ynsa, kvm, g, wpool, wqx, wgbr, poolw, pscale, wbp, wbn, wbx, wout, gpost)


FFN_HALO = 8
FFN_CHUNK = 2816


def _ffn_kernel(x_ref, xh_ref, g_ref, wup_ref, cw_ref, cb_ref, wdn_ref, gpost_ref, o_ref, *, S):
    i = pl.program_id(0)
    tm = x_ref.shape[0]
    t0 = (i * tm) % S
    x = x_ref[...]
    g = g_ref[...]
    hh = jnp.where(t0 > 0, _rms(xh_ref[...], g), 0.0)
    he = jnp.concatenate([hh, _rms(x, g)], axis=0).astype(BF16)

    def conv(col0):
        u = jnp.dot(he, wup_ref[:, col0:col0 + FFN_CHUNK], preferred_element_type=F32)
        c = cb_ref[:, col0:col0 + FFN_CHUNK] + cw_ref[CONV_WIDTH - 1:CONV_WIDTH, col0:col0 + FFN_CHUNK] * u
        for k in range(1, CONV_WIDTH):
            tap = cw_ref[CONV_WIDTH - 1 - k:CONV_WIDTH - k, col0:col0 + FFN_CHUNK]
            c = c + tap * pltpu.roll(u, k, 0)
        return c[FFN_HALO:, :]

    f = jnp.zeros((tm, D_MODEL), F32)
    for j in range(D_FF // FFN_CHUNK):
        act = _gelu_tanh(conv(j * FFN_CHUNK)) * conv(D_FF + j * FFN_CHUNK)
        f = f + jnp.dot(act.astype(BF16), wdn_ref[j * FFN_CHUNK:(j + 1) * FFN_CHUNK, :],
                        preferred_element_type=F32)
    o_ref[...] = x + _rms(f, gpost_ref[...])


def _ffn(x2, g, wup, cw, cb, wdn, gpost, S, tm=512):
    T = x2.shape[0]
    row = pl.BlockSpec((tm, D_MODEL), lambda i: (i, 0))
    halo = pl.BlockSpec((FFN_HALO, D_MODEL), lambda i: (jnp.maximum(i * (tm // FFN_HALO) - 1, 0), 0))
    return pl.pallas_call(
        functools.partial(_ffn_kernel, S=S),
        grid=(T // tm,),
        in_specs=[row, halo, _resident((1, D_MODEL)), _resident(wup.shape), _resident(cw.shape),
                  _resident(cb.shape), _resident(wdn.shape), _resident((1, D_MODEL))],
        out_specs=row,
        out_shape=jax.ShapeDtypeStruct((T, D_MODEL), F32),
        compiler_params=pltpu.CompilerParams(dimension_semantics=("parallel",), vmem_limit_bytes=VMEM_LIMIT),
        name="ffn",
    )(x2, x2, g, wup, cw, cb, wdn, gpost)


def _layer(x2, mem, pos2, B, S, pre_mix_g, w_in, pool_w, pool_scale, cmp_pe, cmp_w1, cmp_w2, mem_norm_g,
           w_mem_kv, w_br_pool, w_br_nsa, w_br_xa, w_out, post_mix_g, pre_ffn_g, w_up, conv_w, conv_b,
           w_down, post_ffn_g):
    o = np.cumsum((0,) + IN_SIZES)
    w_pool, w_q, w_kv, w_gn, w_qx, w_gbr = (w_in[:, o[k]:o[k + 1]] for k in range(6))
    w_gn = jnp.pad(w_gn, ((0, 0), (0, LANES - w_gn.shape[1])))
    vec = lambda a: a.reshape(1, -1)
    q, kc, vc, ks, vst, kw, vwt, gn = _in_proj(x2, pos2, vec(pre_mix_g), w_q.astype(BF16), w_kv[:, :128].astype(BF16),
                                               w_kv[:, 128:].astype(BF16), w_gn.astype(BF16))
    b3 = lambda a: a.reshape(B, S, a.shape[-1])
    kcc = _compress(b3(kc), cmp_pe[0], cmp_w1[0], cmp_w2[0])
    vcc = _compress(b3(vc), cmp_pe[1], cmp_w1[1], cmp_w2[1])
    ynsa = _nsa(b3(q), kcc, vcc, b3(ks), vst, b3(kw), vwt, b3(gn))
    kvm = _mem_kv(mem, vec(mem_norm_g), w_mem_kv.astype(BF16))
    x2 = _merge(x2, ynsa.reshape(B * S, NSA_WIDTH), kvm, vec(pre_mix_g), w_pool.astype(BF16), w_qx.astype(BF16),
                w_gbr.astype(BF16), pool_w.astype(BF16), vec(pool_scale), w_br_pool.astype(BF16),
                w_br_nsa.astype(BF16), w_br_xa.astype(BF16), w_out.astype(BF16), vec(post_mix_g), S)
    return _ffn(x2, vec(pre_ffn_g), w_up.astype(BF16), conv_w, vec(conv_b), w_down.astype(BF16),
                vec(post_ffn_g), S)


def kernel(x, mem, positions, pre_mix_g, w_in, pool_w, pool_scale, cmp_pe, cmp_w1, cmp_w2, mem_norm_g, w_mem_kv,
           w_br_pool, w_br_nsa, w_br_xa, w_out, post_mix_g, pre_ffn_g, w_up, conv_w, conv_b, w_down, post_ffn_g):
    B, S, D = x.shape
    x2 = x.reshape(B * S, D)
    pos2 = positions.reshape(B * S)
    for l in range(pre_mix_g.shape[0]):
        x2 = _layer(x2, mem, pos2, B, S, pre_mix_g[l], w_in[l], pool_w[l], pool_scale[l], cmp_pe[l], cmp_w1[l],
                    cmp_w2[l], mem_norm_g[l], w_mem_kv[l], w_br_pool[l], w_br_nsa[l], w_br_xa[l], w_out[l],
                    post_mix_g[l], pre_ffn_g[l], w_up[l], conv_w[l], conv_b[l], w_down[l], post_ffn_g[l])
    return x2.reshape(B, S, D)
```

```python
import functools

import numpy as np
import jax
import jax.numpy as jnp
from jax import lax
from jax.experimental import pallas as pl
from jax.experimental.pallas import tpu as pltpu

F32 = jnp.float32
BF16 = jnp.bfloat16

D_MODEL = 1024
EPS = 1e-6
POOL_WINDOWS = (2, 4, 8, 16)
POOL_GROUP = 128
POOL_WIDTH = POOL_GROUP * len(POOL_WINDOWS)
NSA_HEADS = 16
NSA_KV_GROUPS = 2
HEADS_PER_GROUP = NSA_HEADS // NSA_KV_GROUPS
HEAD_DIM = 64
NSA_WIDTH = NSA_HEADS * HEAD_DIM
NSA_KV_WIDTH = NSA_KV_GROUPS * HEAD_DIM
CMP_BLOCK = 32
CMP_STRIDE = 16
CMP_HIDDEN = 256
SEL_BLOCK = 64
SEL_TOPK = 16
WINDOW = 512
ROPE_THETA = 500000.0
ROT_DIM = HEAD_DIM // 4
XA_HEADS = 4
XA_HEAD_DIM = 128
XA_WIDTH = XA_HEADS * XA_HEAD_DIM
N_BRANCHES = 3
D_FF = 2816
CONV_WIDTH = 3
IN_SIZES = (POOL_WIDTH, NSA_WIDTH, 6 * NSA_KV_WIDTH, 3 * NSA_HEADS, XA_WIDTH, N_BRANCHES * D_MODEL)

SEL_FORCED = 3
LANES = 128
ONES_ROWS = 16
Q_TILE = 128
SEL_KEY_TILE = 512
MASK_NEG = -1e9
SCORE_FLOOR = -1e30
LAZY_MAX_HEADROOM = 64.0
VMEM_LIMIT = 56 * 1024 * 1024
Q_SCALE = HEAD_DIM ** -0.5 * 1.4426950408889634


def _rms(x, g):
    return x * lax.rsqrt(jnp.mean(x * x, axis=-1, keepdims=True) + EPS) * g


def _gelu_tanh(x):
    c = 0.7978845608028654
    return x * (0.5 + 0.5 * jnp.tanh(x * (c + (0.044715 * c) * (x * x))))


def _dot_bf16(x, w):
    return jnp.dot(x.astype(BF16), w, preferred_element_type=F32)


def _dot_nt(a, b):
    return lax.dot_general(a, b, (((1,), (1,)), ((), ())), preferred_element_type=F32)


def _resident(shape):
    return pl.BlockSpec(shape, lambda *_: (0,) * len(shape), pipeline_mode=pl.Buffered(1))


def _inproj_kernel(x_ref, pos_ref, g_ref, wq_ref, wkc_ref, wkv_ref, wgn_ref, frq_ref, sg0_ref, sg1_ref, sg2_ref, one_ref,
                   q_ref, kc_ref, vc_ref, ks_ref, vs_ref, kw_ref, vw_ref, gn_ref):
    h = _rms(x_ref[...], g_ref[...])
    hb = h.astype(BF16)
    ang_t = frq_ref[...] * pos_ref[0].astype(F32)

    def spread(table_t, e_ref):
        hi = table_t.astype(BF16)
        rest = table_t - hi.astype(F32)
        mid = rest.astype(BF16)
        lo = (rest - mid.astype(F32)).astype(BF16)
        return lax.dot_general(jnp.concatenate([hi, mid, lo], axis=0), e_ref[...], (((0,), (0,)), ((), ())),
                               preferred_element_type=F32)

    cos_t = jnp.cos(ang_t)
    sin_t = jnp.sin(ang_t)
    c = spread(cos_t, sg0_ref) + one_ref[...]
    s1 = spread(sin_t, sg1_ref)
    s2 = spread(sin_t, sg2_ref)

    def rope(v):
        return v * c + pltpu.roll(v, LANES - ROT_DIM // 2, 1) * s1 + pltpu.roll(v, ROT_DIM // 2, 1) * s2

    q = jnp.dot(hb, wq_ref[...], preferred_element_type=F32)
    for j in range(NSA_WIDTH // LANES):
        q_ref[:, j * LANES:(j + 1) * LANES] = rope(q[:, j * LANES:(j + 1) * LANES]) * Q_SCALE
    kc_ref[...] = rope(jnp.dot(hb, wkc_ref[...], preferred_element_type=F32)).astype(BF16)
    kv = jnp.dot(hb, wkv_ref[...], preferred_element_type=F32)
    vc_ref[...] = kv[:, 0:128].astype(BF16)
    ks_ref[...] = rope(kv[:, 128:256]).astype(BF16)
    vs_ref[...] = kv[:, 256:384].T.astype(BF16)
    kw_ref[...] = rope(kv[:, 384:512]).astype(BF16)
    vw_ref[...] = kv[:, 512:640].T.astype(BF16)
    gn_ref[...] = jax.nn.sigmoid(jnp.dot(hb, wgn_ref[...], preferred_element_type=F32))


def _in_proj(x2, pos, g, wq, wkc, wkv, wgn, tm=512):
    T = x2.shape[0]
    half = ROT_DIM // 2
    d = np.arange(LANES) % HEAD_DIM
    inv_freq = ROPE_THETA ** (-jnp.arange(half, dtype=F32) * (2.0 / ROT_DIM))
    frq = inv_freq[:, None]
    j = np.arange(half)[:, None]
    stack3 = lambda m: jnp.asarray(np.tile(m.astype(np.float32), (3, 1)), dtype=BF16)
    sg0 = stack3((d[None, :] < ROT_DIM) & (d[None, :] % half == j))
    sg1 = stack3(-((d[None, :] < half) & (d[None, :] == j)).astype(np.float32))
    sg2 = stack3((d[None, :] >= half) & (d[None, :] < ROT_DIM) & (d[None, :] - half == j))
    one = np.where(d >= ROT_DIM, 1.0, 0.0).astype(np.float32)[None, :]
    row = lambda w: pl.BlockSpec((tm, w), lambda i: (i, 0))
    col = pl.BlockSpec((128, tm), lambda i: (0, i))
    out_shape = [jax.ShapeDtypeStruct((T, NSA_WIDTH), F32),
                 jax.ShapeDtypeStruct((T, 128), BF16), jax.ShapeDtypeStruct((T, 128), BF16),
                 jax.ShapeDtypeStruct((T, 128), BF16), jax.ShapeDtypeStruct((128, T), BF16),
                 jax.ShapeDtypeStruct((T, 128), BF16), jax.ShapeDtypeStruct((128, T), BF16),
                 jax.ShapeDtypeStruct((T, 128), F32)]
    return pl.pallas_call(
        _inproj_kernel,
        grid=(T // tm,),
        in_specs=[row(D_MODEL), pl.BlockSpec((1, 1, tm), lambda i: (i, 0, 0)), _resident((1, D_MODEL)),
                  _resident(wq.shape), _resident(wkc.shape), _resident(wkv.shape), _resident(wgn.shape),
                  _resident((half, 1)), _resident((3 * half, LANES)), _resident((3 * half, LANES)),
                  _resident((3 * half, LANES)), _resident((1, LANES))],
        out_specs=[row(NSA_WIDTH), row(128), row(128), row(128), col, row(128), col, row(128)],
        out_shape=out_shape,
        compiler_params=pltpu.CompilerParams(dimension_semantics=("parallel",), vmem_limit_bytes=VMEM_LIMIT),
        name="in_proj",
    )(x2, pos.reshape(T // tm, 1, tm), g, wq, wkc, wkv, wgn, frq, sg0, sg1, sg2, jnp.asarray(one))


def _compress_kernel(c_ref, pet_ref, peb_ref, wt_ref, wb_ref, w2_ref, o_ref):
    c = c_ref[0].astype(F32)
    a = _dot_bf16(c + pet_ref[...], wt_ref[...])
    b = _dot_bf16(c + peb_ref[...], wb_ref[...])
    n = c.shape[0]
    hid = a + pltpu.roll(b, n - 1, 0)
    o_ref[0] = _dot_bf16(_gelu_tanh(hid), w2_ref[...])


def _compress(kv, pe, w1, w2):
    B, S, _ = kv.shape
    n = S // CMP_STRIDE
    G = NSA_KV_GROUPS
    half = CMP_BLOCK // 2
    eye = jnp.eye(G, dtype=F32)

    def blockdiag_w1(w):
        w = w.reshape(half, 1, HEAD_DIM, 1, CMP_HIDDEN) * eye.reshape(1, G, 1, G, 1)
        return w.reshape(half * G * HEAD_DIM, G * CMP_HIDDEN)

    wt = blockdiag_w1(w1[:half * HEAD_DIM]).astype(BF16)
    wb = blockdiag_w1(w1[half * HEAD_DIM:]).astype(BF16)
    w2b = ((w2.reshape(1, CMP_HIDDEN, 1, HEAD_DIM) * eye.reshape(G, 1, G, 1))
           .reshape(G * CMP_HIDDEN, G * HEAD_DIM).astype(BF16))
    pet = jnp.broadcast_to(pe[:half, None, :], (half, G, HEAD_DIM)).reshape(1, half * G * HEAD_DIM)
    peb = jnp.broadcast_to(pe[half:, None, :], (half, G, HEAD_DIM)).reshape(1, half * G * HEAD_DIM)
    c = kv.reshape(B, n, CMP_STRIDE * 128)
    return pl.pallas_call(
        _compress_kernel,
        grid=(B,),
        in_specs=[pl.BlockSpec((1, n, CMP_STRIDE * 128), lambda b: (b, 0, 0)),
                  _resident(pet.shape), _resident(peb.shape), _resident(wt.shape), _resident(wb.shape),
                  _resident(w2b.shape)],
        out_specs=pl.BlockSpec((1, n, 128), lambda b: (b, 0, 0)),
        out_shape=jax.ShapeDtypeStruct((B, n, 128), F32),
        compiler_params=pltpu.CompilerParams(dimension_semantics=("parallel",), vmem_limit_bytes=VMEM_LIMIT),
        name="compress",
    )(c, pet, peb, wt, wb, w2b)


def _memkv_kernel(m_ref, g_ref, w_ref, o_ref):
    mn = _rms(m_ref[0], g_ref[...])
    o_ref[0] = jnp.dot(mn.astype(BF16), w_ref[...], preferred_element_type=F32).astype(BF16)


def _mem_kv(mem, g, w):
    B, M, _ = mem.shape
    return pl.pallas_call(
        _memkv_kernel,
        grid=(B,),
        in_specs=[pl.BlockSpec((1, M, D_MODEL), lambda b: (b, 0, 0)), _resident((1, D_MODEL)),
                  _resident(w.shape)],
        out_specs=pl.BlockSpec((1, M, 2 * XA_WIDTH), lambda b: (b, 0, 0)),
        out_shape=jax.ShapeDtypeStruct((B, M, 2 * XA_WIDTH), BF16),
        compiler_params=pltpu.CompilerParams(dimension_semantics=("parallel",), vmem_limit_bytes=VMEM_LIMIT),
        name="mem_kv",
    )(mem, g, w)


def _nsa_kernel(q_ref, kcc_ref, vcc_ref, ks_ref, vst_ref, kw_ref, vwt_ref, gn_ref, ovt_ref, blk_ref, o_ref,
                qat_ref, oct_ref, ost_ref, owt_ref, m_ref, l_ref, ap_ref, pvp_ref, over_ref, imp_ref):
    qt = pl.program_id(1)
    q0 = qt * Q_TILE
    n_cmp = kcc_ref.shape[1]
    head_cols = [slice(h * Q_TILE, (h + 1) * Q_TILE) for h in range(NSA_HEADS)]
    group_cols = [slice(g * HEADS_PER_GROUP * Q_TILE, (g + 1) * HEADS_PER_GROUP * Q_TILE)
                  for g in range(NSA_KV_GROUPS)]
    per_head = lambda a, n: jnp.concatenate([a] * n, axis=1)

    def masked(s, bias):
        return jnp.concatenate([s[:, c:c + Q_TILE] + bias for c in range(0, s.shape[1], Q_TILE)], axis=1)

    def value_product(vt_ref, start, n, p):
        out = []
        for g in range(NSA_KV_GROUPS):
            lhs = jnp.concatenate([vt_ref[g * HEAD_DIM:(g + 1) * HEAD_DIM, pl.ds(start, n)],
                                   jnp.ones((ONES_ROWS, n), BF16)], axis=0)
            out.append(jnp.dot(lhs, p[:, group_cols[g]], preferred_element_type=F32))
        return jnp.concatenate(out, axis=1)

    zeros = jnp.zeros((HEAD_DIM, Q_TILE), F32)
    for j in range(NSA_HEADS // 2):
        t = q_ref[0, :, j * LANES:(j + 1) * LANES].T
        for k in range(2):
            h = 2 * j + k
            part = t[k * HEAD_DIM:(k + 1) * HEAD_DIM, :]
            pair = [part, zeros] if h // HEADS_PER_GROUP == 0 else [zeros, part]
            qat_ref[0:LANES, head_cols[h]] = jnp.concatenate(pair, axis=0).astype(BF16)

    def compressed_branch(n_rows):
        crow = lax.broadcasted_iota(jnp.int32, (n_rows, Q_TILE), 0)
        cq = lax.broadcasted_iota(jnp.int32, (n_rows, Q_TILE), 1) + q0
        valid_c = crow * CMP_STRIDE + (CMP_BLOCK - 1) <= cq
        bias_c = jnp.where(valid_c, 0.0, SCORE_FLOOR)
        seen_c = per_head(jnp.where(cq[0:1, :] >= CMP_BLOCK - 1, 1.0, 0.0), HEADS_PER_GROUP)
        kcc = kcc_ref[0, 0:n_rows, :].astype(BF16)
        vcct = vcc_ref[0, 0:n_rows, :].T.astype(BF16)
        ones_ov = jnp.concatenate([jnp.ones((ONES_ROWS, n_rows), BF16), ovt_ref[:, 0:n_rows]], axis=0)
        for g in range(NSA_KV_GROUPS):
            cols = group_cols[g]
            s = masked(jnp.dot(kcc, qat_ref[0:LANES, cols], preferred_element_type=F32), bias_c)
            p = jnp.exp2(s - jnp.max(s, axis=0, keepdims=True)).astype(BF16)
            lhs = jnp.concatenate([vcct[g * HEAD_DIM:(g + 1) * HEAD_DIM], ones_ov], axis=0)
            prod = jnp.dot(lhs, p, preferred_element_type=F32)
            norm = seen_c / jnp.maximum(prod[HEAD_DIM:HEAD_DIM + 1], jnp.finfo(F32).tiny)
            oct_ref[:, cols] = prod[0:HEAD_DIM] * norm
            imp_heads = prod[HEAD_DIM + ONES_ROWS:HEAD_DIM + ONES_ROWS + LANES] * norm
            imp = imp_heads[:, 0:Q_TILE]
            for k in range(1, HEADS_PER_GROUP):
                imp = imp + imp_heads[:, k * Q_TILE:(k + 1) * Q_TILE]
            imp_ref[g] = imp

    row_steps = list(range(LANES, n_cmp + 1, LANES))
    for k, n_rows in enumerate(row_steps):
        in_step = (qt * (Q_TILE // CMP_STRIDE) + Q_TILE // CMP_STRIDE - 1) // LANES == k
        pl.when(in_step)(functools.partial(compressed_branch, n_rows))

    brow = lax.broadcasted_iota(jnp.int32, (LANES, Q_TILE), 0)
    cur = (lax.broadcasted_iota(jnp.int32, (LANES, Q_TILE), 1) + q0) // SEL_BLOCK
    forced = (brow == 0) | (brow == cur) | (brow == cur - 1)
    brow_f = brow.astype(F32)
    for g in range(NSA_KV_GROUPS):
        v = jnp.where(brow > cur, -1.0, jnp.where(forced, -2.0, imp_ref[g]))
        for _ in range(SEL_TOPK - SEL_FORCED):
            mx = jnp.max(v, axis=0, keepdims=True)
            first = jnp.min(jnp.where(v == mx, brow_f, float(LANES)), axis=0, keepdims=True)
            v = jnp.where(brow_f == first, -2.0, v)
        neg = jnp.where(v == -2.0, 0.0, MASK_NEG).astype(BF16)
        for h in range(HEADS_PER_GROUP):
            qat_ref[LANES:2 * LANES, head_cols[g * HEADS_PER_GROUP + h]] = neg

    def scores(start, n):
        ka = jnp.concatenate([ks_ref[0, pl.ds(start, n), :], blk_ref[pl.ds(start, n), :]], axis=1)
        return jnp.dot(ka, qat_ref[...], preferred_element_type=F32)

    def online_update(s, start, n):
        m_old = m_ref[...]
        m_new = jnp.maximum(m_old, jnp.max(s, axis=0, keepdims=True))
        alpha = jnp.exp2(m_old - m_new)
        pv = value_product(vst_ref, start, n, jnp.exp2(s - m_new).astype(BF16))
        ost_ref[...] = alpha * ost_ref[...] + pv[0:HEAD_DIM]
        l_ref[...] = alpha * l_ref[...] + pv[HEAD_DIM:HEAD_DIM + 1]
        m_ref[...] = m_new

    qd = pl.multiple_of(q0, Q_TILE)
    drow = lax.broadcasted_iota(jnp.int32, (Q_TILE, Q_TILE), 0)
    dcol = lax.broadcasted_iota(jnp.int32, (Q_TILE, Q_TILE), 1)
    bias_d = jnp.where(drow <= dcol, 0.0, SCORE_FLOOR)
    head = q0 % SEL_KEY_TILE
    n_full = q0 // SEL_KEY_TILE
    head_sizes = range(Q_TILE, SEL_KEY_TILE, Q_TILE)
    tile_start = lambda j: pl.multiple_of(head + j * SEL_KEY_TILE, Q_TILE)

    def start_with_diagonal():
        l_ref[...] = jnp.zeros(l_ref.shape, F32)
        ost_ref[...] = jnp.zeros(ost_ref.shape, F32)
        m_ref[...] = jnp.full(m_ref.shape, SCORE_FLOOR, F32)
        online_update(masked(jnp.dot(ks_ref[0, pl.ds(qd, Q_TILE), :], qat_ref[0:LANES, :],
                                     preferred_element_type=F32), bias_d), qd, Q_TILE)

    start_with_diagonal()
    pvp_ref[...] = jnp.zeros(pvp_ref.shape, F32)
    ap_ref[...] = jnp.ones(ap_ref.shape, F32)
    over_ref[...] = jnp.full(over_ref.shape, SCORE_FLOOR, F32)

    def fold_parked():
        alpha = ap_ref[...]
        ost_ref[...] = alpha * (ost_ref[...] + pvp_ref[0:HEAD_DIM, :])
        l_ref[...] = alpha * (l_ref[...] + pvp_ref[HEAD_DIM:HEAD_DIM + 1, :])

    def lazy_step(start, n):
        fold_parked()
        s = scores(start, n)
        m_old = m_ref[...]
        pvp_ref[...] = value_product(vst_ref, start, n, jnp.exp2(s - m_old).astype(BF16))
        t_max = jnp.max(s, axis=0, keepdims=True)
        m_new = jnp.maximum(m_old, t_max)
        over_ref[...] = jnp.maximum(over_ref[...], t_max - m_old)
        ap_ref[...] = jnp.exp2(m_old - m_new)
        m_ref[...] = m_new

    def lazy_pair(j, carry):
        lazy_step(tile_start(2 * j), SEL_KEY_TILE)
        lazy_step(tile_start(2 * j + 1), SEL_KEY_TILE)
        return carry

    lax.fori_loop(0, n_full // 2, lazy_pair, 0)
    pl.when(n_full % 2 == 1)(lambda: lazy_step(tile_start(n_full - 1), SEL_KEY_TILE))
    for n in head_sizes:
        pl.when(head == n)(functools.partial(lazy_step, 0, n))
    fold_parked()

    @pl.when(jnp.max(over_ref[...]) > LAZY_MAX_HEADROOM)
    def _():
        start_with_diagonal()

        def textbook_tile(j, carry):
            online_update(scores(tile_start(j), SEL_KEY_TILE), tile_start(j), SEL_KEY_TILE)
            return carry

        lax.fori_loop(0, n_full, textbook_tile, 0)
        for n in head_sizes:
            pl.when(head == n)(lambda n=n: online_update(scores(0, n), 0, n))

    wlen = WINDOW + Q_TILE
    w0 = pl.multiple_of(jnp.maximum(q0 - WINDOW, 0), Q_TILE)
    kidx = w0 + lax.broadcasted_iota(jnp.int32, (wlen, Q_TILE), 0)
    tqw = q0 + lax.broadcasted_iota(jnp.int32, (wlen, Q_TILE), 1)
    bias_w = jnp.where((kidx <= tqw) & (tqw - kidx < WINDOW), 0.0, SCORE_FLOOR)
    s = masked(jnp.dot(kw_ref[0, pl.ds(w0, wlen), :], qat_ref[0:LANES, :], preferred_element_type=F32), bias_w)
    o = value_product(vwt_ref, w0, wlen, jnp.exp2(s - jnp.max(s, axis=0, keepdims=True)).astype(BF16))
    owt_ref[...] = o[0:HEAD_DIM] / o[HEAD_DIM:HEAD_DIM + 1]

    gt = gn_ref[0].T
    for j in range(NSA_HEADS // 2):
        parts = []
        for h in (2 * j, 2 * j + 1):
            cols = head_cols[h]
            o_sel = ost_ref[:, cols] / l_ref[:, cols]
            parts.append(gt[3 * h:3 * h + 1, :] * oct_ref[:, cols] + gt[3 * h + 1:3 * h + 2, :] * o_sel
                         + gt[3 * h + 2:3 * h + 3, :] * owt_ref[:, cols])
        o_ref[0, :, j * LANES:(j + 1) * LANES] = jnp.concatenate(parts, axis=0).T.astype(o_ref.dtype)


def _nsa(q, kcc, vcc, ks, vst, kw, vwt, gn):
    B, S, _ = q.shape
    n_cmp = kcc.shape[1]
    n_sel = S // SEL_BLOCK
    assert n_sel <= LANES and n_cmp % LANES == 0 and S % SEL_KEY_TILE == 0 and S >= WINDOW + Q_TILE
    ci = np.arange(n_cmp)[None, :] * CMP_STRIDE
    sj = np.arange(LANES)[:, None] * SEL_BLOCK
    overlap_t = ((ci < sj + SEL_BLOCK) & (ci + CMP_BLOCK > sj)).astype(np.float32)
    block_onehot = (np.arange(S)[:, None] // SEL_BLOCK == np.arange(LANES)[None, :])
    block_onehot = jnp.asarray(block_onehot.astype(np.float32), dtype=BF16)
    cols = NSA_HEADS * Q_TILE
    full = lambda n, w: pl.BlockSpec((1, n, w), lambda b, i: (b, 0, 0))
    full_t = pl.BlockSpec((128, S), lambda b, i: (0, b))
    tile = lambda w: pl.BlockSpec((1, Q_TILE, w), lambda b, i: (b, i, 0))
    return pl.pallas_call(
        _nsa_kernel,
        grid=(B, S // Q_TILE),
        in_specs=[tile(NSA_WIDTH), full(n_cmp, 128), full(n_cmp, 128), full(S, 128), full_t,
                  full(S, 128), full_t, tile(128), _resident(overlap_t.shape), _resident((S, LANES))],
        out_specs=tile(NSA_WIDTH),
        out_shape=jax.ShapeDtypeStruct((B, S, NSA_WIDTH), BF16),
        scratch_shapes=[pltpu.VMEM((2 * LANES, cols), BF16),
                        pltpu.VMEM((HEAD_DIM, cols), F32), pltpu.VMEM((HEAD_DIM, cols), F32),
                        pltpu.VMEM((HEAD_DIM, cols), F32), pltpu.VMEM((1, cols), F32),
                        pltpu.VMEM((1, cols), F32), pltpu.VMEM((1, cols), F32),
                        pltpu.VMEM((HEAD_DIM + ONES_ROWS, cols), F32), pltpu.VMEM((1, cols), F32),
                        pltpu.VMEM((NSA_KV_GROUPS, LANES, Q_TILE), F32)],
        compiler_params=pltpu.CompilerParams(dimension_semantics=("parallel", "parallel"),
                                             vmem_limit_bytes=VMEM_LIMIT),
        name="nsa",
    )(q, kcc, vcc, ks, vst, kw, vwt, gn, jnp.asarray(overlap_t, dtype=BF16), block_onehot)


POOL_HALO = 16


def _merge_kernel(x_ref, xh_ref, yn_ref, kvm_ref, g_ref, wpool_ref, wqx_ref, wgbr_ref, poolw_ref, pscale_ref,
                  wbp_ref, wbn_ref, wbx_ref, wout_ref, gpost_ref, o_ref, *, S):
    i = pl.program_id(0)
    tm = x_ref.shape[0]
    t0 = (i * tm) % S
    x = x_ref[...]
    g = g_ref[...]
    h = _rms(x, g)
    hb = h.astype(BF16)

    hh = _rms(xh_ref[...], g).astype(BF16)
    u_halo = jnp.dot(hh, wpool_ref[...], preferred_element_type=F32)
    u_halo = jnp.where(t0 > 0, u_halo, 0.0)
    u = jnp.dot(hb, wpool_ref[...], preferred_element_type=F32)
    ue = jnp.concatenate([u_halo, u], axis=0)
    trow = t0 + lax.broadcasted_iota(jnp.int32, (tm, 1), 0) + 1
    ypool = []
    for gi, w in enumerate(POOL_WINDOWS):
        acc = ue[:, gi * POOL_GROUP:(gi + 1) * POOL_GROUP]
        step = 1
        while step < w:
            acc = acc + pltpu.roll(acc, step, 0)
            step *= 2
        cnt = jnp.minimum(trow, w).astype(F32)
        p = acc[POOL_HALO:, :] / cnt - u[:, gi * POOL_GROUP:(gi + 1) * POOL_GROUP]
        ypool.append(jnp.dot(p.astype(BF16), poolw_ref[gi], preferred_element_type=F32))
    ypool = jnp.concatenate(ypool, axis=1) * pscale_ref[...]

    qx = jnp.dot(hb, wqx_ref[...], preferred_element_type=F32)
    ymem = []
    for hd in range(XA_HEADS):
        km = kvm_ref[0, :, hd * XA_HEAD_DIM:(hd + 1) * XA_HEAD_DIM]
        vm = kvm_ref[0, :, XA_WIDTH + hd * XA_HEAD_DIM:XA_WIDTH + (hd + 1) * XA_HEAD_DIM]
        s = _dot_nt(qx[:, hd * XA_HEAD_DIM:(hd + 1) * XA_HEAD_DIM].astype(BF16), km) * (XA_HEAD_DIM ** -0.5)
        e = jnp.exp(s - jnp.max(s, axis=-1, keepdims=True))
        p = e / jnp.sum(e, axis=-1, keepdims=True)
        ymem.append(jnp.dot(p.astype(BF16), vm, preferred_element_type=F32))
    ymem = jnp.concatenate(ymem, axis=1)

    gbr = jax.nn.sigmoid(jnp.dot(hb, wgbr_ref[...], preferred_element_type=F32))
    y = (gbr[:, 0:D_MODEL] * jnp.dot(ypool.astype(BF16), wbp_ref[...], preferred_element_type=F32)
         + gbr[:, D_MODEL:2 * D_MODEL] * jnp.dot(yn_ref[...], wbn_ref[...], preferred_element_type=F32)
         + gbr[:, 2 * D_MODEL:3 * D_MODEL] * jnp.dot(ymem.astype(BF16), wbx_ref[...], preferred_element_type=F32))
    o = jnp.dot(y.astype(BF16), wout_ref[...], preferred_element_type=F32)
    o_ref[...] = x + _rms(o, gpost_ref[...])


def _merge(x2, ynsa, kvm, g, wpool, wqx, wgbr, poolw, pscale, wbp, wbn, wbx, wout, gpost, S, tm=512):
    T = x2.shape[0]
    M = kvm.shape[1]
    row = lambda w: pl.BlockSpec((tm, w), lambda i: (i, 0))
    halo = pl.BlockSpec((POOL_HALO, D_MODEL), lambda i: (jnp.maximum(i * (tm // POOL_HALO) - 1, 0), 0))
    return pl.pallas_call(
        functools.partial(_merge_kernel, S=S),
        grid=(T // tm,),
        in_specs=[row(D_MODEL), halo, row(NSA_WIDTH),
                  pl.BlockSpec((1, M, 2 * XA_WIDTH), lambda i: ((i * tm) // S, 0, 0)),
                  _resident((1, D_MODEL)), _resident(wpool.shape), _resident(wqx.shape), _resident(wgbr.shape),
                  _resident(poolw.shape), _resident(pscale.shape), _resident(wbp.shape), _resident(wbn.shape),
                  _resident(wbx.shape), _resident(wout.shape), _resident((1, D_MODEL))],
        out_specs=row(D_MODEL),
        out_shape=jax.ShapeDtypeStruct((T, D_MODEL), F32),
        compiler_params=pltpu.CompilerParams(dimension_semantics=("parallel",), vmem_limit_bytes=VMEM_LIMIT),
        name="merge",
    )(x2, x2, ynsa, kvm, g, wpool, wqx, wgbr, poolw, pscale, wbp, wbn, wbx, wout, gpost)


FFN_HALO = 8
FFN_CHUNK = 2816


def _ffn_kernel(x_ref, xh_ref, g_ref, wup_ref, cw_ref, cb_ref, wdn_ref, gpost_ref, o_ref, *, S):
    i = pl.program_id(0)
    tm = x_ref.shape[0]
    t0 = (i * tm) % S
    x = x_ref[...]
    g = g_ref[...]
    hh = jnp.where(t0 > 0, _rms(xh_ref[...], g), 0.0)
    he = jnp.concatenate([hh, _rms(x, g)], axis=0).astype(BF16)

    def conv(col0):
        u = jnp.dot(he, wup_ref[:, col0:col0 + FFN_CHUNK], preferred_element_type=F32)
        c = cb_ref[:, col0:col0 + FFN_CHUNK] + cw_ref[CONV_WIDTH - 1:CONV_WIDTH, col0:col0 + FFN_CHUNK] * u
        for k in range(1, CONV_WIDTH):
            tap = cw_ref[CONV_WIDTH - 1 - k:CONV_WIDTH - k, col0:col0 + FFN_CHUNK]
            c = c + tap * pltpu.roll(u, k, 0)
        return c[FFN_HALO:, :]

    f = jnp.zeros((tm, D_MODEL), F32)
    for j in range(D_FF // FFN_CHUNK):
        act = _gelu_tanh(conv(j * FFN_CHUNK)) * conv(D_FF + j * FFN_CHUNK)
        f = f + jnp.dot(act.astype(BF16), wdn_ref[j * FFN_CHUNK:(j + 1) * FFN_CHUNK, :],
                        preferred_element_type=F32)
    o_ref[...] = x + _rms(f, gpost_ref[...])


def _ffn(x2, g, wup, cw, cb, wdn, gpost, S, tm=512):
    T = x2.shape[0]
    row = pl.BlockSpec((tm, D_MODEL), lambda i: (i, 0))
    halo = pl.BlockSpec((FFN_HALO, D_MODEL), lambda i: (jnp.maximum(i * (tm // FFN_HALO) - 1, 0), 0))
    return pl.pallas_call(
        functools.partial(_ffn_kernel, S=S),
        grid=(T // tm,),
        in_specs=[row, halo, _resident((1, D_MODEL)), _resident(wup.shape), _resident(cw.shape),
                  _resident(cb.shape), _resident(wdn.shape), _resident((1, D_MODEL))],
        out_specs=row,
        out_shape=jax.ShapeDtypeStruct((T, D_MODEL), F32),
        compiler_params=pltpu.CompilerParams(dimension_semantics=("parallel",), vmem_limit_bytes=VMEM_LIMIT),
        name="ffn",
    )(x2, x2, g, wup, cw, cb, wdn, gpost)


def _layer(x2, mem, pos2, B, S, pre_mix_g, w_in, pool_w, pool_scale, cmp_pe, cmp_w1, cmp_w2, mem_norm_g,
           w_mem_kv, w_br_pool, w_br_nsa, w_br_xa, w_out, post_mix_g, pre_ffn_g, w_up, conv_w, conv_b,
           w_down, post_ffn_g):
    o = np.cumsum((0,) + IN_SIZES)
    w_pool, w_q, w_kv, w_gn, w_qx, w_gbr = (w_in[:, o[k]:o[k + 1]] for k in range(6))
    w_gn = jnp.pad(w_gn, ((0, 0), (0, LANES - w_gn.shape[1])))
    vec = lambda a: a.reshape(1, -1)
    q, kc, vc, ks, vst, kw, vwt, gn = _in_proj(x2, pos2, vec(pre_mix_g), w_q.astype(BF16), w_kv[:, :128].astype(BF16),
                                               w_kv[:, 128:].astype(BF16), w_gn.astype(BF16))
    b3 = lambda a: a.reshape(B, S, a.shape[-1])
    kcc = _compress(b3(kc), cmp_pe[0], cmp_w1[0], cmp_w2[0])
    vcc = _compress(b3(vc), cmp_pe[1], cmp_w1[1], cmp_w2[1])
    ynsa = _nsa(b3(q), kcc, vcc, b3(ks), vst, b3(kw), vwt, b3(gn))
    kvm = _mem_kv(mem, vec(mem_norm_g), w_mem_kv.astype(BF16))
    x2 = _merge(x2, ynsa.reshape(B * S, NSA_WIDTH), kvm, vec(pre_mix_g), w_pool.astype(BF16), w_qx.astype(BF16),
                w_gbr.astype(BF16), pool_w.astype(BF16), vec(pool_scale), w_br_pool.astype(BF16),
                w_br_nsa.astype(BF16), w_br_xa.astype(BF16), w_out.astype(BF16), vec(post_mix_g), S)
    return _ffn(x2, vec(pre_ffn_g), w_up.astype(BF16), conv_w, vec(conv_b), w_down.astype(BF16),
                vec(post_ffn_g), S)


def kernel(x, mem, positions, pre_mix_g, w_in, pool_w, pool_scale, cmp_pe, cmp_w1, cmp_w2, mem_norm_g, w_mem_kv,
           w_br_pool, w_br_nsa, w_br_xa, w_out, post_mix_g, pre_ffn_g, w_up, conv_w, conv_b, w_down, post_ffn_g):
    B, S, D = x.shape
    x2 = x.reshape(B * S, D)
    pos2 = positions.reshape(B * S)
    for l in range(pre_mix_g.shape[0]):
        x2 = _layer(x2, mem, pos2, B, S, pre_mix_g[l], w_in[l], pool_w[l], pool_scale[l], cmp_pe[l], cmp_w1[l],
                    cmp_w2[l], mem_norm_g[l], w_mem_kv[l], w_br_pool[l], w_br_nsa[l], w_br_xa[l], w_out[l],
                    post_mix_g[l], pre_ffn_g[l], w_up[l], conv_w[l], conv_b[l], w_down[l], post_ffn_g[l])
    return x2.reshape(B, S, D)
```

```python
import functools

import numpy as np
import jax
import jax.numpy as jnp
from jax import lax
from jax.experimental import pallas as pl
from jax.experimental.pallas import tpu as pltpu

F32 = jnp.float32
BF16 = jnp.bfloat16

D_MODEL = 1024
EPS = 1e-6
POOL_WINDOWS = (2, 4, 8, 16)
POOL_GROUP = 128
POOL_WIDTH = POOL_GROUP * len(POOL_WINDOWS)
NSA_HEADS = 16
NSA_KV_GROUPS = 2
HEADS_PER_GROUP = NSA_HEADS // NSA_KV_GROUPS
HEAD_DIM = 64
NSA_WIDTH = NSA_HEADS * HEAD_DIM
NSA_KV_WIDTH = NSA_KV_GROUPS * HEAD_DIM
CMP_BLOCK = 32
CMP_STRIDE = 16
CMP_HIDDEN = 256
SEL_BLOCK = 64
SEL_TOPK = 16
WINDOW = 512
ROPE_THETA = 500000.0
ROT_DIM = HEAD_DIM // 4
XA_HEADS = 4
XA_HEAD_DIM = 128
XA_WIDTH = XA_HEADS * XA_HEAD_DIM
N_BRANCHES = 3
D_FF = 2816
CONV_WIDTH = 3
IN_SIZES = (POOL_WIDTH, NSA_WIDTH, 6 * NSA_KV_WIDTH, 3 * NSA_HEADS, XA_WIDTH, N_BRANCHES * D_MODEL)

SEL_FORCED = 3
LANES = 128
ONES_ROWS = 16
Q_TILE = 128
SEL_KEY_TILE = 512
MASK_NEG = -1e9
SCORE_FLOOR = -1e30
LAZY_MAX_HEADROOM = 64.0
VMEM_LIMIT = 56 * 1024 * 1024
Q_SCALE = HEAD_DIM ** -0.5 * 1.4426950408889634


def _rms(x, g):
    return x * lax.rsqrt(jnp.mean(x * x, axis=-1, keepdims=True) + EPS) * g


def _gelu_tanh(x):
    c = 0.7978845608028654
    return x * (0.5 + 0.5 * jnp.tanh(x * (c + (0.044715 * c) * (x * x))))


def _dot_bf16(x, w):
    return jnp.dot(x.astype(BF16), w, preferred_element_type=F32)


def _dot_nt(a, b):
    return lax.dot_general(a, b, (((1,), (1,)), ((), ())), preferred_element_type=F32)


def _resident(shape):
    return pl.BlockSpec(shape, lambda *_: (0,) * len(shape), pipeline_mode=pl.Buffered(1))


def _inproj_kernel(x_ref, pos_ref, g_ref, wq_ref, wkc_ref, wkv_ref, wgn_ref, frq_ref, sg0_ref, sg1_ref, sg2_ref, one_ref,
                   q_ref, kc_ref, vc_ref, ks_ref, vs_ref, kw_ref, vw_ref, gn_ref):
    h = _rms(x_ref[...], g_ref[...])
    hb = h.astype(BF16)
    ang_t = frq_ref[...] * pos_ref[0].astype(F32)

    def spread(table_t, e_ref):
        hi = table_t.astype(BF16)
        rest = table_t - hi.astype(F32)
        mid = rest.astype(BF16)
        lo = (rest - mid.astype(F32)).astype(BF16)
        return lax.dot_general(jnp.concatenate([hi, mid, lo], axis=0), e_ref[...], (((0,), (0,)), ((), ())),
                               preferred_element_type=F32)

    cos_t = jnp.cos(ang_t)
    sin_t = jnp.sin(ang_t)
    c = spread(cos_t, sg0_ref) + one_ref[...]
    s1 = spread(sin_t, sg1_ref)
    s2 = spread(sin_t, sg2_ref)

    def rope(v):
        return v * c + pltpu.roll(v, LANES - ROT_DIM // 2, 1) * s1 + pltpu.roll(v, ROT_DIM // 2, 1) * s2

    q = jnp.dot(hb, wq_ref[...], preferred_element_type=F32)
    for j in range(NSA_WIDTH // LANES):
        q_ref[:, j * LANES:(j + 1) * LANES] = rope(q[:, j * LANES:(j + 1) * LANES]) * Q_SCALE
    kc_ref[...] = rope(jnp.dot(hb, wkc_ref[...], preferred_element_type=F32)).astype(BF16)
    kv = jnp.dot(hb, wkv_ref[...], preferred_element_type=F32)
    vc_ref[...] = kv[:, 0:128].astype(BF16)
    ks_ref[...] = rope(kv[:, 128:256]).astype(BF16)
    vs_ref[...] = kv[:, 256:384].T.astype(BF16)
    kw_ref[...] = rope(kv[:, 384:512]).astype(BF16)
    vw_ref[...] = kv[:, 512:640].T.astype(BF16)
    gn_ref[...] = jax.nn.sigmoid(jnp.dot(hb, wgn_ref[...], preferred_element_type=F32))


def _in_proj(x2, pos, g, wq, wkc, wkv, wgn, tm=1024):
    T = x2.shape[0]
    half = ROT_DIM // 2
    d = np.arange(LANES) % HEAD_DIM
    inv_freq = ROPE_THETA ** (-jnp.arange(half, dtype=F32) * (2.0 / ROT_DIM))
    frq = inv_freq[:, None]
    j = np.arange(half)[:, None]
    stack3 = lambda m: jnp.asarray(np.tile(m.astype(np.float32), (3, 1)), dtype=BF16)
    sg0 = stack3((d[None, :] < ROT_DIM) & (d[None, :] % half == j))
    sg1 = stack3(-((d[None, :] < half) & (d[None, :] == j)).astype(np.float32))
    sg2 = stack3((d[None, :] >= half) & (d[None, :] < ROT_DIM) & (d[None, :] - half == j))
    one = np.where(d >= ROT_DIM, 1.0, 0.0).astype(np.float32)[None, :]
    row = lambda w: pl.BlockSpec((tm, w), lambda i: (i, 0))
    col = pl.BlockSpec((128, tm), lambda i: (0, i))
    out_shape = [jax.ShapeDtypeStruct((T, NSA_WIDTH), F32),
                 jax.ShapeDtypeStruct((T, 128), BF16), jax.ShapeDtypeStruct((T, 128), BF16),
                 jax.ShapeDtypeStruct((T, 128), BF16), jax.ShapeDtypeStruct((128, T), BF16),
                 jax.ShapeDtypeStruct((T, 128), BF16), jax.ShapeDtypeStruct((128, T), BF16),
                 jax.ShapeDtypeStruct((T, 128), F32)]
    return pl.pallas_call(
        _inproj_kernel,
        grid=(T // tm,),
        in_specs=[row(D_MODEL), pl.BlockSpec((1, 1, tm), lambda i: (i, 0, 0)), _resident((1, D_MODEL)),
                  _resident(wq.shape), _resident(wkc.shape), _resident(wkv.shape), _resident(wgn.shape),
                  _resident((half, 1)), _resident((3 * half, LANES)), _resident((3 * half, LANES)),
                  _resident((3 * half, LANES)), _resident((1, LANES))],
        out_specs=[row(NSA_WIDTH), row(128), row(128), row(128), col, row(128), col, row(128)],
        out_shape=out_shape,
        compiler_params=pltpu.CompilerParams(dimension_semantics=("parallel",), vmem_limit_bytes=VMEM_LIMIT),
        name="in_proj",
    )(x2, pos.reshape(T // tm, 1, tm), g, wq, wkc, wkv, wgn, frq, sg0, sg1, sg2, jnp.asarray(one))


def _compress_kernel(c_ref, pet_ref, peb_ref, wt_ref, wb_ref, w2_ref, o_ref):
    c = c_ref[0].astype(F32)
    a = _dot_bf16(c + pet_ref[...], wt_ref[...])
    b = _dot_bf16(c + peb_ref[...], wb_ref[...])
    n = c.shape[0]
    hid = a + pltpu.roll(b, n - 1, 0)
    o_ref[0] = _dot_bf16(_gelu_tanh(hid), w2_ref[...])


def _compress(kv, pe, w1, w2):
    B, S, _ = kv.shape
    n = S // CMP_STRIDE
    G = NSA_KV_GROUPS
    half = CMP_BLOCK // 2
    eye = jnp.eye(G, dtype=F32)

    def blockdiag_w1(w):
        w = w.reshape(half, 1, HEAD_DIM, 1, CMP_HIDDEN) * eye.reshape(1, G, 1, G, 1)
        return w.reshape(half * G * HEAD_DIM, G * CMP_HIDDEN)

    wt = blockdiag_w1(w1[:half * HEAD_DIM]).astype(BF16)
    wb = blockdiag_w1(w1[half * HEAD_DIM:]).astype(BF16)
    w2b = ((w2.reshape(1, CMP_HIDDEN, 1, HEAD_DIM) * eye.reshape(G, 1, G, 1))
           .reshape(G * CMP_HIDDEN, G * HEAD_DIM).astype(BF16))
    pet = jnp.broadcast_to(pe[:half, None, :], (half, G, HEAD_DIM)).reshape(1, half * G * HEAD_DIM)
    peb = jnp.broadcast_to(pe[half:, None, :], (half, G, HEAD_DIM)).reshape(1, half * G * HEAD_DIM)
    c = kv.reshape(B, n, CMP_STRIDE * 128)
    return pl.pallas_call(
        _compress_kernel,
        grid=(B,),
        in_specs=[pl.BlockSpec((1, n, CMP_STRIDE * 128), lambda b: (b, 0, 0)),
                  _resident(pet.shape), _resident(peb.shape), _resident(wt.shape), _resident(wb.shape),
                  _resident(w2b.shape)],
        out_specs=pl.BlockSpec((1, n, 128), lambda b: (b, 0, 0)),
        out_shape=jax.ShapeDtypeStruct((B, n, 128), F32),
        compiler_params=pltpu.CompilerParams(dimension_semantics=("parallel",), vmem_limit_bytes=VMEM_LIMIT),
        name="compress",
    )(c, pet, peb, wt, wb, w2b)


def _memkv_kernel(m_ref, g_ref, w_ref, o_ref):
    mn = _rms(m_ref[0], g_ref[...])
    o_ref[0] = jnp.dot(mn.astype(BF16), w_ref[...], preferred_element_type=F32).astype(BF16)


def _mem_kv(mem, g, w):
    B, M, _ = mem.shape
    return pl.pallas_call(
        _memkv_kernel,
        grid=(B,),
        in_specs=[pl.BlockSpec((1, M, D_MODEL), lambda b: (b, 0, 0)), _resident((1, D_MODEL)),
                  _resident(w.shape)],
        out_specs=pl.BlockSpec((1, M, 2 * XA_WIDTH), lambda b: (b, 0, 0)),
        out_shape=jax.ShapeDtypeStruct((B, M, 2 * XA_WIDTH), BF16),
        compiler_params=pltpu.CompilerParams(dimension_semantics=("parallel",), vmem_limit_bytes=VMEM_LIMIT),
        name="mem_kv",
    )(mem, g, w)


def _nsa_kernel(q_ref, kcc_ref, vcc_ref, ks_ref, vst_ref, kw_ref, vwt_ref, gn_ref, ovt_ref, blk_ref, o_ref,
                qat_ref, oct_ref, ost_ref, owt_ref, m_ref, l_ref, ap_ref, pvp_ref, over_ref, imp_ref):
    qt = pl.program_id(1)
    q0 = qt * Q_TILE
    n_cmp = kcc_ref.shape[1]
    head_cols = [slice(h * Q_TILE, (h + 1) * Q_TILE) for h in range(NSA_HEADS)]
    group_cols = [slice(g * HEADS_PER_GROUP * Q_TILE, (g + 1) * HEADS_PER_GROUP * Q_TILE)
                  for g in range(NSA_KV_GROUPS)]
    per_head = lambda a, n: jnp.concatenate([a] * n, axis=1)

    def masked(s, bias):
        return jnp.concatenate([s[:, c:c + Q_TILE] + bias for c in range(0, s.shape[1], Q_TILE)], axis=1)

    def value_product(vt_ref, start, n, p):
        out = []
        for g in range(NSA_KV_GROUPS):
            lhs = jnp.concatenate([vt_ref[g * HEAD_DIM:(g + 1) * HEAD_DIM, pl.ds(start, n)],
                                   jnp.ones((ONES_ROWS, n), BF16)], axis=0)
            out.append(jnp.dot(lhs, p[:, group_cols[g]], preferred_element_type=F32))
        return jnp.concatenate(out, axis=1)

    zeros = jnp.zeros((HEAD_DIM, Q_TILE), F32)
    for j in range(NSA_HEADS // 2):
        t = q_ref[0, :, j * LANES:(j + 1) * LANES].T
        for k in range(2):
            h = 2 * j + k
            part = t[k * HEAD_DIM:(k + 1) * HEAD_DIM, :]
            pair = [part, zeros] if h // HEADS_PER_GROUP == 0 else [zeros, part]
            qat_ref[0:LANES, head_cols[h]] = jnp.concatenate(pair, axis=0).astype(BF16)

    def compressed_branch(n_rows):
        crow = lax.broadcasted_iota(jnp.int32, (n_rows, Q_TILE), 0)
        cq = lax.broadcasted_iota(jnp.int32, (n_rows, Q_TILE), 1) + q0
        valid_c = crow * CMP_STRIDE + (CMP_BLOCK - 1) <= cq
        bias_c = jnp.where(valid_c, 0.0, SCORE_FLOOR)
        seen_c = per_head(jnp.where(cq[0:1, :] >= CMP_BLOCK - 1, 1.0, 0.0), HEADS_PER_GROUP)
        kcc = kcc_ref[0, 0:n_rows, :].astype(BF16)
        vcct = vcc_ref[0, 0:n_rows, :].T.astype(BF16)
        ones_ov = jnp.concatenate([jnp.ones((ONES_ROWS, n_rows), BF16), ovt_ref[:, 0:n_rows]], axis=0)
        for g in range(NSA_KV_GROUPS):
            cols = group_cols[g]
            s = masked(jnp.dot(kcc, qat_ref[0:LANES, cols], preferred_element_type=F32), bias_c)
            p = jnp.exp2(s - jnp.max(s, axis=0, keepdims=True)).astype(BF16)
            lhs = jnp.concatenate([vcct[g * HEAD_DIM:(g + 1) * HEAD_DIM], ones_ov], axis=0)
            prod = jnp.dot(lhs, p, preferred_element_type=F32)
            norm = seen_c / jnp.maximum(prod[HEAD_DIM:HEAD_DIM + 1], jnp.finfo(F32).tiny)
            oct_ref[:, cols] = prod[0:HEAD_DIM] * norm
            imp_heads = prod[HEAD_DIM + ONES_ROWS:HEAD_DIM + ONES_ROWS + LANES] * norm
            imp = imp_heads[:, 0:Q_TILE]
            for k in range(1, HEADS_PER_GROUP):
                imp = imp + imp_heads[:, k * Q_TILE:(k + 1) * Q_TILE]
            imp_ref[g] = imp

    row_steps = list(range(LANES, n_cmp + 1, LANES))
    for k, n_rows in enumerate(row_steps):
        in_step = (qt * (Q_TILE // CMP_STRIDE) + Q_TILE // CMP_STRIDE - 1) // LANES == k
        pl.when(in_step)(functools.partial(compressed_branch, n_rows))

    brow = lax.broadcasted_iota(jnp.int32, (LANES, Q_TILE), 0)
    cur = (lax.broadcasted_iota(jnp.int32, (LANES, Q_TILE), 1) + q0) // SEL_BLOCK
    forced = (brow == 0) | (brow == cur) | (brow == cur - 1)
    brow_f = brow.astype(F32)
    for g in range(NSA_KV_GROUPS):
        v = jnp.where(brow > cur, -1.0, jnp.where(forced, -2.0, imp_ref[g]))
        for _ in range(SEL_TOPK - SEL_FORCED):
            mx = jnp.max(v, axis=0, keepdims=True)
            first = jnp.min(jnp.where(v == mx, brow_f, float(LANES)), axis=0, keepdims=True)
            v = jnp.where(brow_f == first, -2.0, v)
        neg = jnp.where(v == -2.0, 0.0, MASK_NEG).astype(BF16)
        for h in range(HEADS_PER_GROUP):
            qat_ref[LANES:2 * LANES, head_cols[g * HEADS_PER_GROUP + h]] = neg

    def scores(start, n):
        ka = jnp.concatenate([ks_ref[0, pl.ds(start, n), :], blk_ref[pl.ds(start, n), :]], axis=1)
        return jnp.dot(ka, qat_ref[...], preferred_element_type=F32)

    def online_update(s, start, n):
        m_old = m_ref[...]
        m_new = jnp.maximum(m_old, jnp.max(s, axis=0, keepdims=True))
        alpha = jnp.exp2(m_old - m_new)
        pv = value_product(vst_ref, start, n, jnp.exp2(s - m_new).astype(BF16))
        ost_ref[...] = alpha * ost_ref[...] + pv[0:HEAD_DIM]
        l_ref[...] = alpha * l_ref[...] + pv[HEAD_DIM:HEAD_DIM + 1]
        m_ref[...] = m_new

    qd = pl.multiple_of(q0, Q_TILE)
    drow = lax.broadcasted_iota(jnp.int32, (Q_TILE, Q_TILE), 0)
    dcol = lax.broadcasted_iota(jnp.int32, (Q_TILE, Q_TILE), 1)
    bias_d = jnp.where(drow <= dcol, 0.0, SCORE_FLOOR)
    head = q0 % SEL_KEY_TILE
    n_full = q0 // SEL_KEY_TILE
    head_sizes = range(Q_TILE, SEL_KEY_TILE, Q_TILE)
    tile_start = lambda j: pl.multiple_of(head + j * SEL_KEY_TILE, Q_TILE)

    def start_with_diagonal():
        l_ref[...] = jnp.zeros(l_ref.shape, F32)
        ost_ref[...] = jnp.zeros(ost_ref.shape, F32)
        m_ref[...] = jnp.full(m_ref.shape, SCORE_FLOOR, F32)
        online_update(masked(jnp.dot(ks_ref[0, pl.ds(qd, Q_TILE), :], qat_ref[0:LANES, :],
                                     preferred_element_type=F32), bias_d), qd, Q_TILE)

    start_with_diagonal()
    pvp_ref[...] = jnp.zeros(pvp_ref.shape, F32)
    ap_ref[...] = jnp.ones(ap_ref.shape, F32)
    over_ref[...] = jnp.full(over_ref.shape, SCORE_FLOOR, F32)

    def fold_parked():
        alpha = ap_ref[...]
        ost_ref[...] = alpha * (ost_ref[...] + pvp_ref[0:HEAD_DIM, :])
        l_ref[...] = alpha * (l_ref[...] + pvp_ref[HEAD_DIM:HEAD_DIM + 1, :])

    def lazy_step(start, n):
        fold_parked()
        s = scores(start, n)
        m_old = m_ref[...]
        pvp_ref[...] = value_product(vst_ref, start, n, jnp.exp2(s - m_old).astype(BF16))
        t_max = jnp.max(s, axis=0, keepdims=True)
        m_new = jnp.maximum(m_old, t_max)
        over_ref[...] = jnp.maximum(over_ref[...], t_max - m_old)
        ap_ref[...] = jnp.exp2(m_old - m_new)
        m_ref[...] = m_new

    def lazy_pair(j, carry):
        lazy_step(tile_start(2 * j), SEL_KEY_TILE)
        lazy_step(tile_start(2 * j + 1), SEL_KEY_TILE)
        return carry

    lax.fori_loop(0, n_full // 2, lazy_pair, 0)
    pl.when(n_full % 2 == 1)(lambda: lazy_step(tile_start(n_full - 1), SEL_KEY_TILE))
    for n in head_sizes:
        pl.when(head == n)(functools.partial(lazy_step, 0, n))
    fold_parked()

    @pl.when(jnp.max(over_ref[...]) > LAZY_MAX_HEADROOM)
    def _():
        start_with_diagonal()

        def textbook_tile(j, carry):
            online_update(scores(tile_start(j), SEL_KEY_TILE), tile_start(j), SEL_KEY_TILE)
            return carry

        lax.fori_loop(0, n_full, textbook_tile, 0)
        for n in head_sizes:
            pl.when(head == n)(lambda n=n: online_update(scores(0, n), 0, n))

    wlen = WINDOW + Q_TILE
    w0 = pl.multiple_of(jnp.maximum(q0 - WINDOW, 0), Q_TILE)
    kidx = w0 + lax.broadcasted_iota(jnp.int32, (wlen, Q_TILE), 0)
    tqw = q0 + lax.broadcasted_iota(jnp.int32, (wlen, Q_TILE), 1)
    bias_w = jnp.where((kidx <= tqw) & (tqw - kidx < WINDOW), 0.0, SCORE_FLOOR)
    s = masked(jnp.dot(kw_ref[0, pl.ds(w0, wlen), :], qat_ref[0:LANES, :], preferred_element_type=F32), bias_w)
    o = value_product(vwt_ref, w0, wlen, jnp.exp2(s - jnp.max(s, axis=0, keepdims=True)).astype(BF16))
    owt_ref[...] = o[0:HEAD_DIM] / o[HEAD_DIM:HEAD_DIM + 1]

    gt = gn_ref[0].T
    for j in range(NSA_HEADS // 2):
        parts = []
        for h in (2 * j, 2 * j + 1):
            cols = head_cols[h]
            o_sel = ost_ref[:, cols] / l_ref[:, cols]
            parts.append(gt[3 * h:3 * h + 1, :] * oct_ref[:, cols] + gt[3 * h + 1:3 * h + 2, :] * o_sel
                         + gt[3 * h + 2:3 * h + 3, :] * owt_ref[:, cols])
        o_ref[0, :, j * LANES:(j + 1) * LANES] = jnp.concatenate(parts, axis=0).T.astype(o_ref.dtype)


def _nsa(q, kcc, vcc, ks, vst, kw, vwt, gn):
    B, S, _ = q.shape
    n_cmp = kcc.shape[1]
    n_sel = S // SEL_BLOCK
    assert n_sel <= LANES and n_cmp % LANES == 0 and S % SEL_KEY_TILE == 0 and S >= WINDOW + Q_TILE
    ci = np.arange(n_cmp)[None, :] * CMP_STRIDE
    sj = np.arange(LANES)[:, None] * SEL_BLOCK
    overlap_t = ((ci < sj + SEL_BLOCK) & (ci + CMP_BLOCK > sj)).astype(np.float32)
    block_onehot = (np.arange(S)[:, None] // SEL_BLOCK == np.arange(LANES)[None, :])
    block_onehot = jnp.asarray(block_onehot.astype(np.float32), dtype=BF16)
    cols = NSA_HEADS * Q_TILE
    full = lambda n, w: pl.BlockSpec((1, n, w), lambda b, i: (b, 0, 0))
    full_t = pl.BlockSpec((128, S), lambda b, i: (0, b))
    tile = lambda w: pl.BlockSpec((1, Q_TILE, w), lambda b, i: (b, i, 0))
    return pl.pallas_call(
        _nsa_kernel,
        grid=(B, S // Q_TILE),
        in_specs=[tile(NSA_WIDTH), full(n_cmp, 128), full(n_cmp, 128), full(S, 128), full_t,
                  full(S, 128), full_t, tile(128), _resident(overlap_t.shape), _resident((S, LANES))],
        out_specs=tile(NSA_WIDTH),
        out_shape=jax.ShapeDtypeStruct((B, S, NSA_WIDTH), BF16),
        scratch_shapes=[pltpu.VMEM((2 * LANES, cols), BF16),
                        pltpu.VMEM((HEAD_DIM, cols), F32), pltpu.VMEM((HEAD_DIM, cols), F32),
                        pltpu.VMEM((HEAD_DIM, cols), F32), pltpu.VMEM((1, cols), F32),
                        pltpu.VMEM((1, cols), F32), pltpu.VMEM((1, cols), F32),
                        pltpu.VMEM((HEAD_DIM + ONES_ROWS, cols), F32), pltpu.VMEM((1, cols), F32),
                        pltpu.VMEM((NSA_KV_GROUPS, LANES, Q_TILE), F32)],
        compiler_params=pltpu.CompilerParams(dimension_semantics=("parallel", "parallel"),
                                             vmem_limit_bytes=VMEM_LIMIT),
        name="nsa",
    )(q, kcc, vcc, ks, vst, kw, vwt, gn, jnp.asarray(overlap_t, dtype=BF16), block_onehot)


POOL_HALO = 16


def _merge_kernel(x_ref, xh_ref, yn_ref, kvm_ref, g_ref, wpool_ref, wqx_ref, wgbr_ref, poolw_ref, pscale_ref,
                  wbp_ref, wbn_ref, wbx_ref, wout_ref, gpost_ref, o_ref, *, S):
    i = pl.program_id(0)
    tm = x_ref.shape[0]
    t0 = (i * tm) % S
    x = x_ref[...]
    g = g_ref[...]
    h = _rms(x, g)
    hb = h.astype(BF16)

    hh = _rms(xh_ref[...], g).astype(BF16)
    u_halo = jnp.dot(hh, wpool_ref[...], preferred_element_type=F32)
    u_halo = jnp.where(t0 > 0, u_halo, 0.0)
    u = jnp.dot(hb, wpool_ref[...], preferred_element_type=F32)
    ue = jnp.concatenate([u_halo, u], axis=0)
    trow = t0 + lax.broadcasted_iota(jnp.int32, (tm, 1), 0) + 1
    ypool = []
    for gi, w in enumerate(POOL_WINDOWS):
        acc = ue[:, gi * POOL_GROUP:(gi + 1) * POOL_GROUP]
        step = 1
        while step < w:
            acc = acc + pltpu.roll(acc, step, 0)
            step *= 2
        cnt = jnp.minimum(trow, w).astype(F32)
        p = acc[POOL_HALO:, :] / cnt - u[:, gi * POOL_GROUP:(gi + 1) * POOL_GROUP]
        ypool.append(jnp.dot(p.astype(BF16), poolw_ref[gi], preferred_element_type=F32))
    ypool = jnp.concatenate(ypool, axis=1) * pscale_ref[...]

    qx = jnp.dot(hb, wqx_ref[...], preferred_element_type=F32)
    ymem = []
    for hd in range(XA_HEADS):
        km = kvm_ref[0, :, hd * XA_HEAD_DIM:(hd + 1) * XA_HEAD_DIM]
        vm = kvm_ref[0, :, XA_WIDTH + hd * XA_HEAD_DIM:XA_WIDTH + (hd + 1) * XA_HEAD_DIM]
        s = _dot_nt(qx[:, hd * XA_HEAD_DIM:(hd + 1) * XA_HEAD_DIM].astype(BF16), km) * (XA_HEAD_DIM ** -0.5)
        e = jnp.exp(s - jnp.max(s, axis=-1, keepdims=True))
        p = e / jnp.sum(e, axis=-1, keepdims=True)
        ymem.append(jnp.dot(p.astype(BF16), vm, preferred_element_type=F32))
    ymem = jnp.concatenate(ymem, axis=1)

    gbr = jax.nn.sigmoid(jnp.dot(hb, wgbr_ref[...], preferred_element_type=F32))
    y = (gbr[:, 0:D_MODEL] * jnp.dot(ypool.astype(BF16), wbp_ref[...], preferred_element_type=F32)
         + gbr[:, D_MODEL:2 * D_MODEL] * jnp.dot(yn_ref[...], wbn_ref[...], preferred_element_type=F32)
         + gbr[:, 2 * D_MODEL:3 * D_MODEL] * jnp.dot(ymem.astype(BF16), wbx_ref[...], preferred_element_type=F32))
    o = jnp.dot(y.astype(BF16), wout_ref[...], preferred_element_type=F32)
    o_ref[...] = x + _rms(o, gpost_ref[...])


def _merge(x2, ynsa, kvm, g, wpool, wqx, wgbr, poolw, pscale, wbp, wbn, wbx, wout, gpost, S, tm=1024):
    T = x2.shape[0]
    M = kvm.shape[1]
    row = lambda w: pl.BlockSpec((tm, w), lambda i: (i, 0))
    halo = pl.BlockSpec((POOL_HALO, D_MODEL), lambda i: (jnp.maximum(i * (tm // POOL_HALO) - 1, 0), 0))
    return pl.pallas_call(
        functools.partial(_merge_kernel, S=S),
        grid=(T // tm,),
        in_specs=[row(D_MODEL), halo, row(NSA_WIDTH),
                  pl.BlockSpec((1, M, 2 * XA_WIDTH), lambda i: ((i * tm) // S, 0, 0)),
                  _resident((1, D_MODEL)), _resident(wpool.shape), _resident(wqx.shape), _resident(wgbr.shape),
                  _resident(poolw.shape), _resident(pscale.shape), _resident(wbp.shape), _resident(wbn.shape),
                  _resident(wbx.shape), _resident(wout.shape), _resident((1, D_MODEL))],
        out_specs=row(D_MODEL),
        out_shape=jax.ShapeDtypeStruct((T, D_MODEL), F32),
        compiler_params=pltpu.CompilerParams(dimension_semantics=("parallel",), vmem_limit_bytes=VMEM_LIMIT),
        name="merge",
    )(x2, x2, ynsa, kvm, g, wpool, wqx, wgbr, poolw, pscale, wbp, wbn, wbx, wout, gpost)


FFN_HALO = 8
FFN_CHUNK = 2816


def _ffn_kernel(x_ref, xh_ref, g_ref, wup_ref, cw_ref, cb_ref, wdn_ref, gpost_ref, o_ref, *, S):
    i = pl.program_id(0)
    tm = x_ref.shape[0]
    t0 = (i * tm) % S
    x = x_ref[...]
    g = g_ref[...]
    hh = jnp.where(t0 > 0, _rms(xh_ref[...], g), 0.0)
    he = jnp.concatenate([hh, _rms(x, g)], axis=0).astype(BF16)

    def conv(col0):
        u = jnp.dot(he, wup_ref[:, col0:col0 + FFN_CHUNK], preferred_element_type=F32)
        c = cb_ref[:, col0:col0 + FFN_CHUNK] + cw_ref[CONV_WIDTH - 1:CONV_WIDTH, col0:col0 + FFN_CHUNK] * u
        for k in range(1, CONV_WIDTH):
            tap = cw_ref[CONV_WIDTH - 1 - k:CONV_WIDTH - k, col0:col0 + FFN_CHUNK]
            c = c + tap * pltpu.roll(u, k, 0)
        return c[FFN_HALO:, :]

    f = jnp.zeros((tm, D_MODEL), F32)
    for j in range(D_FF // FFN_CHUNK):
        act = _gelu_tanh(conv(j * FFN_CHUNK)) * conv(D_FF + j * FFN_CHUNK)
        f = f + jnp.dot(act.astype(BF16), wdn_ref[j * FFN_CHUNK:(j + 1) * FFN_CHUNK, :],
                        preferred_element_type=F32)
    o_ref[...] = x + _rms(f, gpost_ref[...])


def _ffn(x2, g, wup, cw, cb, wdn, gpost, S, tm=1024):
    T = x2.shape[0]
    row = pl.BlockSpec((tm, D_MODEL), lambda i: (i, 0))
    halo = pl.BlockSpec((FFN_HALO, D_MODEL), lambda i: (jnp.maximum(i * (tm // FFN_HALO) - 1, 0), 0))
    return pl.pallas_call(
        functools.partial(_ffn_kernel, S=S),
        grid=(T // tm,),
        in_specs=[row, halo, _resident((1, D_MODEL)), _resident(wup.shape), _resident(cw.shape),
                  _resident(cb.shape), _resident(wdn.shape), _resident((1, D_MODEL))],
        out_specs=row,
        out_shape=jax.ShapeDtypeStruct((T, D_MODEL), F32),
        compiler_params=pltpu.CompilerParams(dimension_semantics=("parallel",), vmem_limit_bytes=VMEM_LIMIT),
        name="ffn",
    )(x2, x2, g, wup, cw, cb, wdn, gpost)


def _layer(x2, mem, pos2, B, S, pre_mix_g, w_in, pool_w, pool_scale, cmp_pe, cmp_w1, cmp_w2, mem_norm_g,
           w_mem_kv, w_br_pool, w_br_nsa, w_br_xa, w_out, post_mix_g, pre_ffn_g, w_up, conv_w, conv_b,
           w_down, post_ffn_g):
    o = np.cumsum((0,) + IN_SIZES)
    w_pool, w_q, w_kv, w_gn, w_qx, w_gbr = (w_in[:, o[k]:o[k + 1]] for k in range(6))
    w_gn = jnp.pad(w_gn, ((0, 0), (0, LANES - w_gn.shape[1])))
    vec = lambda a: a.reshape(1, -1)
    q, kc, vc, ks, vst, kw, vwt, gn = _in_proj(x2, pos2, vec(pre_mix_g), w_q.astype(BF16), w_kv[:, :128].astype(BF16),
                                               w_kv[:, 128:].astype(BF16), w_gn.astype(BF16))
    b3 = lambda a: a.reshape(B, S, a.shape[-1])
    kcc = _compress(b3(kc), cmp_pe[0], cmp_w1[0], cmp_w2[0])
    vcc = _compress(b3(vc), cmp_pe[1], cmp_w1[1], cmp_w2[1])
    ynsa = _nsa(b3(q), kcc, vcc, b3(ks), vst, b3(kw), vwt, b3(gn))
    kvm = _mem_kv(mem, vec(mem_norm_g), w_mem_kv.astype(BF16))
    x2 = _merge(x2, ynsa.reshape(B * S, NSA_WIDTH), kvm, vec(pre_mix_g), w_pool.astype(BF16), w_qx.astype(BF16),
                w_gbr.astype(BF16), pool_w.astype(BF16), vec(pool_scale), w_br_pool.astype(BF16),
                w_br_nsa.astype(BF16), w_br_xa.astype(BF16), w_out.astype(BF16), vec(post_mix_g), S)
    return _ffn(x2, vec(pre_ffn_g), w_up.astype(BF16), conv_w, vec(conv_b), w_down.astype(BF16),
                vec(post_ffn_g), S)


def kernel(x, mem, positions, pre_mix_g, w_in, pool_w, pool_scale, cmp_pe, cmp_w1, cmp_w2, mem_norm_g, w_mem_kv,
           w_br_pool, w_br_nsa, w_br_xa, w_out, post_mix_g, pre_ffn_g, w_up, conv_w, conv_b, w_down, post_ffn_g):
    B, S, D = x.shape
    x2 = x.reshape(B * S, D)
    pos2 = positions.reshape(B * S)
    for l in range(pre_mix_g.shape[0]):
        x2 = _layer(x2, mem, pos2, B, S, pre_mix_g[l], w_in[l], pool_w[l], pool_scale[l], cmp_pe[l], cmp_w1[l],
                    cmp_w2[l], mem_norm_g[l], w_mem_kv[l], w_br_pool[l], w_br_nsa[l], w_br_xa[l], w_out[l],
                    post_mix_g[l], pre_ffn_g[l], w_up[l], conv_w[l], conv_b[l], w_down[l], post_ffn_g[l])
    return x2.reshape(B, S, D)
```

```python
import functools

import numpy as np
import jax
import jax.numpy as jnp
from jax import lax
from jax.experimental import pallas as pl
from jax.experimental.pallas import tpu as pltpu

F32 = jnp.float32
BF16 = jnp.bfloat16

D_MODEL = 1024
EPS = 1e-6
POOL_WINDOWS = (2, 4, 8, 16)
POOL_GROUP = 128
POOL_WIDTH = POOL_GROUP * len(POOL_WINDOWS)
NSA_HEADS = 16
NSA_KV_GROUPS = 2
HEADS_PER_GROUP = NSA_HEADS // NSA_KV_GROUPS
HEAD_DIM = 64
NSA_WIDTH = NSA_HEADS * HEAD_DIM
NSA_KV_WIDTH = NSA_KV_GROUPS * HEAD_DIM
CMP_BLOCK = 32
CMP_STRIDE = 16
CMP_HIDDEN = 256
SEL_BLOCK = 64
SEL_TOPK = 16
WINDOW = 512
ROPE_THETA = 500000.0
ROT_DIM = HEAD_DIM // 4
XA_HEADS = 4
XA_HEAD_DIM = 128
XA_WIDTH = XA_HEADS * XA_HEAD_DIM
N_BRANCHES = 3
D_FF = 2816
CONV_WIDTH = 3
IN_SIZES = (POOL_WIDTH, NSA_WIDTH, 6 * NSA_KV_WIDTH, 3 * NSA_HEADS, XA_WIDTH, N_BRANCHES * D_MODEL)

SEL_FORCED = 3
LANES = 128
ONES_ROWS = 16
Q_TILE = 128
SEL_KEY_TILE = 512
MASK_NEG = -1e9
SCORE_FLOOR = -1e30
LAZY_MAX_HEADROOM = 64.0
VMEM_LIMIT = 56 * 1024 * 1024
Q_SCALE = HEAD_DIM ** -0.5 * 1.4426950408889634


def _rms(x, g):
    return x * lax.rsqrt(jnp.mean(x * x, axis=-1, keepdims=True) + EPS) * g


def _gelu_tanh(x):
    c = 0.7978845608028654
    return x * (0.5 + 0.5 * jnp.tanh(x * (c + (0.044715 * c) * (x * x))))


def _dot_bf16(x, w):
    return jnp.dot(x.astype(BF16), w, preferred_element_type=F32)


def _dot_nt(a, b):
    return lax.dot_general(a, b, (((1,), (1,)), ((), ())), preferred_element_type=F32)


def _resident(shape):
    return pl.BlockSpec(shape, lambda *_: (0,) * len(shape), pipeline_mode=pl.Buffered(1))


def _inproj_kernel(x_ref, pos_ref, g_ref, wq_ref, wkc_ref, wkv_ref, wgn_ref, frq_ref, sg0_ref, sg1_ref, sg2_ref, one_ref,
                   q_ref, kc_ref, vc_ref, ks_ref, vs_ref, kw_ref, vw_ref, gn_ref):
    h = _rms(x_ref[...], g_ref[...])
    hb = h.astype(BF16)
    ang_t = frq_ref[...] * pos_ref[0].astype(F32)

    def spread(table_t, e_ref):
        hi = table_t.astype(BF16)
        rest = table_t - hi.astype(F32)
        mid = rest.astype(BF16)
        lo = (rest - mid.astype(F32)).astype(BF16)
        return lax.dot_general(jnp.concatenate([hi, mid, lo], axis=0), e_ref[...], (((0,), (0,)), ((), ())),
                               preferred_element_type=F32)

    cos_t = jnp.cos(ang_t)
    sin_t = jnp.sin(ang_t)
    c = spread(cos_t, sg0_ref) + one_ref[...]
    s1 = spread(sin_t, sg1_ref)
    s2 = spread(sin_t, sg2_ref)

    def rope(v):
        return v * c + pltpu.roll(v, LANES - ROT_DIM // 2, 1) * s1 + pltpu.roll(v, ROT_DIM // 2, 1) * s2

    q = jnp.dot(hb, wq_ref[...], preferred_element_type=F32)
    for j in range(NSA_WIDTH // LANES):
        q_ref[j * LANES:(j + 1) * LANES, :] = (rope(q[:, j * LANES:(j + 1) * LANES]) * Q_SCALE).T.astype(BF16)
    kc_ref[...] = rope(jnp.dot(hb, wkc_ref[...], preferred_element_type=F32)).astype(BF16)
    kv = jnp.dot(hb, wkv_ref[...], preferred_element_type=F32)
    vc_ref[...] = kv[:, 0:128].astype(BF16)
    ks_ref[...] = rope(kv[:, 128:256]).astype(BF16)
    vs_ref[...] = kv[:, 256:384].T.astype(BF16)
    kw_ref[...] = rope(kv[:, 384:512]).astype(BF16)
    vw_ref[...] = kv[:, 512:640].T.astype(BF16)
    gn_ref[...] = jax.nn.sigmoid(jnp.dot(hb, wgn_ref[...], preferred_element_type=F32))


def _in_proj(x2, pos, g, wq, wkc, wkv, wgn, tm=1024):
    T = x2.shape[0]
    half = ROT_DIM // 2
    d = np.arange(LANES) % HEAD_DIM
    inv_freq = ROPE_THETA ** (-jnp.arange(half, dtype=F32) * (2.0 / ROT_DIM))
    frq = inv_freq[:, None]
    j = np.arange(half)[:, None]
    stack3 = lambda m: jnp.asarray(np.tile(m.astype(np.float32), (3, 1)), dtype=BF16)
    sg0 = stack3((d[None, :] < ROT_DIM) & (d[None, :] % half == j))
    sg1 = stack3(-((d[None, :] < half) & (d[None, :] == j)).astype(np.float32))
    sg2 = stack3((d[None, :] >= half) & (d[None, :] < ROT_DIM) & (d[None, :] - half == j))
    one = np.where(d >= ROT_DIM, 1.0, 0.0).astype(np.float32)[None, :]
    row = lambda w: pl.BlockSpec((tm, w), lambda i: (i, 0))
    col = pl.BlockSpec((128, tm), lambda i: (0, i))
    out_shape = [jax.ShapeDtypeStruct((NSA_WIDTH, T), BF16),
                 jax.ShapeDtypeStruct((T, 128), BF16), jax.ShapeDtypeStruct((T, 128), BF16),
                 jax.ShapeDtypeStruct((T, 128), BF16), jax.ShapeDtypeStruct((128, T), BF16),
                 jax.ShapeDtypeStruct((T, 128), BF16), jax.ShapeDtypeStruct((128, T), BF16),
                 jax.ShapeDtypeStruct((T, 128), F32)]
    return pl.pallas_call(
        _inproj_kernel,
        grid=(T // tm,),
        in_specs=[row(D_MODEL), pl.BlockSpec((1, 1, tm), lambda i: (i, 0, 0)), _resident((1, D_MODEL)),
                  _resident(wq.shape), _resident(wkc.shape), _resident(wkv.shape), _resident(wgn.shape),
                  _resident((half, 1)), _resident((3 * half, LANES)), _resident((3 * half, LANES)),
                  _resident((3 * half, LANES)), _resident((1, LANES))],
        out_specs=[pl.BlockSpec((NSA_WIDTH, tm), lambda i: (0, i)), row(128), row(128), row(128), col, row(128),
                   col, row(128)],
        out_shape=out_shape,
        compiler_params=pltpu.CompilerParams(dimension_semantics=("parallel",), vmem_limit_bytes=VMEM_LIMIT),
        name="in_proj",
    )(x2, pos.reshape(T // tm, 1, tm), g, wq, wkc, wkv, wgn, frq, sg0, sg1, sg2, jnp.asarray(one))


def _compress_kernel(c_ref, pet_ref, peb_ref, wt_ref, wb_ref, w2_ref, o_ref):
    c = c_ref[0].astype(F32)
    a = _dot_bf16(c + pet_ref[...], wt_ref[...])
    b = _dot_bf16(c + peb_ref[...], wb_ref[...])
    n = c.shape[0]
    hid = a + pltpu.roll(b, n - 1, 0)
    o_ref[0] = _dot_bf16(_gelu_tanh(hid), w2_ref[...])


def _compress(kv, pe, w1, w2):
    B, S, _ = kv.shape
    n = S // CMP_STRIDE
    G = NSA_KV_GROUPS
    half = CMP_BLOCK // 2
    eye = jnp.eye(G, dtype=F32)

    def blockdiag_w1(w):
        w = w.reshape(half, 1, HEAD_DIM, 1, CMP_HIDDEN) * eye.reshape(1, G, 1, G, 1)
        return w.reshape(half * G * HEAD_DIM, G * CMP_HIDDEN)

    wt = blockdiag_w1(w1[:half * HEAD_DIM]).astype(BF16)
    wb = blockdiag_w1(w1[half * HEAD_DIM:]).astype(BF16)
    w2b = ((w2.reshape(1, CMP_HIDDEN, 1, HEAD_DIM) * eye.reshape(G, 1, G, 1))
           .reshape(G * CMP_HIDDEN, G * HEAD_DIM).astype(BF16))
    pet = jnp.broadcast_to(pe[:half, None, :], (half, G, HEAD_DIM)).reshape(1, half * G * HEAD_DIM)
    peb = jnp.broadcast_to(pe[half:, None, :], (half, G, HEAD_DIM)).reshape(1, half * G * HEAD_DIM)
    c = kv.reshape(B, n, CMP_STRIDE * 128)
    return pl.pallas_call(
        _compress_kernel,
        grid=(B,),
        in_specs=[pl.BlockSpec((1, n, CMP_STRIDE * 128), lambda b: (b, 0, 0)),
                  _resident(pet.shape), _resident(peb.shape), _resident(wt.shape), _resident(wb.shape),
                  _resident(w2b.shape)],
        out_specs=pl.BlockSpec((1, n, 128), lambda b: (b, 0, 0)),
        out_shape=jax.ShapeDtypeStruct((B, n, 128), F32),
        compiler_params=pltpu.CompilerParams(dimension_semantics=("parallel",), vmem_limit_bytes=VMEM_LIMIT),
        name="compress",
    )(c, pet, peb, wt, wb, w2b)


def _memkv_kernel(m_ref, g_ref, w_ref, o_ref):
    mn = _rms(m_ref[0], g_ref[...])
    o_ref[0] = jnp.dot(mn.astype(BF16), w_ref[...], preferred_element_type=F32).astype(BF16)


def _mem_kv(mem, g, w):
    B, M, _ = mem.shape
    return pl.pallas_call(
        _memkv_kernel,
        grid=(B,),
        in_specs=[pl.BlockSpec((1, M, D_MODEL), lambda b: (b, 0, 0)), _resident((1, D_MODEL)),
                  _resident(w.shape)],
        out_specs=pl.BlockSpec((1, M, 2 * XA_WIDTH), lambda b: (b, 0, 0)),
        out_shape=jax.ShapeDtypeStruct((B, M, 2 * XA_WIDTH), BF16),
        compiler_params=pltpu.CompilerParams(dimension_semantics=("parallel",), vmem_limit_bytes=VMEM_LIMIT),
        name="mem_kv",
    )(mem, g, w)


def _nsa_kernel(q_ref, kcc_ref, vcc_ref, ks_ref, vst_ref, kw_ref, vwt_ref, gn_ref, ovt_ref, blk_ref, o_ref,
                qat_ref, oct_ref, ost_ref, owt_ref, m_ref, l_ref, ap_ref, pvp_ref, over_ref, imp_ref):
    qt = pl.program_id(1)
    q0 = qt * Q_TILE
    n_cmp = kcc_ref.shape[1]
    head_cols = [slice(h * Q_TILE, (h + 1) * Q_TILE) for h in range(NSA_HEADS)]
    group_cols = [slice(g * HEADS_PER_GROUP * Q_TILE, (g + 1) * HEADS_PER_GROUP * Q_TILE)
                  for g in range(NSA_KV_GROUPS)]
    per_head = lambda a, n: jnp.concatenate([a] * n, axis=1)

    def masked(s, bias):
        return jnp.concatenate([s[:, c:c + Q_TILE] + bias for c in range(0, s.shape[1], Q_TILE)], axis=1)

    def value_product(vt_ref, start, n, p):
        out = []
        for g in range(NSA_KV_GROUPS):
            lhs = jnp.concatenate([vt_ref[g * HEAD_DIM:(g + 1) * HEAD_DIM, pl.ds(start, n)],
                                   jnp.ones((ONES_ROWS, n), BF16)], axis=0)
            out.append(jnp.dot(lhs, p[:, group_cols[g]], preferred_element_type=F32))
        return jnp.concatenate(out, axis=1)

    zeros = jnp.zeros((HEAD_DIM, Q_TILE), BF16)
    for h in range(NSA_HEADS):
        part = q_ref[h * HEAD_DIM:(h + 1) * HEAD_DIM, :]
        pair = [part, zeros] if h // HEADS_PER_GROUP == 0 else [zeros, part]
        qat_ref[0:LANES, head_cols[h]] = jnp.concatenate(pair, axis=0)

    def compressed_branch(n_rows):
        crow = lax.broadcasted_iota(jnp.int32, (n_rows, Q_TILE), 0)
        cq = lax.broadcasted_iota(jnp.int32, (n_rows, Q_TILE), 1) + q0
        valid_c = crow * CMP_STRIDE + (CMP_BLOCK - 1) <= cq
        bias_c = jnp.where(valid_c, 0.0, SCORE_FLOOR)
        seen_c = per_head(jnp.where(cq[0:1, :] >= CMP_BLOCK - 1, 1.0, 0.0), HEADS_PER_GROUP)
        kcc = kcc_ref[0, 0:n_rows, :].astype(BF16)
        vcct = vcc_ref[0, 0:n_rows, :].T.astype(BF16)
        ones_ov = jnp.concatenate([jnp.ones((ONES_ROWS, n_rows), BF16), ovt_ref[:, 0:n_rows]], axis=0)
        for g in range(NSA_KV_GROUPS):
            cols = group_cols[g]
            s = masked(jnp.dot(kcc, qat_ref[0:LANES, cols], preferred_element_type=F32), bias_c)
            p = jnp.exp2(s - jnp.max(s, axis=0, keepdims=True)).astype(BF16)
            lhs = jnp.concatenate([vcct[g * HEAD_DIM:(g + 1) * HEAD_DIM], ones_ov], axis=0)
            prod = jnp.dot(lhs, p, preferred_element_type=F32)
            norm = seen_c / jnp.maximum(prod[HEAD_DIM:HEAD_DIM + 1], jnp.finfo(F32).tiny)
            oct_ref[:, cols] = prod[0:HEAD_DIM] * norm
            imp_heads = prod[HEAD_DIM + ONES_ROWS:HEAD_DIM + ONES_ROWS + LANES] * norm
            imp = imp_heads[:, 0:Q_TILE]
            for k in range(1, HEADS_PER_GROUP):
                imp = imp + imp_heads[:, k * Q_TILE:(k + 1) * Q_TILE]
            imp_ref[g] = imp

    row_steps = list(range(LANES, n_cmp + 1, LANES))
    for k, n_rows in enumerate(row_steps):
        in_step = (qt * (Q_TILE // CMP_STRIDE) + Q_TILE // CMP_STRIDE - 1) // LANES == k
        pl.when(in_step)(functools.partial(compressed_branch, n_rows))

    brow = lax.broadcasted_iota(jnp.int32, (LANES, Q_TILE), 0)
    cur = (lax.broadcasted_iota(jnp.int32, (LANES, Q_TILE), 1) + q0) // SEL_BLOCK
    forced = (brow == 0) | (brow == cur) | (brow == cur - 1)
    brow_f = brow.astype(F32)
    for g in range(NSA_KV_GROUPS):
        v = jnp.where(brow > cur, -1.0, jnp.where(forced, -2.0, imp_ref[g]))
        for _ in range(SEL_TOPK - SEL_FORCED):
            mx = jnp.max(v, axis=0, keepdims=True)
            first = jnp.min(jnp.where(v == mx, brow_f, float(LANES)), axis=0, keepdims=True)
            v = jnp.where(brow_f == first, -2.0, v)
        neg = jnp.where(v == -2.0, 0.0, MASK_NEG).astype(BF16)
        for h in range(HEADS_PER_GROUP):
            qat_ref[LANES:2 * LANES, head_cols[g * HEADS_PER_GROUP + h]] = neg

    def scores(start, n):
        ka = jnp.concatenate([ks_ref[0, pl.ds(start, n), :], blk_ref[pl.ds(start, n), :]], axis=1)
        return jnp.dot(ka, qat_ref[...], preferred_element_type=F32)

    def online_update(s, start, n):
        m_old = m_ref[...]
        m_new = jnp.maximum(m_old, jnp.max(s, axis=0, keepdims=True))
        alpha = jnp.exp2(m_old - m_new)
        pv = value_product(vst_ref, start, n, jnp.exp2(s - m_new).astype(BF16))
        ost_ref[...] = alpha * ost_ref[...] + pv[0:HEAD_DIM]
        l_ref[...] = alpha * l_ref[...] + pv[HEAD_DIM:HEAD_DIM + 1]
        m_ref[...] = m_new

    qd = pl.multiple_of(q0, Q_TILE)
    drow = lax.broadcasted_iota(jnp.int32, (Q_TILE, Q_TILE), 0)
    dcol = lax.broadcasted_iota(jnp.int32, (Q_TILE, Q_TILE), 1)
    bias_d = jnp.where(drow <= dcol, 0.0, SCORE_FLOOR)
    head = q0 % SEL_KEY_TILE
    n_full = q0 // SEL_KEY_TILE
    head_sizes = range(Q_TILE, SEL_KEY_TILE, Q_TILE)
    tile_start = lambda j: pl.multiple_of(head + j * SEL_KEY_TILE, Q_TILE)

    def start_with_diagonal():
        l_ref[...] = jnp.zeros(l_ref.shape, F32)
        ost_ref[...] = jnp.zeros(ost_ref.shape, F32)
        m_ref[...] = jnp.full(m_ref.shape, SCORE_FLOOR, F32)
        online_update(masked(jnp.dot(ks_ref[0, pl.ds(qd, Q_TILE), :], qat_ref[0:LANES, :],
                                     preferred_element_type=F32), bias_d), qd, Q_TILE)

    start_with_diagonal()
    pvp_ref[...] = jnp.zeros(pvp_ref.shape, F32)
    ap_ref[...] = jnp.ones(ap_ref.shape, F32)
    over_ref[...] = jnp.full(over_ref.shape, SCORE_FLOOR, F32)

    def fold_parked():
        alpha = ap_ref[...]
        ost_ref[...] = alpha * (ost_ref[...] + pvp_ref[0:HEAD_DIM, :])
        l_ref[...] = alpha * (l_ref[...] + pvp_ref[HEAD_DIM:HEAD_DIM + 1, :])

    def lazy_step(start, n):
        fold_parked()
        s = scores(start, n)
        m_old = m_ref[...]
        pvp_ref[...] = value_product(vst_ref, start, n, jnp.exp2(s - m_old).astype(BF16))
        t_max = jnp.max(s, axis=0, keepdims=True)
        m_new = jnp.maximum(m_old, t_max)
        over_ref[...] = jnp.maximum(over_ref[...], t_max - m_old)
        ap_ref[...] = jnp.exp2(m_old - m_new)
        m_ref[...] = m_new

    def lazy_pair(j, carry):
        lazy_step(tile_start(2 * j), SEL_KEY_TILE)
        lazy_step(tile_start(2 * j + 1), SEL_KEY_TILE)
        return carry

    lax.fori_loop(0, n_full // 2, lazy_pair, 0)
    pl.when(n_full % 2 == 1)(lambda: lazy_step(tile_start(n_full - 1), SEL_KEY_TILE))
    for n in head_sizes:
        pl.when(head == n)(functools.partial(lazy_step, 0, n))
    fold_parked()

    @pl.when(jnp.max(over_ref[...]) > LAZY_MAX_HEADROOM)
    def _():
        start_with_diagonal()

        def textbook_tile(j, carry):
            online_update(scores(tile_start(j), SEL_KEY_TILE), tile_start(j), SEL_KEY_TILE)
            return carry

        lax.fori_loop(0, n_full, textbook_tile, 0)
        for n in head_sizes:
            pl.when(head == n)(lambda n=n: online_update(scores(0, n), 0, n))

    wlen = WINDOW + Q_TILE
    w0 = pl.multiple_of(jnp.maximum(q0 - WINDOW, 0), Q_TILE)
    kidx = w0 + lax.broadcasted_iota(jnp.int32, (wlen, Q_TILE), 0)
    tqw = q0 + lax.broadcasted_iota(jnp.int32, (wlen, Q_TILE), 1)
    bias_w = jnp.where((kidx <= tqw) & (tqw - kidx < WINDOW), 0.0, SCORE_FLOOR)
    s = masked(jnp.dot(kw_ref[0, pl.ds(w0, wlen), :], qat_ref[0:LANES, :], preferred_element_type=F32), bias_w)
    o = value_product(vwt_ref, w0, wlen, jnp.exp2(s - jnp.max(s, axis=0, keepdims=True)).astype(BF16))
    owt_ref[...] = o[0:HEAD_DIM] / o[HEAD_DIM:HEAD_DIM + 1]

    gt = gn_ref[0].T
    for j in range(NSA_HEADS // 2):
        parts = []
        for h in (2 * j, 2 * j + 1):
            cols = head_cols[h]
            o_sel = ost_ref[:, cols] / l_ref[:, cols]
            parts.append(gt[3 * h:3 * h + 1, :] * oct_ref[:, cols] + gt[3 * h + 1:3 * h + 2, :] * o_sel
                         + gt[3 * h + 2:3 * h + 3, :] * owt_ref[:, cols])
        o_ref[0, :, j * LANES:(j + 1) * LANES] = jnp.concatenate(parts, axis=0).T.astype(o_ref.dtype)


def _nsa(qt, kcc, vcc, ks, vst, kw, vwt, gn):
    B, S, _ = ks.shape
    n_cmp = kcc.shape[1]
    n_sel = S // SEL_BLOCK
    assert n_sel <= LANES and n_cmp % LANES == 0 and S % SEL_KEY_TILE == 0 and S >= WINDOW + Q_TILE
    ci = np.arange(n_cmp)[None, :] * CMP_STRIDE
    sj = np.arange(LANES)[:, None] * SEL_BLOCK
    overlap_t = ((ci < sj + SEL_BLOCK) & (ci + CMP_BLOCK > sj)).astype(np.float32)
    block_onehot = (np.arange(S)[:, None] // SEL_BLOCK == np.arange(LANES)[None, :])
    block_onehot = jnp.asarray(block_onehot.astype(np.float32), dtype=BF16)
    cols = NSA_HEADS * Q_TILE
    full = lambda n, w: pl.BlockSpec((1, n, w), lambda b, i: (b, 0, 0))
    full_t = pl.BlockSpec((128, S), lambda b, i: (0, b))
    tile = lambda w: pl.BlockSpec((1, Q_TILE, w), lambda b, i: (b, i, 0))
    return pl.pallas_call(
        _nsa_kernel,
        grid=(B, S // Q_TILE),
        in_specs=[pl.BlockSpec((NSA_WIDTH, Q_TILE), lambda b, i: (0, b * (S // Q_TILE) + i)), full(n_cmp, 128), full(n_cmp, 128), full(S, 128), full_t,
                  full(S, 128), full_t, tile(128), _resident(overlap_t.shape), _resident((S, LANES))],
        out_specs=tile(NSA_WIDTH),
        out_shape=jax.ShapeDtypeStruct((B, S, NSA_WIDTH), BF16),
        scratch_shapes=[pltpu.VMEM((2 * LANES, cols), BF16),
                        pltpu.VMEM((HEAD_DIM, cols), F32), pltpu.VMEM((HEAD_DIM, cols), F32),
                        pltpu.VMEM((HEAD_DIM, cols), F32), pltpu.VMEM((1, cols), F32),
                        pltpu.VMEM((1, cols), F32), pltpu.VMEM((1, cols), F32),
                        pltpu.VMEM((HEAD_DIM + ONES_ROWS, cols), F32), pltpu.VMEM((1, cols), F32),
                        pltpu.VMEM((NSA_KV_GROUPS, LANES, Q_TILE), F32)],
        compiler_params=pltpu.CompilerParams(dimension_semantics=("parallel", "parallel"),
                                             vmem_limit_bytes=VMEM_LIMIT),
        name="nsa",
    )(qt, kcc, vcc, ks, vst, kw, vwt, gn, jnp.asarray(overlap_t, dtype=BF16), block_onehot)


POOL_HALO = 16


def _merge_kernel(x_ref, xh_ref, yn_ref, kvm_ref, g_ref, wpool_ref, wqx_ref, wgbr_ref, poolw_ref, pscale_ref,
                  wbp_ref, wbn_ref, wbx_ref, wout_ref, gpost_ref, o_ref, *, S):
    i = pl.program_id(0)
    tm = x_ref.shape[0]
    t0 = (i * tm) % S
    x = x_ref[...]
    g = g_ref[...]
    h = _rms(x, g)
    hb = h.astype(BF16)

    hh = _rms(xh_ref[...], g).astype(BF16)
    u_halo = jnp.dot(hh, wpool_ref[...], preferred_element_type=F32)
    u_halo = jnp.where(t0 > 0, u_halo, 0.0)
    u = jnp.dot(hb, wpool_ref[...], preferred_element_type=F32)
    ue = jnp.concatenate([u_halo, u], axis=0)
    trow = t0 + lax.broadcasted_iota(jnp.int32, (tm, 1), 0) + 1
    ypool = []
    for gi, w in enumerate(POOL_WINDOWS):
        acc = ue[:, gi * POOL_GROUP:(gi + 1) * POOL_GROUP]
        step = 1
        while step < w:
            acc = acc + pltpu.roll(acc, step, 0)
            step *= 2
        cnt = jnp.minimum(trow, w).astype(F32)
        p = acc[POOL_HALO:, :] / cnt - u[:, gi * POOL_GROUP:(gi + 1) * POOL_GROUP]
        ypool.append(jnp.dot(p.astype(BF16), poolw_ref[gi], preferred_element_type=F32))
    ypool = jnp.concatenate(ypool, axis=1) * pscale_ref[...]

    qx = jnp.dot(hb, wqx_ref[...], preferred_element_type=F32)
    ymem = []
    for hd in range(XA_HEADS):
        km = kvm_ref[0, :, hd * XA_HEAD_DIM:(hd + 1) * XA_HEAD_DIM]
        vm = kvm_ref[0, :, XA_WIDTH + hd * XA_HEAD_DIM:XA_WIDTH + (hd + 1) * XA_HEAD_DIM]
        s = _dot_nt(qx[:, hd * XA_HEAD_DIM:(hd + 1) * XA_HEAD_DIM].astype(BF16), km) * (XA_HEAD_DIM ** -0.5)
        e = jnp.exp(s - jnp.max(s, axis=-1, keepdims=True))
        p = e / jnp.sum(e, axis=-1, keepdims=True)
        ymem.append(jnp.dot(p.astype(BF16), vm, preferred_element_type=F32))
    ymem = jnp.concatenate(ymem, axis=1)

    gbr = jax.nn.sigmoid(jnp.dot(hb, wgbr_ref[...], preferred_element_type=F32))
    y = (gbr[:, 0:D_MODEL] * jnp.dot(ypool.astype(BF16), wbp_ref[...], preferred_element_type=F32)
         + gbr[:, D_MODEL:2 * D_MODEL] * jnp.dot(yn_ref[...], wbn_ref[...], preferred_element_type=F32)
         + gbr[:, 2 * D_MODEL:3 * D_MODEL] * jnp.dot(ymem.astype(BF16), wbx_ref[...], preferred_element_type=F32))
    o = jnp.dot(y.astype(BF16), wout_ref[...], preferred_element_type=F32)
    o_ref[...] = x + _rms(o, gpost_ref[...])


def _merge(x2, ynsa, kvm, g, wpool, wqx, wgbr, poolw, pscale, wbp, wbn, wbx, wout, gpost, S, tm=1024):
    T = x2.shape[0]
    M = kvm.shape[1]
    row = lambda w: pl.BlockSpec((tm, w), lambda i: (i, 0))
    halo = pl.BlockSpec((POOL_HALO, D_MODEL), lambda i: (jnp.maximum(i * (tm // POOL_HALO) - 1, 0), 0))
    return pl.pallas_call(
        functools.partial(_merge_kernel, S=S),
        grid=(T // tm,),
        in_specs=[row(D_MODEL), halo, row(NSA_WIDTH),
                  pl.BlockSpec((1, M, 2 * XA_WIDTH), lambda i: ((i * tm) // S, 0, 0)),
                  _resident((1, D_MODEL)), _resident(wpool.shape), _resident(wqx.shape), _resident(wgbr.shape),
                  _resident(poolw.shape), _resident(pscale.shape), _resident(wbp.shape), _resident(wbn.shape),
                  _resident(wbx.shape), _resident(wout.shape), _resident((1, D_MODEL))],
        out_specs=row(D_MODEL),
        out_shape=jax.ShapeDtypeStruct((T, D_MODEL), F32),
        compiler_params=pltpu.CompilerParams(dimension_semantics=("parallel",), vmem_limit_bytes=VMEM_LIMIT),
        name="merge",
    )(x2, x2, ynsa, kvm, g, wpool, wqx, wgbr, poolw, pscale, wbp, wbn, wbx, wout, gpost)


FFN_HALO = 8
FFN_CHUNK = 2816


def _ffn_kernel(x_ref, xh_ref, g_ref, wup_ref, cw_ref, cb_ref, wdn_ref, gpost_ref, o_ref, *, S):
    i = pl.program_id(0)
    tm = x_ref.shape[0]
    t0 = (i * tm) % S
    x = x_ref[...]
    g = g_ref[...]
    hh = jnp.where(t0 > 0, _rms(xh_ref[...], g), 0.0)
    he = jnp.concatenate([hh, _rms(x, g)], axis=0).astype(BF16)

    def conv(col0):
        u = jnp.dot(he, wup_ref[:, col0:col0 + FFN_CHUNK], preferred_element_type=F32)
        c = cb_ref[:, col0:col0 + FFN_CHUNK] + cw_ref[CONV_WIDTH - 1:CONV_WIDTH, col0:col0 + FFN_CHUNK] * u
        for k in range(1, CONV_WIDTH):
            tap = cw_ref[CONV_WIDTH - 1 - k:CONV_WIDTH - k, col0:col0 + FFN_CHUNK]
            c = c + tap * pltpu.roll(u, k, 0)
        return c[FFN_HALO:, :]

    f = jnp.zeros((tm, D_MODEL), F32)
    for j in range(D_FF // FFN_CHUNK):
        act = _gelu_tanh(conv(j * FFN_CHUNK)) * conv(D_FF + j * FFN_CHUNK)
        f = f + jnp.dot(act.astype(BF16), wdn_ref[j * FFN_CHUNK:(j + 1) * FFN_CHUNK, :],
                        preferred_element_type=F32)
    o_ref[...] = x + _rms(f, gpost_ref[...])


def _ffn(x2, g, wup, cw, cb, wdn, gpost, S, tm=1024):
    T = x2.shape[0]
    row = pl.BlockSpec((tm, D_MODEL), lambda i: (i, 0))
    halo = pl.BlockSpec((FFN_HALO, D_MODEL), lambda i: (jnp.maximum(i * (tm // FFN_HALO) - 1, 0), 0))
    return pl.pallas_call(
        functools.partial(_ffn_kernel, S=S),
        grid=(T // tm,),
        in_specs=[row, halo, _resident((1, D_MODEL)), _resident(wup.shape), _resident(cw.shape),
                  _resident(cb.shape), _resident(wdn.shape), _resident((1, D_MODEL))],
        out_specs=row,
        out_shape=jax.ShapeDtypeStruct((T, D_MODEL), F32),
        compiler_params=pltpu.CompilerParams(dimension_semantics=("parallel",), vmem_limit_bytes=VMEM_LIMIT),
        name="ffn",
    )(x2, x2, g, wup, cw, cb, wdn, gpost)


def _layer(x2, mem, pos2, B, S, pre_mix_g, w_in, pool_w, pool_scale, cmp_pe, cmp_w1, cmp_w2, mem_norm_g,
           w_mem_kv, w_br_pool, w_br_nsa, w_br_xa, w_out, post_mix_g, pre_ffn_g, w_up, conv_w, conv_b,
           w_down, post_ffn_g):
    o = np.cumsum((0,) + IN_SIZES)
    w_pool, w_q, w_kv, w_gn, w_qx, w_gbr = (w_in[:, o[k]:o[k + 1]] for k in range(6))
    w_gn = jnp.pad(w_gn, ((0, 0), (0, LANES - w_gn.shape[1])))
    vec = lambda a: a.reshape(1, -1)
    q, kc, vc, ks, vst, kw, vwt, gn = _in_proj(x2, pos2, vec(pre_mix_g), w_q.astype(BF16), w_kv[:, :128].astype(BF16),
                                               w_kv[:, 128:].astype(BF16), w_gn.astype(BF16))
    b3 = lambda a: a.reshape(B, S, a.shape[-1])
    kcc = _compress(b3(kc), cmp_pe[0], cmp_w1[0], cmp_w2[0])
    vcc = _compress(b3(vc), cmp_pe[1], cmp_w1[1], cmp_w2[1])
    ynsa = _nsa(q, kcc, vcc, b3(ks), vst, b3(kw), vwt, b3(gn))
    kvm = _mem_kv(mem, vec(mem_norm_g), w_mem_kv.astype(BF16))
    x2 = _merge(x2, ynsa.reshape(B * S, NSA_WIDTH), kvm, vec(pre_mix_g), w_pool.astype(BF16), w_qx.astype(BF16),
                w_gbr.astype(BF16), pool_w.astype(BF16), vec(pool_scale), w_br_pool.astype(BF16),
                w_br_nsa.astype(BF16), w_br_xa.astype(BF16), w_out.astype(BF16), vec(post_mix_g), S)
    return _ffn(x2, vec(pre_ffn_g), w_up.astype(BF16), conv_w, vec(conv_b), w_down.astype(BF16),
                vec(post_ffn_g), S)


def kernel(x, mem, positions, pre_mix_g, w_in, pool_w, pool_scale, cmp_pe, cmp_w1, cmp_w2, mem_norm_g, w_mem_kv,
           w_br_pool, w_br_nsa, w_br_xa, w_out, post_mix_g, pre_ffn_g, w_up, conv_w, conv_b, w_down, post_ffn_g):
    B, S, D = x.shape
    x2 = x.reshape(B * S, D)
    pos2 = positions.reshape(B * S)
    for l in range(pre_mix_g.shape[0]):
        x2 = _layer(x2, mem, pos2, B, S, pre_mix_g[l], w_in[l], pool_w[l], pool_scale[l], cmp_pe[l], cmp_w1[l],
                    cmp_w2[l], mem_norm_g[l], w_mem_kv[l], w_br_pool[l], w_br_nsa[l], w_br_xa[l], w_out[l],
                    post_mix_g[l], pre_ffn_g[l], w_up[l], conv_w[l], conv_b[l], w_down[l], post_ffn_g[l])
    return x2.reshape(B, S, D)
```
